```python
import math
import jax, jax.numpy as jnp
from jax import lax
import numpy as np

D_MODEL = 2048
BATCH = 4
SEQ = 2048
DEPTH = 1
DEC_BATCH = 128
DEC_SEQ = 8
PAST_LEN = 16384
PAGE_SIZE = 128

N_META = 16
CONV_W = 3
D_CONV = D_MODEL // 2
GLA_HEADS = 4
GLA_DK = (D_MODEL // 2) // GLA_HEADS
GLA_DV = D_MODEL // GLA_HEADS
GATE_RANK = 16
GATE_TAU = 16.0
GLA_CHUNK = 16
D_FF = 5632
EPS = 1e-6

SPLIT_SIZES = (D_CONV, D_CONV, D_CONV,
               GLA_HEADS * GLA_DK, GLA_HEADS * GLA_DK,
               GLA_HEADS * GLA_DV, GLA_HEADS * GLA_DV,
               GATE_RANK, D_MODEL, D_MODEL)
IN_COLS = sum(SPLIT_SIZES)
SPLIT_POINTS = tuple(int(p) for p in np.cumsum(SPLIT_SIZES)[:-1])

kernel_name = "hybrid_conv_gla_convffn_step"


def rmsnorm(x, g):
    xf = x.astype(jnp.float32)
    y = xf * lax.rsqrt(jnp.mean(xf * xf, axis=-1, keepdims=True) + EPS)
    return (y * g.astype(jnp.float32)).astype(x.dtype)


def causal_dwconv(x, prev, w):
    L = x.shape[1]
    xp = jnp.concatenate([prev.astype(x.dtype), x], axis=1)
    y = w[0] * xp[:, 0:L] + w[1] * xp[:, 1:L + 1] + w[2] * xp[:, 2:L + 2]
    return y, xp[:, -(CONV_W - 1):]


def gla_chunked(q, k, v, logg, s0, chunk):
    f32 = jnp.float32
    Bsz, L, H, DK = q.shape
    DV = v.shape[-1]
    n = L // chunk
    q = q.astype(f32).reshape(Bsz, n, chunk, H, DK)
    k = k.astype(f32).reshape(Bsz, n, chunk, H, DK)
    v = v.astype(f32).reshape(Bsz, n, chunk, H, DV)
    logg = logg.astype(f32).reshape(Bsz, n, chunk, H, DK)
    b = jnp.cumsum(logg, axis=2)
    b_last = b[:, :, -1]
    qe = q * jnp.exp(b)
    ke = k * jnp.exp(-b)
    kd = k * jnp.exp(b_last[:, :, None] - b)
    mask = jnp.tril(jnp.ones((chunk, chunk), dtype=bool))
    att = jnp.einsum('bnihk,bnjhk->bnhij', qe, ke)
    att = jnp.where(mask, att, 0.0)
    o_intra = jnp.einsum('bnhij,bnjhv->bnihv', att, v)

    def step(S, inp):
        qe_c, kd_c, v_c, dl_c = inp
        o = jnp.einsum('bihk,bhkv->bihv', qe_c, S)
        S = S * dl_c[..., None] + jnp.einsum('bjhk,bjhv->bhkv', kd_c, v_c)
        return S, o

    xs = (jnp.moveaxis(qe, 1, 0), jnp.moveaxis(kd, 1, 0), jnp.moveaxis(v, 1, 0),
          jnp.moveaxis(jnp.exp(b_last), 1, 0))
    S, o_inter = lax.scan(step, s0.astype(f32), xs)
    o = o_intra + jnp.moveaxis(o_inter, 0, 1)
    return o.reshape(Bsz, L, H, DV), S


def layer(x, conv_prev, gla_prev, ffn_prev, chunk,
          norm_mix_g, w_in, conv_mix_w, w_conv_out, w_gate_up, b_gate, gla_norm_g,
          w_gla_out, w_o, norm_ffn_g, w_ffn_up, ffn_conv_w, ffn_conv_b, w_ffn_down):
    Bsz, L, _ = x.shape
    n = rmsnorm(x, norm_mix_g)
    proj = n @ w_in
    cb, cc, ch, q, k, v, g, a_lr, gate_a, gate_b = jnp.split(proj, SPLIT_POINTS, axis=-1)
    uc, conv_new = causal_dwconv(cc * ch, conv_prev, conv_mix_w)
    y_a = (cb * uc) @ w_conv_out
    logit = (a_lr @ w_gate_up + b_gate).astype(jnp.float32)
    logg = jax.nn.log_sigmoid(logit) / GATE_TAU
    o, S = gla_chunked((q * GLA_DK ** -0.5).reshape(Bsz, L, GLA_HEADS, GLA_DK),
                       k.reshape(Bsz, L, GLA_HEADS, GLA_DK),
                       v.reshape(Bsz, L, GLA_HEADS, GLA_DV),
                       logg.reshape(Bsz, L, GLA_HEADS, GLA_DK), gla_prev, chunk)
    o = rmsnorm(o, gla_norm_g).astype(x.dtype).reshape(Bsz, L, GLA_HEADS * GLA_DV)
    y_b = (o * jax.nn.silu(g)) @ w_gla_out
    merged = jax.nn.sigmoid(gate_a) * y_a + jax.nn.sigmoid(gate_b) * y_b
    h = x + merged @ w_o
    n2 = rmsnorm(h, norm_ffn_g)
    a, gt = jnp.split(n2 @ w_ffn_up, 2, axis=-1)
    gc, ffn_new = causal_dwconv(gt, ffn_prev, ffn_conv_w)
    ff = (jax.nn.silu(gc + ffn_conv_b) * a) @ w_ffn_down
    return h + ff, conv_new, S, ffn_new


def setup_inputs(seed: int = 0) -> dict:
    key = jax.random.key(seed)
    ks = jax.random.split(key, 24)
    nrm = jax.random.normal
    f32 = jnp.float32
    return {
        "x_prompt": nrm(ks[0], (BATCH, SEQ, D_MODEL), f32),
        "x_sample": nrm(ks[1], (DEC_BATCH, DEC_SEQ, D_MODEL), f32),
        "state_conv": nrm(ks[2], (DEPTH, DEC_BATCH, CONV_W - 1, D_CONV), f32) * 0.5,
        "state_gla": nrm(ks[3], (DEPTH, DEC_BATCH, GLA_HEADS, GLA_DK, GLA_DV), f32) * 0.05,
        "state_ffn_conv": nrm(ks[4], (DEPTH, DEC_BATCH, CONV_W - 1, D_FF), f32),
        "meta_tokens": nrm(ks[5], (N_META, D_MODEL), f32),
        "norm_mix_g": 1.0 + 0.02 * nrm(ks[6], (DEPTH, D_MODEL), f32),
        "w_in": nrm(ks[7], (DEPTH, D_MODEL, IN_COLS), f32) * D_MODEL ** -0.5,
        "conv_mix_w": nrm(ks[8], (DEPTH, CONV_W, D_CONV), f32) * CONV_W ** -0.5,
        "w_conv_out": nrm(ks[9], (DEPTH, D_CONV, D_MODEL), f32) * D_CONV ** -0.5,
        "w_gate_up": nrm(ks[10], (DEPTH, GATE_RANK, GLA_HEADS * GLA_DK), f32) * GATE_RANK ** -0.5,
        "b_gate": 0.1 * nrm(ks[11], (DEPTH, GLA_HEADS * GLA_DK), f32),
        "gla_norm_g": 1.0 + 0.02 * nrm(ks[12], (DEPTH, GLA_DV), f32),
        "w_gla_out": nrm(ks[13], (DEPTH, GLA_HEADS * GLA_DV, D_MODEL), f32) * (GLA_HEADS * GLA_DV) ** -0.5,
        "w_o": nrm(ks[14], (DEPTH, D_MODEL, D_MODEL), f32) * D_MODEL ** -0.5,
        "norm_ffn_g": 1.0 + 0.02 * nrm(ks[15], (DEPTH, D_MODEL), f32),
        "w_ffn_up": nrm(ks[16], (DEPTH, D_MODEL, 2 * D_FF), f32) * D_MODEL ** -0.5,
        "ffn_conv_w": nrm(ks[17], (DEPTH, CONV_W, D_FF), f32) * CONV_W ** -0.5,
        "ffn_conv_b": 0.02 * nrm(ks[18], (DEPTH, D_FF), f32),
        "w_ffn_down": nrm(ks[19], (DEPTH, D_FF, D_MODEL), f32) * D_FF ** -0.5,
        "final_norm_g": 1.0 + 0.02 * nrm(ks[20], (D_MODEL,), f32),
    }


def reference(x_prompt, x_sample, state_conv, state_gla, state_ffn_conv, meta_tokens,
              norm_mix_g, w_in, conv_mix_w, w_conv_out, w_gate_up, b_gate, gla_norm_g,
              w_gla_out, w_o, norm_ffn_g, w_ffn_up, ffn_conv_w, ffn_conv_b, w_ffn_down,
              final_norm_g):
    bp = x_prompt.shape[0]
    meta = jnp.broadcast_to(meta_tokens.astype(x_prompt.dtype)[None], (bp, N_META, D_MODEL))
    hp = jnp.concatenate([meta, x_prompt], axis=1)
    hs = x_sample
    chunk_p = math.gcd(hp.shape[1], GLA_CHUNK)
    chunk_s = math.gcd(hs.shape[1], GLA_CHUNK)
    conv_p, gla_p, ffn_p, conv_s, gla_s, ffn_s = [], [], [], [], [], []
    for l in range(DEPTH):
        pl = (norm_mix_g[l], w_in[l], conv_mix_w[l], w_conv_out[l], w_gate_up[l], b_gate[l],
              gla_norm_g[l], w_gla_out[l], w_o[l], norm_ffn_g[l], w_ffn_up[l], ffn_conv_w[l],
              ffn_conv_b[l], w_ffn_down[l])
        zc = jnp.zeros((bp, CONV_W - 1, D_CONV), hp.dtype)
        zs = jnp.zeros((bp, GLA_HEADS, GLA_DK, GLA_DV), jnp.float32)
        zf = jnp.zeros((bp, CONV_W - 1, D_FF), hp.dtype)
        hp, c1, s1, f1 = layer(hp, zc, zs, zf, chunk_p, *pl)
        hs, c2, s2, f2 = layer(hs, state_conv[l], state_gla[l], state_ffn_conv[l], chunk_s, *pl)
        conv_p.append(c1)
        gla_p.append(s1.astype(x_prompt.dtype))
        ffn_p.append(f1)
        conv_s.append(c2.astype(state_conv.dtype))
        gla_s.append(s2.astype(state_gla.dtype))
        ffn_s.append(f2.astype(state_ffn_conv.dtype))
    y_prompt = rmsnorm(hp[:, N_META:], final_norm_g)
    y_sample = rmsnorm(hs, final_norm_g)
    return (y_prompt, y_sample, jnp.stack(conv_p), jnp.stack(gla_p), jnp.stack(ffn_p),
            jnp.stack(conv_s), jnp.stack(gla_s), jnp.stack(ffn_s))
```

```python
import functools
import math

import jax
import jax.numpy as jnp
from jax import lax
from jax.experimental import pallas as pl
from jax.experimental.pallas import tpu as pltpu

EPS = 1e-6
GATE_TAU = 16.0
CONV_W = 3
GLA_SUB = 16
HALO = 8
V7X_VMEM_LIMIT = 56 * 1024 * 1024
LANES = 128
ROW_TILE = 1024
ROW_TILE_RESIDENT = 512
GLA_CHUNK = 128
CONV_COLS = 256
FFN_COLS = 512
SHORT_SEQS_PER_STEP = 8
F32 = jnp.float32
BF16 = jnp.bfloat16


def _cparams(*sem):
    return pltpu.CompilerParams(dimension_semantics=sem, vmem_limit_bytes=V7X_VMEM_LIMIT)


def _row_tile(rows, target):
    best = None
    for t in range(16, min(rows, target) + 1, 16):
        if rows % t == 0:
            best = t
    return best or rows


def _rms(x, g):
    return x * lax.rsqrt(jnp.mean(x * x, axis=-1, keepdims=True) + EPS) * g


def _sigmoid(x):
    return 1.0 / (1.0 + jnp.exp(-x))


def _dot(a, b):
    return jnp.dot(a, b, preferred_element_type=F32)


class SeqLayout:
    def __init__(self, mode, seq_rows=None, n_short=None, short_len=None):
        self.mode, self.seq_rows, self.n_short, self.short_len = mode, seq_rows, n_short, short_len


def _load_halo(work_ref, init_ref, carry_ref, i, j, tiles_per_seq):
    first = (i % tiles_per_seq) == 0

    @pl.when(first)
    def _():
        work_ref[0:HALO, :] = init_ref[...]

    @pl.when(jnp.logical_not(first))
    def _():
        work_ref[0:HALO, :] = carry_ref[j]


def _store_tail(work_ref, carry_ref, st_ref, i, j, tiles_per_seq, tm):
    carry_ref[j] = work_ref[tm:tm + HALO, :]

    @pl.when((i % tiles_per_seq) == tiles_per_seq - 1)
    def _():
        st_ref[i // tiles_per_seq, j] = work_ref[HALO + tm - (CONV_W - 1):HALO + tm, :]


def _conv3(u, w_ref, work_ref, lay, tm, h1=None, h2=None):
    work_ref[HALO:HALO + tm, :] = u
    um1 = work_ref[HALO - 1:HALO - 1 + tm, :]
    um2 = work_ref[HALO - 2:HALO - 2 + tm, :]
    if lay.mode == "table":
        t = lax.broadcasted_iota(jnp.int32, (tm, 1), 0)
        pos = jnp.where(t < lay.n_short, t % lay.short_len, t - lay.n_short)
        um1 = jnp.where(pos >= 1, um1, h1)
        um2 = jnp.where(pos >= 2, um2, h2)
    w = w_ref[...]
    return w[0:1, :] * um2 + w[1:2, :] * um1 + w[2:3, :] * u


def _inproj_conv_kernel(lay, tm, cw, tiles_per_seq, *refs):
    if lay.mode == "carry":
        (x_ref, g_ref, w_ref, cwt_ref, init_ref, n_ref, ca_ref, st_ref, work_ref, carry_ref) = refs
    else:
        (x_ref, g_ref, w_ref, cwt_ref, h1_ref, h2_ref, n_ref, ca_ref, u_ref, work_ref) = refs
    i, j = pl.program_id(0), pl.program_id(1)

    @pl.when(j == 0)
    def _():
        n_ref[...] = _rms(x_ref[...], g_ref[...]).astype(BF16)

    p = _dot(n_ref[...], w_ref[...])
    u = p[:, cw:2 * cw] * p[:, 2 * cw:]
    if lay.mode == "carry":
        _load_halo(work_ref, init_ref, carry_ref, i, j, tiles_per_seq)
        uc = _conv3(u, cwt_ref, work_ref, lay, tm)
        _store_tail(work_ref, carry_ref, st_ref, i, j, tiles_per_seq, tm)
    else:
        work_ref[0:HALO, :] = jnp.zeros((HALO, cw), F32)
        uc = _conv3(u, cwt_ref, work_ref, lay, tm, h1_ref[...], h2_ref[...])
        u_ref[...] = u
    ca_ref[...] = (p[:, :cw] * uc).astype(BF16)


def _inproj_conv(x, g, w3, conv_w, lay, tm, cw, init=None, h1=None, h2=None):
    rows, d = x.shape
    dc = conv_w.shape[1]
    nj = dc // cw
    tiles_per_seq = (lay.seq_rows // tm) if lay.mode == "carry" else 1
    in_specs = [
        pl.BlockSpec((tm, d), lambda i, j: (i, 0)),
        pl.BlockSpec((1, d), lambda i, j: (0, 0)),
        pl.BlockSpec((d, 3 * cw), lambda i, j: (0, j)),
        pl.BlockSpec((CONV_W, cw), lambda i, j: (0, j)),
    ]
    out_specs = [pl.BlockSpec((tm, d), lambda i, j: (i, 0)), pl.BlockSpec((tm, cw), lambda i, j: (i, j))]
    out_shape = [jax.ShapeDtypeStruct((rows, d), BF16), jax.ShapeDtypeStruct((rows, dc), BF16)]
    scratch = [pltpu.VMEM((HALO + tm, cw), F32)]
    if lay.mode == "carry":
        nseq = rows // lay.seq_rows
        args = (x, g, w3, conv_w, init)
        in_specs.append(pl.BlockSpec((HALO, cw), lambda i, j: (0, j)))
        out_specs.append(pl.BlockSpec((nseq, nj, CONV_W - 1, cw), lambda i, j: (0, 0, 0, 0)))
        out_shape.append(jax.ShapeDtypeStruct((nseq, nj, CONV_W - 1, cw), F32))
        scratch.append(pltpu.VMEM((nj, HALO, cw), F32))
    else:
        args = (x, g, w3, conv_w, h1, h2)
        in_specs += [pl.BlockSpec((tm, cw), lambda i, j: (i, j))] * 2
        out_specs.append(pl.BlockSpec((tm, cw), lambda i, j: (i, j)))
        out_shape.append(jax.ShapeDtypeStruct((rows, dc), F32))
    return pl.pallas_call(
        functools.partial(_inproj_conv_kernel, lay, tm, cw, tiles_per_seq),
        grid=(rows // tm, nj), in_specs=in_specs, out_specs=out_specs, out_shape=out_shape,
        scratch_shapes=scratch, compiler_params=_cparams("arbitrary", "arbitrary"),
        name="inproj_conv_" + lay.mode)(*args)


def _inproj_main_kernel(n_lin, n_silu, q_scale, n_ref, w_ref, lin_ref, gs_ref, sig_ref):
    j = pl.program_id(1)
    p = _dot(n_ref[...], w_ref[...])

    @pl.when(j == 0)
    def _():
        lin_ref[...] = (p * q_scale).astype(BF16)

    @pl.when(jnp.logical_and(j > 0, j < n_lin))
    def _():
        lin_ref[...] = p.astype(BF16)

    @pl.when(jnp.logical_and(j >= n_lin, j < n_lin + n_silu))
    def _():
        gs_ref[...] = (p * _sigmoid(p)).astype(BF16)

    @pl.when(j >= n_lin + n_silu)
    def _():
        sig_ref[...] = _sigmoid(p).astype(BF16)


def _inproj_main(n, w, widths, tm, tn, q_scale):
    rows, d = n.shape
    n_lin, n_silu, n_sig = (wd // tn for wd in widths)
    clampi = lambda j, lo, cnt: jnp.clip(j - lo, 0, cnt - 1)
    return pl.pallas_call(
        functools.partial(_inproj_main_kernel, n_lin, n_silu, q_scale),
        grid=(rows // tm, n_lin + n_silu + n_sig),
        in_specs=[pl.BlockSpec((tm, d), lambda i, j: (i, 0)), pl.BlockSpec((d, tn), lambda i, j: (0, j))],
        out_specs=[pl.BlockSpec((tm, tn), lambda i, j: (i, clampi(j, 0, n_lin))),
                   pl.BlockSpec((tm, tn), lambda i, j: (i, clampi(j, n_lin, n_silu))),
                   pl.BlockSpec((tm, tn), lambda i, j: (i, clampi(j, n_lin + n_silu, n_sig)))],
        out_shape=[jax.ShapeDtypeStruct((rows, wd), BF16) for wd in widths],
        compiler_params=_cparams("arbitrary", "arbitrary"), name="inproj_main")(n, w)


def _gate_kernel(n_ref, wa_ref, wup_ref, b_ref, o_ref):
    a = _dot(n_ref[...], wa_ref[...]).astype(BF16)
    z = _dot(a, wup_ref[...]) + b_ref[...]
    o_ref[...] = (jnp.minimum(z, 0.0) - jnp.log(1.0 + jnp.exp(-jnp.abs(z)))) * (1.0 / GATE_TAU)


def _gate(n, wa, wup, b, tm):
    rows, d = n.shape
    rp, dk = wup.shape
    return pl.pallas_call(
        _gate_kernel, grid=(rows // tm,),
        in_specs=[pl.BlockSpec((tm, d), lambda i: (i, 0)), pl.BlockSpec((d, rp), lambda i: (0, 0)),
                  pl.BlockSpec((rp, dk), lambda i: (0, 0)), pl.BlockSpec((1, dk), lambda i: (0, 0))],
        out_specs=pl.BlockSpec((tm, dk), lambda i: (i, 0)),
        out_shape=jax.ShapeDtypeStruct((rows, dk), F32),
        compiler_params=_cparams("arbitrary"), name="gate_lowrank")(n, wa, wup, b)


def _cumsum_groups(x, sub):
    rows = x.shape[0]
    pos = lax.broadcasted_iota(jnp.int32, (rows, 1), 0) % sub
    s = 1
    while s < sub:
        x = x + jnp.where(pos >= s, pltpu.roll(x, s, 0), 0.0)
        s *= 2
    return x


def _gla_chunk(q, k, v, lg, s, sub):
    c, dk = q.shape
    nsub = c // sub
    bt = _cumsum_groups(lg, sub)
    r = jnp.zeros((1, dk), F32)
    r_sub, b_rows = [], []
    for i in range(nsub):
        r_sub.append(r)
        b_rows.append(bt[i * sub:(i + 1) * sub, :] + r)
        r = r + bt[(i + 1) * sub - 1:(i + 1) * sub, :]
    b = jnp.concatenate(b_rows, axis=0) if nsub > 1 else b_rows[0]
    b_last = r
    row = lax.broadcasted_iota(jnp.int32, (c, 1), 0)
    qe = (q * jnp.exp(b)).astype(BF16)
    kd = (k * jnp.exp(b_last - b)).astype(BF16)
    qt = q * jnp.exp(bt)
    att_rows = []
    for i in range(nsub):
        arg = jnp.where(row < (i + 1) * sub, r_sub[i] - b, -jnp.inf)
        ke = (k * jnp.exp(arg)).astype(BF16)
        a = lax.dot_general(qt[i * sub:(i + 1) * sub, :].astype(BF16), ke,
                            (((1,), (1,)), ((), ())), preferred_element_type=F32)
        col = lax.broadcasted_iota(jnp.int32, (sub, c), 1)
        rloc = lax.broadcasted_iota(jnp.int32, (sub, c), 0)
        att_rows.append(jnp.where(col <= rloc + i * sub, a, 0.0))
    att = (jnp.concatenate(att_rows, axis=0) if nsub > 1 else att_rows[0]).astype(BF16)
    o = _dot(att, v) + _dot(qe, s.astype(BF16))
    upd = lax.dot_general(kd, v, (((0,), (0,)), ((), ())), preferred_element_type=F32)
    dl = jnp.exp(b_last)
    dl_col = jnp.transpose(jnp.broadcast_to(dl, (LANES, dk)))
    dv = s.shape[1]
    s_dec = jnp.concatenate([s[:, n * LANES:(n + 1) * LANES] * dl_col for n in range(dv // LANES)], axis=1)
    return o, s_dec + upd


def _gla_out(o, gn, gs):
    return (_rms(o, gn) * gs.astype(F32)).astype(BF16)


def _gla_seq_kernel(sub, q_ref, k_ref, v_ref, lg_ref, gs_ref, gn_ref, s0_ref, ob_ref, s_ref):
    @pl.when(pl.program_id(2) == 0)
    def _():
        s_ref[...] = s0_ref[...]

    o, s_new = _gla_chunk(q_ref[...].astype(F32), k_ref[...].astype(F32), v_ref[...], lg_ref[...],
                          s_ref[0, 0], sub)
    s_ref[0, 0] = s_new
    ob_ref[...] = _gla_out(o, gn_ref[...], gs_ref[...])


def _gla_seq(lin, logg, gs, gn, s0, heads, dk, dv, nseq, seq_rows, chunk, sub, row0, shared_init):
    nchunk = seq_rows // chunk
    blk0 = row0 // chunk
    rb = lambda b, h, c: blk0 + b * nchunk + c
    kq = (heads * dk) // dk
    kv = (2 * heads * dk) // dv
    return pl.pallas_call(
        functools.partial(_gla_seq_kernel, sub),
        grid=(nseq, heads, nchunk),
        in_specs=[pl.BlockSpec((chunk, dk), lambda b, h, c: (rb(b, h, c), h)),
                  pl.BlockSpec((chunk, dk), lambda b, h, c: (rb(b, h, c), kq + h)),
                  pl.BlockSpec((chunk, dv), lambda b, h, c: (rb(b, h, c), kv + h)),
                  pl.BlockSpec((chunk, dk), lambda b, h, c: (rb(b, h, c), h)),
                  pl.BlockSpec((chunk, dv), lambda b, h, c: (rb(b, h, c), h)),
                  pl.BlockSpec((1, dv), lambda b, h, c: (0, 0)),
                  pl.BlockSpec((1, 1, dk, dv), lambda b, h, c: (0 if shared_init else b, h, 0, 0))],
        out_specs=[pl.BlockSpec((chunk, dv), lambda b, h, c: (b * nchunk + c, h)),
                   pl.BlockSpec((1, 1, dk, dv), lambda b, h, c: (b, h, 0, 0))],
        out_shape=[jax.ShapeDtypeStruct((nseq * seq_rows, heads * dv), BF16),
                   jax.ShapeDtypeStruct((nseq, heads, dk, dv), F32)],
        compiler_params=_cparams("arbitrary", "arbitrary", "arbitrary"),
        name="gla_seq")(lin, lin, lin, logg, gs, gn, s0)


def _gla_short_kernel(nb, sl, q_ref, k_ref, v_ref, lg_ref, gs_ref, gn_ref, s0_ref, ob_ref, s_ref):
    q = q_ref[...].astype(F32)
    k = k_ref[...].astype(F32)
    v = v_ref[...].astype(F32)
    lg = lg_ref[...]
    outs = []
    for n in range(nb):
        rs = slice(n * sl, (n + 1) * sl)
        o, s_new = _gla_chunk(q[rs], k[rs], v[rs].astype(BF16), lg[rs], s0_ref[n, 0], sl)
        s_ref[n, 0] = s_new
        outs.append(o)
    o = jnp.concatenate(outs, axis=0)
    ob_ref[...] = _gla_out(o, gn_ref[...], gs_ref[...])


def _gla_short(lin, logg, gs, gn, s0, heads, dk, dv, nseq, sl, nb):
    rows = nb * sl
    kq = (heads * dk) // dk
    kv = (2 * heads * dk) // dv
    return pl.pallas_call(
        functools.partial(_gla_short_kernel, nb, sl),
        grid=(nseq // nb, heads),
        in_specs=[pl.BlockSpec((rows, dk), lambda b, h: (b, h)),
                  pl.BlockSpec((rows, dk), lambda b, h: (b, kq + h)),
                  pl.BlockSpec((rows, dv), lambda b, h: (b, kv + h)),
                  pl.BlockSpec((rows, dk), lambda b, h: (b, h)),
                  pl.BlockSpec((rows, dv), lambda b, h: (b, h)),
                  pl.BlockSpec((1, dv), lambda b, h: (0, 0)),
                  pl.BlockSpec((nb, 1, dk, dv), lambda b, h: (b, h, 0, 0))],
        out_specs=[pl.BlockSpec((rows, dv), lambda b, h: (b, h)),
                   pl.BlockSpec((nb, 1, dk, dv), lambda b, h: (b, h, 0, 0))],
        out_shape=[jax.ShapeDtypeStruct((nseq * sl, heads * dv), BF16),
                   jax.ShapeDtypeStruct((nseq, heads, dk, dv), F32)],
        compiler_params=_cparams("arbitrary", "arbitrary"),
        name="gla_short")(lin, lin, lin, logg, gs, gn, s0)


def _merge_kernel(ca_ref, ob_ref, ga_ref, gb_ref, wc_ref, wg_ref, o_ref):
    ya = _dot(ca_ref[...], wc_ref[...])
    yb = _dot(ob_ref[...], wg_ref[...])
    o_ref[...] = (ga_ref[...].astype(F32) * ya + gb_ref[...].astype(F32) * yb).astype(BF16)


def _merge(ca, ob, sig, wc, wg, tm, tn):
    rows, dc = ca.shape
    dg, d = wg.shape
    nn = d // tn
    return pl.pallas_call(
        _merge_kernel, grid=(rows // tm, nn),
        in_specs=[pl.BlockSpec((tm, dc), lambda i, j: (i, 0)), pl.BlockSpec((tm, dg), lambda i, j: (i, 0)),
                  pl.BlockSpec((tm, tn), lambda i, j: (i, j)), pl.BlockSpec((tm, tn), lambda i, j: (i, nn + j)),
                  pl.BlockSpec((dc, tn), lambda i, j: (0, j)), pl.BlockSpec((dg, tn), lambda i, j: (0, j))],
        out_specs=pl.BlockSpec((tm, tn), lambda i, j: (i, j)),
        out_shape=jax.ShapeDtypeStruct((rows, d), BF16),
        compiler_params=_cparams("arbitrary", "arbitrary"), name="merge")(ca, ob, sig, sig, wc, wg)


def _oproj_kernel(m_ref, x_ref, w_ref, g_ref, h_ref, n2_ref):
    h = x_ref[...] + _dot(m_ref[...], w_ref[...])
    h_ref[...] = h
    n2_ref[...] = _rms(h, g_ref[...]).astype(BF16)


def _oproj(m, x, w, g, tm):
    rows, d = x.shape
    row = lambda i: (i, 0)
    return pl.pallas_call(
        _oproj_kernel, grid=(rows // tm,),
        in_specs=[pl.BlockSpec((tm, d), row), pl.BlockSpec((tm, d), row),
                  pl.BlockSpec((d, d), lambda i: (0, 0)), pl.BlockSpec((1, d), lambda i: (0, 0))],
        out_specs=[pl.BlockSpec((tm, d), row), pl.BlockSpec((tm, d), row)],
        out_shape=[jax.ShapeDtypeStruct((rows, d), F32), jax.ShapeDtypeStruct((rows, d), BF16)],
        compiler_params=_cparams("arbitrary"), name="oproj")(m, x, w, g)


def _ffn_up_kernel(lay, tm, fw, tiles_per_seq, *refs):
    if lay.mode == "carry":
        (n_ref, w_ref, cwt_ref, b_ref, init_ref, act_ref, st_ref, work_ref, carry_ref) = refs
    else:
        (n_ref, w_ref, cwt_ref, b_ref, h1_ref, h2_ref, act_ref, gt_ref, work_ref) = refs
    i, j = pl.program_id(0), pl.program_id(1)
    p = _dot(n_ref[...], w_ref[...])
    gt = p[:, fw:]
    if lay.mode == "carry":
        _load_halo(work_ref, init_ref, carry_ref, i, j, tiles_per_seq)
        gc = _conv3(gt, cwt_ref, work_ref, lay, tm)
        _store_tail(work_ref, carry_ref, st_ref, i, j, tiles_per_seq, tm)
    else:
        work_ref[0:HALO, :] = jnp.zeros((HALO, fw), F32)
        gc = _conv3(gt, cwt_ref, work_ref, lay, tm, h1_ref[...], h2_ref[...])
        gt_ref[...] = gt
    z = gc + b_ref[...]
    act_ref[...] = (z * _sigmoid(z) * p[:, :fw]).astype(BF16)


def _ffn_up(n2, w2, conv_w, bias, lay, tm, fw, init=None, h1=None, h2=None):
    rows, d = n2.shape
    dff = conv_w.shape[1]
    nj = dff // fw
    tiles_per_seq = (lay.seq_rows // tm) if lay.mode == "carry" else 1
    in_specs = [
        pl.BlockSpec((tm, d), lambda i, j: (i, 0)),
        pl.BlockSpec((d, 2 * fw), lambda i, j: (0, j)),
        pl.BlockSpec((CONV_W, fw), lambda i, j: (0, j)),
        pl.BlockSpec((1, fw), lambda i, j: (0, j)),
    ]
    out_specs = [pl.BlockSpec((tm, fw), lambda i, j: (i, j))]
    out_shape = [jax.ShapeDtypeStruct((rows, dff), BF16)]
    scratch = [pltpu.VMEM((HALO + tm, fw), F32)]
    if lay.mode == "carry":
        nseq = rows // lay.seq_rows
        args = (n2, w2, conv_w, bias, init)
        in_specs.append(pl.BlockSpec((HALO, fw), lambda i, j: (0, j)))
        out_specs.append(pl.BlockSpec((nseq, nj, CONV_W - 1, fw), lambda i, j: (0, 0, 0, 0)))
        out_shape.append(jax.ShapeDtypeStruct((nseq, nj, CONV_W - 1, fw), F32))
        scratch.append(pltpu.VMEM((nj, HALO, fw), F32))
    else:
        args = (n2, w2, conv_w, bias, h1, h2)
        in_specs += [pl.BlockSpec((tm, fw), lambda i, j: (i, j))] * 2
        out_specs.append(pl.BlockSpec((tm, fw), lambda i, j: (i, j)))
        out_shape.append(jax.ShapeDtypeStruct((rows, dff), F32))
    return pl.pallas_call(
        functools.partial(_ffn_up_kernel, lay, tm, fw, tiles_per_seq),
        grid=(rows // tm, nj), in_specs=in_specs, out_specs=out_specs, out_shape=out_shape,
        scratch_shapes=scratch, compiler_params=_cparams("arbitrary", "arbitrary"),
        name="ffn_up_" + lay.mode)(*args)


def _ffn_down_kernel(act_ref, w_ref, h_ref, g_ref, y_ref):
    k = pl.program_id(1)
    part = _dot(act_ref[...], w_ref[...])

    @pl.when(k == 0)
    def _():
        y_ref[...] = h_ref[...] + part

    @pl.when(k > 0)
    def _():
        y_ref[...] += part

    @pl.when(k == pl.num_programs(1) - 1)
    def _():
        y_ref[...] = _rms(y_ref[...], g_ref[...])


def _ffn_down(act, w, h, g, tm, tk):
    rows, dff = act.shape
    d = w.shape[1]
    return pl.pallas_call(
        _ffn_down_kernel, grid=(rows // tm, dff // tk),
        in_specs=[pl.BlockSpec((tm, tk), lambda i, k: (i, k)), pl.BlockSpec((tk, d), lambda i, k: (k, 0)),
                  pl.BlockSpec((tm, d), lambda i, k: (i, 0)), pl.BlockSpec((1, d), lambda i, k: (0, 0))],
        out_specs=pl.BlockSpec((tm, d), lambda i, k: (i, 0)),
        out_shape=jax.ShapeDtypeStruct((rows, d), F32),
        compiler_params=_cparams("arbitrary", "arbitrary"), name="ffn_down")(act, w, h, g)


def _interleave(parts, width):
    d, c = parts[0].shape
    st = jnp.stack([p.reshape(d, c // width, width) for p in parts], axis=2)
    return st.reshape(d, len(parts) * c)


def _halo_tables(state, n_short, short_len, rows):
    nseq, _, c = state.shape
    z = jnp.zeros((nseq, short_len, c), state.dtype)
    h1 = z.at[:, 0].set(state[:, 1]).reshape(n_short, c)
    h2 = z.at[:, 0].set(state[:, 0]).at[:, 1].set(state[:, 1]).reshape(n_short, c)
    pad = ((0, rows - n_short), (0, 0))
    return jnp.pad(h1, pad), jnp.pad(h2, pad)


def _init_rows(state2):
    return jnp.pad(state2, ((HALO - (CONV_W - 1), 0), (0, 0)))


def kernel(x_prompt, x_sample, state_conv, state_gla, state_ffn_conv, meta_tokens, norm_mix_g, w_in, conv_mix_w, w_conv_out, w_gate_up, b_gate, gla_norm_g, w_gla_out, w_o, norm_ffn_g, w_ffn_up, ffn_conv_w, ffn_conv_b, w_ffn_down, final_norm_g):
    bp, seq, d = x_prompt.shape
    bs, sl, _ = x_sample.shape
    depth = w_in.shape[0]
    assert depth == 1, "single-layer step"
    n_meta = meta_tokens.shape[0]
    dc = state_conv.shape[-1]
    _, _, heads, dk, dv = state_gla.shape
    dff = state_ffn_conv.shape[-1]
    rank = w_gate_up.shape[1]
    assert n_meta % GLA_SUB == 0 and seq % GLA_SUB == 0 and GLA_SUB % sl == 0

    wi = w_in[0]
    o_cb, o_cc, o_ch, o_q = 0, dc, 2 * dc, 3 * dc
    o_k = o_q + heads * dk
    o_v = o_k + heads * dk
    o_g = o_v + heads * dv
    o_a = o_g + heads * dv
    o_ga = o_a + rank
    o_gb = o_ga + d
    cw = CONV_COLS
    w3 =_interleave([wi[:, o_cb:o_cc], wi[:, o_cc:o_ch], wi[:, o_ch:o_q]], cw).astype(BF16)
    w_main = jnp.concatenate([wi[:, o_q:o_a], wi[:, o_ga:]], axis=1).astype(BF16)
    rp = LANES
    wa =jnp.pad(wi[:, o_a:o_ga], ((0, 0), (0, rp - rank))).astype(BF16)
    wup = jnp.pad(w_gate_up[0], ((0, rp - rank), (0, 0))).astype(BF16)
    fw = FFN_COLS
    wu =w_ffn_up[0]
    w2 = _interleave([wu[:, :dff], wu[:, dff:]], fw).astype(BF16)
    wc = w_conv_out[0].astype(BF16)
    wg = w_gla_out[0].astype(BF16)
    wo = w_o[0].astype(BF16)
    wd = w_ffn_down[0].astype(BF16)
    g1 = norm_mix_g[0][None]
    g2 = norm_ffn_g[0][None]
    gf = final_norm_g[None]
    gn = gla_norm_g[0][None]
    bg = b_gate[0][None]
    fb = ffn_conv_b[0][None]
    cmw = conv_mix_w[0]
    fcw = ffn_conv_w[0]
    tn = heads * dk
    widths = (2 * heads * dk + heads * dv, heads * dv, 2 * d)
    q_scale = float(dk) ** -0.5

    n_short = bs * sl
    rows_s = n_short + n_meta
    xs = jnp.concatenate([x_sample.reshape(n_short, d), meta_tokens.astype(x_sample.dtype)], axis=0)
    lay_s = SeqLayout("table", n_short=n_short, short_len=sl)
    tm_s = rows_s
    h1, h2 = _halo_tables(state_conv[0], n_short, sl, rows_s)
    n_s, ca_s, u_s = _inproj_conv(xs, g1, w3, cmw, lay_s, tm_s, cw, h1=h1, h2=h2)
    lin_s, gs_s, sig_s = _inproj_main(n_s, w_main, widths, tm_s, tn, q_scale)
    logg_s = _gate(n_s, wa, wup, bg, tm_s)
    s_zero = jnp.zeros((1, heads, dk, dv), F32)
    ob_m, s_meta = _gla_seq(lin_s, logg_s, gs_s, gn, s_zero, heads, dk, dv, 1, n_meta, n_meta, GLA_SUB,
                            n_short, True)
    ob_smp, s_smp = _gla_short(lin_s, logg_s, gs_s, gn, state_gla[0], heads, dk, dv, bs, sl,
                               SHORT_SEQS_PER_STEP)
    ob_s = jnp.concatenate([ob_smp, ob_m], axis=0)
    m_s = _merge(ca_s, ob_s, sig_s, wc, wg, tm_s, tn)
    h_s, n2_s = _oproj(m_s, xs, wo, g2, _row_tile(rows_s, ROW_TILE_RESIDENT))
    f1, f2 = _halo_tables(state_ffn_conv[0], n_short, sl, rows_s)
    act_s, gt_s = _ffn_up(n2_s, w2, fcw, fb, lay_s, tm_s, fw, h1=f1, h2=f2)
    y_s = _ffn_down(act_s, wd, h_s, gf, tm_s, fw)

    rows_p = bp * seq
    xp = x_prompt.reshape(rows_p, d)
    lay_p = SeqLayout("carry", seq_rows=seq)
    tm_p = _row_tile(seq, ROW_TILE)
    n_p, ca_p, conv_p = _inproj_conv(xp, g1, w3, cmw, lay_p, tm_p, cw, init=_init_rows(u_s[rows_s - 2:]))
    lin_p, gs_p, sig_p = _inproj_main(n_p, w_main, widths, tm_p, tn, q_scale)
    logg_p = _gate(n_p, wa, wup, bg, tm_p)
    ob_p, s_p = _gla_seq(lin_p, logg_p, gs_p, gn, s_meta, heads, dk, dv, bp, seq, _row_tile(seq, GLA_CHUNK),
                         GLA_SUB, 0, True)
    m_p = _merge(ca_p, ob_p, sig_p, wc, wg, tm_p, tn)
    tm_o = _row_tile(seq, ROW_TILE_RESIDENT)
    h_p, n2_p = _oproj(m_p, xp, wo, g2, tm_o)
    act_p, ffn_p = _ffn_up(n2_p, w2, fcw, fb, lay_p, tm_p, fw, init=_init_rows(gt_s[rows_s - 2:]))
    y_p = _ffn_down(act_p, wd, h_p, gf, tm_o, dff // 4)

    u_smp = u_s[:n_short].reshape(bs, sl, dc)
    gt_smp = gt_s[:n_short].reshape(bs, sl, dff)
    if sl >= CONV_W - 1:
        conv_s = u_smp[:, sl - (CONV_W - 1):]
        ffn_s = gt_smp[:, sl - (CONV_W - 1):]
    else:
        conv_s = jnp.concatenate([state_conv[0], u_smp], axis=1)[:, -(CONV_W - 1):]
        ffn_s = jnp.concatenate([state_ffn_conv[0], gt_smp], axis=1)[:, -(CONV_W - 1):]
    untile = lambda st: jnp.swapaxes(st, 1, 2).reshape(st.shape[0], CONV_W - 1, -1)
    return (y_p.reshape(bp, seq, d), y_s[:n_short].reshape(bs, sl, d),
            untile(conv_p)[None], s_p[None], untile(ffn_p)[None], conv_s[None], s_smp[None], ffn_s[None])
```

```python
import functools

import jax
import jax.numpy as jnp
from jax import lax
from jax.experimental import pallas as pl
from jax.experimental.pallas import tpu as pltpu

EPS = 1e-6
GATE_TAU = 16.0
CONV_W = 3
GLA_SUB = 16
HALO = 8
V7X_VMEM_LIMIT = 56 * 1024 * 1024
LANES = 128
ROW_TILE = 1024
ROW_TILE_RESIDENT = 512
GLA_CHUNK = 128
CONV_COLS = 256
FFN_COLS = 512
SHORT_SEQS_PER_STEP = 8
F32 = jnp.float32
BF16 = jnp.bfloat16


def _cparams(*sem):
    return pltpu.CompilerParams(dimension_semantics=sem, vmem_limit_bytes=V7X_VMEM_LIMIT)


def _row_tile(rows, target):
    best = None
    for t in range(16, min(rows, target) + 1, 16):
        if rows % t == 0:
            best = t
    return best or rows


def _rms(x, g):
    return x * lax.rsqrt(jnp.mean(x * x, axis=-1, keepdims=True) + EPS) * g


def _sigmoid(x):
    return 1.0 / (1.0 + jnp.exp(-x))


def _dot(a, b):
    return jnp.dot(a, b, preferred_element_type=F32)


def _rmsnorm_kernel(x_ref, g_ref, n_ref):
    n_ref[...] = _rms(x_ref[...], g_ref[...]).astype(BF16)


def _rmsnorm(x, g, tm):
    rows, d = x.shape
    return pl.pallas_call(
        _rmsnorm_kernel, grid=(rows // tm,),
        in_specs=[pl.BlockSpec((tm, d), lambda i: (i, 0)), pl.BlockSpec((1, d), lambda i: (0, 0))],
        out_specs=pl.BlockSpec((tm, d), lambda i: (i, 0)),
        out_shape=jax.ShapeDtypeStruct((rows, d), BF16),
        compiler_params=_cparams("arbitrary"), name="rmsnorm")(x, g)


class SeqLayout:
    def __init__(self, mode, seq_rows=None, n_short=None, short_len=None):
        self.mode, self.seq_rows, self.n_short, self.short_len = mode, seq_rows, n_short, short_len


def _load_halo(work_ref, init_ref, carry_ref, i, tiles_per_seq):
    first = (i % tiles_per_seq) == 0

    @pl.when(first)
    def _():
        work_ref[0:HALO, :] = init_ref[...]

    @pl.when(jnp.logical_not(first))
    def _():
        work_ref[0:HALO, :] = carry_ref[...]


def _store_tail(work_ref, carry_ref, st_ref, i, j, tiles_per_seq, tm):
    carry_ref[...] = work_ref[tm:tm + HALO, :]

    @pl.when((i % tiles_per_seq) == tiles_per_seq - 1)
    def _():
        st_ref[i // tiles_per_seq, j] = work_ref[HALO + tm - (CONV_W - 1):HALO + tm, :]


def _conv3(u, w_ref, work_ref, lay, tm, prev2_ref=None, prev1_ref=None):
    work_ref[HALO:HALO + tm, :] = u
    um1 = work_ref[HALO - 1:HALO - 1 + tm, :]
    um2 = work_ref[HALO - 2:HALO - 2 + tm, :]
    if lay.mode == "table":
        c = u.shape[1]
        tail = jnp.zeros((tm - lay.n_short, c), F32)
        p2 = jnp.concatenate([prev2_ref[...], tail], axis=0)
        p1 = jnp.concatenate([prev1_ref[...], tail], axis=0)
        t = lax.broadcasted_iota(jnp.int32, (tm, 1), 0)
        pos = jnp.where(t < lay.n_short, t % lay.short_len, t - lay.n_short)
        um1 = jnp.where(pos == 0, p1, um1)
        um2 = jnp.where(pos == 0, p2, jnp.where(pos == 1, p1, um2))
    w = w_ref[...]
    return w[0:1, :] * um2 + w[1:2, :] * um1 + w[2:3, :] * u


def _conv_specs(lay, rows, tm, c, nj, init, prev2, prev1):
    if lay.mode == "carry":
        nseq = rows // lay.seq_rows
        return ((init,), [pl.BlockSpec((HALO, c), lambda j, i: (0, j))],
                pl.BlockSpec((nseq, nj, CONV_W - 1, c), lambda j, i: (0, 0, 0, 0)),
                jax.ShapeDtypeStruct((nseq, nj, CONV_W - 1, c), F32),
                [pltpu.VMEM((HALO, c), F32)])
    return ((prev2, prev1), [pl.BlockSpec((lay.n_short, c), lambda j, i: (0, j))] * 2,
            pl.BlockSpec((tm, c), lambda j, i: (i, j)),
            jax.ShapeDtypeStruct((rows, nj * c), F32), [])


def _inproj_conv_kernel(lay, tm, cw, tiles_per_seq, *refs):
    n_ref, wb_ref, wc_ref, wh_ref, cwt_ref = refs[:5]
    if lay.mode == "carry":
        init_ref, ca_ref, st_ref, wbf_ref, work_ref, carry_ref = refs[5:]
    else:
        prev2_ref, prev1_ref, ca_ref, u_ref, wbf_ref, work_ref = refs[5:]
    j, i = pl.program_id(0), pl.program_id(1)

    @pl.when(i == 0)
    def _():
        wbf_ref[:, 0:cw] = wb_ref[...].astype(BF16)
        wbf_ref[:, cw:2 * cw] = wc_ref[...].astype(BF16)
        wbf_ref[:, 2 * cw:3 * cw] = wh_ref[...].astype(BF16)

    p = _dot(n_ref[...], wbf_ref[...])
    u = p[:, cw:2 * cw] * p[:, 2 * cw:]
    if lay.mode == "carry":
        _load_halo(work_ref, init_ref, carry_ref, i, tiles_per_seq)
        uc = _conv3(u, cwt_ref, work_ref, lay, tm)
        _store_tail(work_ref, carry_ref, st_ref, i, j, tiles_per_seq, tm)
    else:
        work_ref[0:HALO, :] = jnp.zeros((HALO, cw), F32)
        uc = _conv3(u, cwt_ref, work_ref, lay, tm, prev2_ref, prev1_ref)
        u_ref[...] = u
    ca_ref[...] = (p[:, :cw] * uc).astype(BF16)


def _inproj_conv(n, w_in, conv_w, lay, tm, cw, init=None, prev2=None, prev1=None):
    rows, d = n.shape
    dc = conv_w.shape[1]
    nj = dc // cw
    tiles_per_seq = (lay.seq_rows // tm) if lay.mode == "carry" else 1
    xargs, xspecs, xout_spec, xout_shape, xscratch = _conv_specs(lay, rows, tm, cw, nj, init, prev2, prev1)
    wspec = lambda part: pl.BlockSpec((None, d, cw), lambda j, i: (0, 0, part * nj + j))
    return pl.pallas_call(
        functools.partial(_inproj_conv_kernel, lay, tm, cw, tiles_per_seq),
        grid=(nj, rows // tm),
        in_specs=[pl.BlockSpec((tm, d), lambda j, i: (i, 0)), wspec(0), wspec(1), wspec(2),
                  pl.BlockSpec((CONV_W, cw), lambda j, i: (0, j))] + xspecs,
        out_specs=[pl.BlockSpec((tm, cw), lambda j, i: (i, j)), xout_spec],
        out_shape=[jax.ShapeDtypeStruct((rows, dc), BF16), xout_shape],
        scratch_shapes=[pltpu.VMEM((d, 3 * cw), BF16), pltpu.VMEM((HALO + tm, cw), F32)] + xscratch,
        compiler_params=_cparams("arbitrary", "arbitrary"),
        name="inproj_conv_" + lay.mode)(n, w_in, w_in, w_in, conv_w, *xargs)


def _inproj_act_kernel(segs, q_scale, n_ref, w_ref, *refs):
    out_refs, wbf_ref = refs[:len(segs)], refs[len(segs)]
    j, i = pl.program_id(0), pl.program_id(1)

    @pl.when(i == 0)
    def _():
        wbf_ref[...] = w_ref[...].astype(BF16)

    p = _dot(n_ref[...], wbf_ref[...])
    lo = 0
    for (kind, cnt), o_ref in zip(segs, out_refs):
        if kind == "lin":
            @pl.when(j == lo)
            def _(o_ref=o_ref):
                o_ref[...] = (p * q_scale).astype(BF16)

            @pl.when(jnp.logical_and(j > lo, j < lo + cnt))
            def _(o_ref=o_ref):
                o_ref[...] = p.astype(BF16)
        else:
            @pl.when(jnp.logical_and(j >= lo, j < lo + cnt))
            def _(o_ref=o_ref, kind=kind):
                s = _sigmoid(p)
                o_ref[...] = (p * s if kind == "silu" else s).astype(BF16)
        lo += cnt


def _inproj_act(n, w, w_spec, segs, tm, tn, q_scale):
    rows, d = n.shape
    lows = [sum(c for _, c in segs[:s]) for s in range(len(segs))]
    n_row = rows // tm

    def out_spec(lo, cnt):
        def index(j, i):
            row = jnp.where(j < lo, 0, jnp.where(j >= lo + cnt, n_row - 1, i))
            return row, jnp.clip(j - lo, 0, cnt - 1)
        return pl.BlockSpec((tm, tn), index)

    return pl.pallas_call(
        functools.partial(_inproj_act_kernel, segs, q_scale),
        grid=(sum(c for _, c in segs), n_row),
        in_specs=[pl.BlockSpec((tm, d), lambda j, i: (i, 0)), w_spec],
        out_specs=[out_spec(lo, cnt) for lo, (_, cnt) in zip(lows, segs)],
        out_shape=[jax.ShapeDtypeStruct((rows, cnt * tn), BF16) for _, cnt in segs],
        scratch_shapes=[pltpu.VMEM((d, tn), BF16)],
        compiler_params=_cparams("arbitrary", "arbitrary"), name="inproj_act")(n, w)


def _gate_kernel(n_ref, wa_ref, wup_ref, b_ref, o_ref):
    a = _dot(n_ref[...], wa_ref[...]).astype(BF16)
    z = _dot(a, wup_ref[...]) + b_ref[...]
    o_ref[...] = (jnp.minimum(z, 0.0) - jnp.log(1.0 + jnp.exp(-jnp.abs(z)))) * (1.0 / GATE_TAU)


def _gate(n, wa, wup, b, tm):
    rows, d = n.shape
    rp, dk = wup.shape
    return pl.pallas_call(
        _gate_kernel, grid=(rows // tm,),
        in_specs=[pl.BlockSpec((tm, d), lambda i: (i, 0)), pl.BlockSpec((d, rp), lambda i: (0, 0)),
                  pl.BlockSpec((rp, dk), lambda i: (0, 0)), pl.BlockSpec((1, dk), lambda i: (0, 0))],
        out_specs=pl.BlockSpec((tm, dk), lambda i: (i, 0)),
        out_shape=jax.ShapeDtypeStruct((rows, dk), F32),
        compiler_params=_cparams("arbitrary"), name="gate_lowrank")(n, wa, wup, b)


def _cumsum_groups(x, sub):
    rows = x.shape[0]
    pos = lax.broadcasted_iota(jnp.int32, (rows, 1), 0) % sub
    s = 1
    while s < sub:
        x = x + jnp.where(pos >= s, pltpu.roll(x, s, 0), 0.0)
        s *= 2
    return x


def _gla_chunk(q, k, v, lg, s, sub):
    c, dk = q.shape
    nsub = c // sub
    bt = _cumsum_groups(lg, sub)
    r = jnp.zeros((1, dk), F32)
    r_sub, b_rows = [], []
    for i in range(nsub):
        r_sub.append(r)
        b_rows.append(bt[i * sub:(i + 1) * sub, :] + r)
        r = r + bt[(i + 1) * sub - 1:(i + 1) * sub, :]
    b = jnp.concatenate(b_rows, axis=0) if nsub > 1 else b_rows[0]
    b_last = r
    row = lax.broadcasted_iota(jnp.int32, (c, 1), 0)
    qe = (q * jnp.exp(b)).astype(BF16)
    kd = (k * jnp.exp(b_last - b)).astype(BF16)
    qt = q * jnp.exp(bt)
    att_rows = []
    for i in range(nsub):
        arg = jnp.where(row < (i + 1) * sub, r_sub[i] - b, -jnp.inf)
        ke = (k * jnp.exp(arg)).astype(BF16)
        a = lax.dot_general(qt[i * sub:(i + 1) * sub, :].astype(BF16), ke,
                            (((1,), (1,)), ((), ())), preferred_element_type=F32)
        col = lax.broadcasted_iota(jnp.int32, (sub, c), 1)
        rloc = lax.broadcasted_iota(jnp.int32, (sub, c), 0)
        att_rows.append(jnp.where(col <= rloc + i * sub, a, 0.0))
    att = (jnp.concatenate(att_rows, axis=0) if nsub > 1 else att_rows[0]).astype(BF16)
    o = _dot(att, v) + _dot(qe, s.astype(BF16))
    upd = lax.dot_general(kd, v, (((0,), (0,)), ((), ())), preferred_element_type=F32)
    dl = jnp.exp(b_last)
    dl_col = jnp.transpose(jnp.broadcast_to(dl, (LANES, dk)))
    dv = s.shape[1]
    s_dec = jnp.concatenate([s[:, n * LANES:(n + 1) * LANES] * dl_col for n in range(dv // LANES)], axis=1)
    return o, s_dec + upd


def _gla_out(o, gn, gs):
    return (_rms(o, gn) * gs.astype(F32)).astype(BF16)


def _gla_seq_kernel(sub, q_ref, k_ref, v_ref, lg_ref, gs_ref, gn_ref, s0_ref, ob_ref, s_ref):
    @pl.when(pl.program_id(2) == 0)
    def _():
        s_ref[...] = s0_ref[...]

    o, s_new = _gla_chunk(q_ref[...].astype(F32), k_ref[...].astype(F32), v_ref[...], lg_ref[...],
                          s_ref[0, 0], sub)
    s_ref[0, 0] = s_new
    ob_ref[...] = _gla_out(o, gn_ref[...], gs_ref[...])


def _gla_seq(lin, logg, gs, gn, s0, heads, dk, dv, nseq, seq_rows, chunk, sub, row0, shared_init):
    nchunk = seq_rows // chunk
    blk0 = row0 // chunk
    rb = lambda b, h, c: blk0 + b * nchunk + c
    kq = (heads * dk) // dk
    kv = (2 * heads * dk) // dv
    return pl.pallas_call(
        functools.partial(_gla_seq_kernel, sub),
        grid=(nseq, heads, nchunk),
        in_specs=[pl.BlockSpec((chunk, dk), lambda b, h, c: (rb(b, h, c), h)),
                  pl.BlockSpec((chunk, dk), lambda b, h, c: (rb(b, h, c), kq + h)),
                  pl.BlockSpec((chunk, dv), lambda b, h, c: (rb(b, h, c), kv + h)),
                  pl.BlockSpec((chunk, dk), lambda b, h, c: (rb(b, h, c), h)),
                  pl.BlockSpec((chunk, dv), lambda b, h, c: (rb(b, h, c), h)),
                  pl.BlockSpec((1, dv), lambda b, h, c: (0, 0)),
                  pl.BlockSpec((1, 1, dk, dv), lambda b, h, c: (0 if shared_init else b, h, 0, 0))],
        out_specs=[pl.BlockSpec((chunk, dv), lambda b, h, c: (b * nchunk + c, h)),
                   pl.BlockSpec((1, 1, dk, dv), lambda b, h, c: (b, h, 0, 0))],
        out_shape=[jax.ShapeDtypeStruct((nseq * seq_rows, heads * dv), BF16),
                   jax.ShapeDtypeStruct((nseq, heads, dk, dv), F32)],
        compiler_params=_cparams("arbitrary", "arbitrary", "arbitrary"),
        name="gla_seq")(lin, lin, lin, logg, gs, gn, s0)


def _gla_short_kernel(nb, sl, q_ref, k_ref, v_ref, lg_ref, gs_ref, gn_ref, s0_ref, ob_ref, s_ref):
    q = q_ref[...].astype(F32)
    k = k_ref[...].astype(F32)
    v = v_ref[...].astype(F32)
    lg = lg_ref[...]
    outs = []
    for n in range(nb):
        rs = slice(n * sl, (n + 1) * sl)
        o, s_new = _gla_chunk(q[rs], k[rs], v[rs].astype(BF16), lg[rs], s0_ref[n, 0], sl)
        s_ref[n, 0] = s_new
        outs.append(o)
    o = jnp.concatenate(outs, axis=0)
    ob_ref[...] = _gla_out(o, gn_ref[...], gs_ref[...])


def _gla_short(lin, logg, gs, gn, s0, heads, dk, dv, nseq, sl, nb):
    rows = nb * sl
    kq = (heads * dk) // dk
    kv = (2 * heads * dk) // dv
    return pl.pallas_call(
        functools.partial(_gla_short_kernel, nb, sl),
        grid=(nseq // nb, heads),
        in_specs=[pl.BlockSpec((rows, dk), lambda b, h: (b, h)),
                  pl.BlockSpec((rows, dk), lambda b, h: (b, kq + h)),
                  pl.BlockSpec((rows, dv), lambda b, h: (b, kv + h)),
                  pl.BlockSpec((rows, dk), lambda b, h: (b, h)),
                  pl.BlockSpec((rows, dv), lambda b, h: (b, h)),
                  pl.BlockSpec((1, dv), lambda b, h: (0, 0)),
                  pl.BlockSpec((nb, 1, dk, dv), lambda b, h: (b, h, 0, 0))],
        out_specs=[pl.BlockSpec((rows, dv), lambda b, h: (b, h)),
                   pl.BlockSpec((nb, 1, dk, dv), lambda b, h: (b, h, 0, 0))],
        out_shape=[jax.ShapeDtypeStruct((nseq * sl, heads * dv), BF16),
                   jax.ShapeDtypeStruct((nseq, heads, dk, dv), F32)],
        compiler_params=_cparams("arbitrary", "arbitrary"),
        name="gla_short")(lin, lin, lin, logg, gs, gn, s0)


def _merge_kernel(ca_ref, ob_ref, ga_ref, gb_ref, wc_ref, wg_ref, o_ref):
    ya = _dot(ca_ref[...], wc_ref[...])
    yb = _dot(ob_ref[...], wg_ref[...])
    o_ref[...] = (ga_ref[...].astype(F32) * ya + gb_ref[...].astype(F32) * yb).astype(BF16)


def _merge(ca, ob, sig, wc, wg, tm, tn):
    rows, dc = ca.shape
    dg, d = wg.shape
    nn = d // tn
    return pl.pallas_call(
        _merge_kernel, grid=(rows // tm, nn),
        in_specs=[pl.BlockSpec((tm, dc), lambda i, j: (i, 0)), pl.BlockSpec((tm, dg), lambda i, j: (i, 0)),
                  pl.BlockSpec((tm, tn), lambda i, j: (i, j)), pl.BlockSpec((tm, tn), lambda i, j: (i, nn + j)),
                  pl.BlockSpec((dc, tn), lambda i, j: (0, j)), pl.BlockSpec((dg, tn), lambda i, j: (0, j))],
        out_specs=pl.BlockSpec((tm, tn), lambda i, j: (i, j)),
        out_shape=jax.ShapeDtypeStruct((rows, d), BF16),
        compiler_params=_cparams("arbitrary", "arbitrary"), name="merge")(ca, ob, sig, sig, wc, wg)


def _oproj_kernel(m_ref, x_ref, w_ref, g_ref, h_ref, n2_ref):
    h = x_ref[...] + _dot(m_ref[...], w_ref[...])
    h_ref[...] = h
    n2_ref[...] = _rms(h, g_ref[...]).astype(BF16)


def _oproj(m, x, w, g, tm):
    rows, d = x.shape
    row = lambda i: (i, 0)
    return pl.pallas_call(
        _oproj_kernel, grid=(rows // tm,),
        in_specs=[pl.BlockSpec((tm, d), row), pl.BlockSpec((tm, d), row),
                  pl.BlockSpec((d, d), lambda i: (0, 0)), pl.BlockSpec((1, d), lambda i: (0, 0))],
        out_specs=[pl.BlockSpec((tm, d), row), pl.BlockSpec((tm, d), row)],
        out_shape=[jax.ShapeDtypeStruct((rows, d), F32), jax.ShapeDtypeStruct((rows, d), BF16)],
        compiler_params=_cparams("arbitrary"), name="oproj")(m, x, w, g)


def _ffn_up_kernel(lay, tm, fw, tiles_per_seq, *refs):
    n_ref, wa_ref, wg_ref, cwt_ref, b_ref = refs[:5]
    if lay.mode == "carry":
        init_ref, act_ref, st_ref, wbf_ref, work_ref, carry_ref = refs[5:]
    else:
        prev2_ref, prev1_ref, act_ref, gt_ref, wbf_ref, work_ref = refs[5:]
    j, i = pl.program_id(0), pl.program_id(1)

    @pl.when(i == 0)
    def _():
        wbf_ref[:, 0:fw] = wa_ref[...].astype(BF16)
        wbf_ref[:, fw:2 * fw] = wg_ref[...].astype(BF16)

    p = _dot(n_ref[...], wbf_ref[...])
    gt = p[:, fw:]
    if lay.mode == "carry":
        _load_halo(work_ref, init_ref, carry_ref, i, tiles_per_seq)
        gc = _conv3(gt, cwt_ref, work_ref, lay, tm)
        _store_tail(work_ref, carry_ref, st_ref, i, j, tiles_per_seq, tm)
    else:
        work_ref[0:HALO, :] = jnp.zeros((HALO, fw), F32)
        gc = _conv3(gt, cwt_ref, work_ref, lay, tm, prev2_ref, prev1_ref)
        gt_ref[...] = gt
    z = gc + b_ref[...]
    act_ref[...] = (z * _sigmoid(z) * p[:, :fw]).astype(BF16)


def _ffn_up(n2, w_up, conv_w, bias, lay, tm, fw, init=None, prev2=None, prev1=None):
    rows, d = n2.shape
    dff = conv_w.shape[1]
    nj = dff // fw
    tiles_per_seq = (lay.seq_rows // tm) if lay.mode == "carry" else 1
    xargs, xspecs, xout_spec, xout_shape, xscratch = _conv_specs(lay, rows, tm, fw, nj, init, prev2, prev1)
    wspec = lambda part: pl.BlockSpec((None, d, fw), lambda j, i: (0, 0, part * nj + j))
    return pl.pallas_call(
        functools.partial(_ffn_up_kernel, lay, tm, fw, tiles_per_seq),
        grid=(nj, rows // tm),
        in_specs=[pl.BlockSpec((tm, d), lambda j, i: (i, 0)), wspec(0), wspec(1),
                  pl.BlockSpec((CONV_W, fw), lambda j, i: (0, j)), pl.BlockSpec((1, fw), lambda j, i: (0, j))]
                 + xspecs,
        out_specs=[pl.BlockSpec((tm, fw), lambda j, i: (i, j)), xout_spec],
        out_shape=[jax.ShapeDtypeStruct((rows, dff), BF16), xout_shape],
        scratch_shapes=[pltpu.VMEM((d, 2 * fw), BF16), pltpu.VMEM((HALO + tm, fw), F32)] + xscratch,
        compiler_params=_cparams("arbitrary", "arbitrary"),
        name="ffn_up_" + lay.mode)(n2, w_up, w_up, conv_w, bias, *xargs)


def _ffn_down_kernel(act_ref, w_ref, h_ref, g_ref, y_ref):
    k = pl.program_id(1)
    part = _dot(act_ref[...], w_ref[...])

    @pl.when(k == 0)
    def _():
        y_ref[...] = h_ref[...] + part

    @pl.when(k > 0)
    def _():
        y_ref[...] += part

    @pl.when(k == pl.num_programs(1) - 1)
    def _():
        y_ref[...] = _rms(y_ref[...], g_ref[...])


def _ffn_down(act, w, h, g, tm, tk):
    rows, dff = act.shape
    d = w.shape[1]
    return pl.pallas_call(
        _ffn_down_kernel, grid=(rows // tm, dff // tk),
        in_specs=[pl.BlockSpec((tm, tk), lambda i, k: (i, k)), pl.BlockSpec((tk, d), lambda i, k: (k, 0)),
                  pl.BlockSpec((tm, d), lambda i, k: (i, 0)), pl.BlockSpec((1, d), lambda i, k: (0, 0))],
        out_specs=pl.BlockSpec((tm, d), lambda i, k: (i, 0)),
        out_shape=jax.ShapeDtypeStruct((rows, d), F32),
        compiler_params=_cparams("arbitrary", "arbitrary"), name="ffn_down")(act, w, h, g)


def _init_rows(state2):
    return jnp.pad(state2, ((HALO - (CONV_W - 1), 0), (0, 0)))


def kernel(x_prompt, x_sample, state_conv, state_gla, state_ffn_conv, meta_tokens, norm_mix_g, w_in, conv_mix_w, w_conv_out, w_gate_up, b_gate, gla_norm_g, w_gla_out, w_o, norm_ffn_g, w_ffn_up, ffn_conv_w, ffn_conv_b, w_ffn_down, final_norm_g):
    bp, seq, d = x_prompt.shape
    bs, sl, _ = x_sample.shape
    assert w_in.shape[0] == 1, "single-layer step"
    n_meta = meta_tokens.shape[0]
    dc = state_conv.shape[-1]
    _, _, heads, dk, dv = state_gla.shape
    dff = state_ffn_conv.shape[-1]
    rank = w_gate_up.shape[1]
    assert n_meta % GLA_SUB == 0 and seq % GLA_SUB == 0 and GLA_SUB % sl == 0 and sl >= CONV_W - 1

    o_q = 3 * dc
    o_a = o_q + 2 * heads * dk + 2 * heads * dv
    o_ga = o_a + rank
    tn = heads * dk
    assert o_q % tn == 0 and (heads * dv) % tn == 0 and d % tn == 0
    cw, fw = CONV_COLS, FFN_COLS
    segs_qkvg = (("lin", (2 * heads * dk + heads * dv) // tn), ("silu", heads * dv // tn))
    segs_gates = (("sigmoid", 2 * d // tn),)
    spec_qkvg = pl.BlockSpec((None, d, tn), lambda j, i: (0, 0, o_q // tn + j))
    spec_gates = pl.BlockSpec((d, tn), lambda j, i: (0, j))
    q_scale = float(dk) ** -0.5

    w_gates = w_in[0][:, o_ga:]
    wa = jnp.pad(w_in[0][:, o_a:o_ga], ((0, 0), (0, LANES - rank))).astype(BF16)
    wup = jnp.pad(w_gate_up[0], ((0, LANES - rank), (0, 0))).astype(BF16)
    wc = w_conv_out[0].astype(BF16)
    wg = w_gla_out[0].astype(BF16)
    wo = w_o[0].astype(BF16)
    wd = w_ffn_down[0].astype(BF16)
    g1, g2, gf, gn = norm_mix_g[0][None], norm_ffn_g[0][None], final_norm_g[None], gla_norm_g[0][None]
    bg, fb, cmw, fcw = b_gate[0][None], ffn_conv_b[0][None], conv_mix_w[0], ffn_conv_w[0]

    def in_projections(n, tm, lay, **conv_kw):
        ca, conv_out = _inproj_conv(n, w_in, cmw, lay, tm, cw, **conv_kw)
        lin, gs = _inproj_act(n, w_in, spec_qkvg, segs_qkvg, tm, tn, q_scale)
        sig, = _inproj_act(n, w_gates, spec_gates, segs_gates, tm, tn, 1.0)
        return ca, conv_out, lin, gs, sig, _gate(n, wa, wup, bg, tm)

    n_short = bs * sl
    rows_s = n_short + n_meta
    xs = jnp.concatenate([x_sample.reshape(n_short, d), meta_tokens.astype(x_sample.dtype)], axis=0)
    lay_s = SeqLayout("table", n_short=n_short, short_len=sl)
    tm_s = rows_s
    rep = lambda st, r: jnp.repeat(st[:, r], sl, axis=0)
    n_s = _rmsnorm(xs, g1, tm_s)
    ca_s, u_s, lin_s, gs_s, sig_s, logg_s = in_projections(
        n_s, tm_s, lay_s, prev2=rep(state_conv[0], 0), prev1=rep(state_conv[0], 1))
    s_zero = jnp.zeros((1, heads, dk, dv), F32)
    ob_m, s_meta = _gla_seq(lin_s, logg_s, gs_s, gn, s_zero, heads, dk, dv, 1, n_meta, n_meta, GLA_SUB,
                            n_short, True)
    ob_smp, s_smp = _gla_short(lin_s, logg_s, gs_s, gn, state_gla[0], heads, dk, dv, bs, sl,
                               SHORT_SEQS_PER_STEP)
    ob_s = jnp.concatenate([ob_smp, ob_m], axis=0)
    m_s = _merge(ca_s, ob_s, sig_s, wc, wg, tm_s, tn)
    h_s, n2_s = _oproj(m_s, xs, wo, g2, _row_tile(rows_s, ROW_TILE_RESIDENT))
    act_s, gt_s = _ffn_up(n2_s, w_ffn_up, fcw, fb, lay_s, tm_s, fw,
                          prev2=rep(state_ffn_conv[0], 0), prev1=rep(state_ffn_conv[0], 1))
    y_s = _ffn_down(act_s, wd, h_s, gf, tm_s, fw)

    rows_p = bp * seq
    xp = x_prompt.reshape(rows_p, d)
    lay_p = SeqLayout("carry", seq_rows=seq)
    tm_p = _row_tile(seq, ROW_TILE)
    tm_o = _row_tile(seq, ROW_TILE_RESIDENT)
    n_p = _rmsnorm(xp, g1, tm_o)
    ca_p, conv_p, lin_p, gs_p, sig_p, logg_p = in_projections(
        n_p, tm_p, lay_p, init=_init_rows(u_s[rows_s - (CONV_W - 1):]))
    ob_p, s_p = _gla_seq(lin_p, logg_p, gs_p, gn, s_meta, heads, dk, dv, bp, seq, _row_tile(seq, GLA_CHUNK),
                         GLA_SUB, 0, True)
    m_p = _merge(ca_p, ob_p, sig_p, wc, wg, tm_p, tn)
    h_p, n2_p = _oproj(m_p, xp, wo, g2, tm_o)
    act_p, ffn_p = _ffn_up(n2_p, w_ffn_up, fcw, fb, lay_p, tm_p, fw,
                           init=_init_rows(gt_s[rows_s - (CONV_W - 1):]))
    y_p = _ffn_down(act_p, wd, h_p, gf, tm_o, dff // 4)

    last_rows = lambda a, c: a[:n_short].reshape(bs, sl, c)[:, sl - (CONV_W - 1):]
    untile = lambda st: jnp.swapaxes(st, 1, 2).reshape(st.shape[0], CONV_W - 1, -1)
    return (y_p.reshape(bp, seq, d), y_s[:n_short].reshape(bs, sl, d),
            untile(conv_p)[None], s_p[None], untile(ffn_p)[None],
            last_rows(u_s, dc)[None], s_smp[None], last_rows(gt_s, dff)[None])
```

```python
import functools

import jax
import jax.numpy as jnp
from jax import lax
from jax.experimental import pallas as pl
from jax.experimental.pallas import tpu as pltpu

EPS = 1e-6
GATE_TAU = 16.0
CONV_W = 3
GLA_SUB = 16
HALO = 8
V7X_VMEM_LIMIT = 56 * 1024 * 1024
LANES = 128
ROW_TILE = 1024
ROW_TILE_RESIDENT = 512
GLA_CHUNK = 128
CONV_COLS = 256
FFN_COLS = 512
SHORT_SEQS_PER_STEP = 8
SUB_ROWS = 256
CAST_ROWS = 64
F32 = jnp.float32
BF16 = jnp.bfloat16


def _cparams(*sem):
    return pltpu.CompilerParams(dimension_semantics=sem, vmem_limit_bytes=V7X_VMEM_LIMIT)


def _row_tile(rows, target):
    best = None
    for t in range(16, min(rows, target) + 1, 16):
        if rows % t == 0:
            best = t
    return best or rows


def _rms(x, g):
    return x * lax.rsqrt(jnp.mean(x * x, axis=-1, keepdims=True) + EPS) * g


def _sigmoid(x):
    return 1.0 / (1.0 + jnp.exp(-x))


def _dot(a, b):
    return jnp.dot(a, b, preferred_element_type=F32)


def _dot_nt(a, b):
    return lax.dot_general(a, b, (((1,), (1,)), ((), ())), preferred_element_type=F32)


def _cast_into(dst_ref, row0, col0, src_ref):
    rows, cols = src_ref.shape

    def body(r, carry):
        off = pl.multiple_of(r * CAST_ROWS, CAST_ROWS)
        dst_ref[pl.ds(row0 + off, CAST_ROWS), col0:col0 + cols] = src_ref[pl.ds(off, CAST_ROWS), :].astype(BF16)
        return carry

    lax.fori_loop(0, rows // CAST_ROWS, body, 0)


def _rmsnorm_kernel(x_ref, g_ref, n_ref):
    n_ref[...] = _rms(x_ref[...], g_ref[...]).astype(BF16)


def _rmsnorm(x, g, tm):
    rows, d = x.shape
    return pl.pallas_call(
        _rmsnorm_kernel, grid=(rows // tm,),
        in_specs=[pl.BlockSpec((tm, d), lambda i: (i, 0)), pl.BlockSpec((1, d), lambda i: (0, 0))],
        out_specs=pl.BlockSpec((tm, d), lambda i: (i, 0)),
        out_shape=jax.ShapeDtypeStruct((rows, d), BF16),
        compiler_params=_cparams("arbitrary"), name="rmsnorm")(x, g)


class SeqLayout:
    def __init__(self, mode, seq_rows=None, n_short=None, short_len=None):
        self.mode, self.seq_rows, self.n_short, self.short_len = mode, seq_rows, n_short, short_len


def _load_halo(work_ref, init_ref, carry_ref, i, tiles_per_seq):
    first = (i % tiles_per_seq) == 0

    @pl.when(first)
    def _():
        work_ref[0:HALO, :] = init_ref[...]

    @pl.when(jnp.logical_not(first))
    def _():
        work_ref[0:HALO, :] = carry_ref[...]


def _store_tail(work_ref, carry_ref, st_ref, i, j, tiles_per_seq, tm):
    carry_ref[...] = work_ref[tm:tm + HALO, :]

    @pl.when((i % tiles_per_seq) == tiles_per_seq - 1)
    def _():
        st_ref[i // tiles_per_seq, j] = work_ref[HALO + tm - (CONV_W - 1):HALO + tm, :]


def _conv3(u, w_ref, work_ref, lay, r0, prev2_ref=None, prev1_ref=None):
    rs, c = u.shape
    work_ref[HALO + r0:HALO + r0 + rs, :] = u
    um1 = work_ref[HALO - 1 + r0:HALO - 1 + r0 + rs, :]
    um2 = work_ref[HALO - 2 + r0:HALO - 2 + r0 + rs, :]
    if lay.mode == "table":
        take = max(0, min(lay.n_short, r0 + rs) - r0)

        def table_rows(ref):
            parts = ([ref[r0:r0 + take, :]] if take else []) + ([jnp.zeros((rs - take, c), F32)] if take < rs else [])
            return parts[0] if len(parts) == 1 else jnp.concatenate(parts, axis=0)

        p2, p1 = table_rows(prev2_ref), table_rows(prev1_ref)
        t = r0 + lax.broadcasted_iota(jnp.int32, (rs, 1), 0)
        pos = jnp.where(t < lay.n_short, t % lay.short_len, t - lay.n_short)
        um1 = jnp.where(pos == 0, p1, um1)
        um2 = jnp.where(pos == 0, p2, jnp.where(pos == 1, p1, um2))
    w = w_ref[...]
    return w[0:1, :] * um2 + w[1:2, :] * um1 + w[2:3, :] * u


def _conv_specs(lay, rows, tm, c, nj, init, prev2, prev1):
    if lay.mode == "carry":
        nseq = rows // lay.seq_rows
        return ((init,), [pl.BlockSpec((HALO, c), lambda j, i: (0, j))],
                pl.BlockSpec((nseq, nj, CONV_W - 1, c), lambda j, i: (0, 0, 0, 0)),
                jax.ShapeDtypeStruct((nseq, nj, CONV_W - 1, c), F32),
                [pltpu.VMEM((HALO, c), F32)])
    return ((prev2, prev1), [pl.BlockSpec((lay.n_short, c), lambda j, i: (0, j))] * 2,
            pl.BlockSpec((tm, c), lambda j, i: (i, j)),
            jax.ShapeDtypeStruct((rows, nj * c), F32), [])


def _inproj_conv_kernel(lay, tm, cw, tiles_per_seq, *refs):
    n_ref, wb_ref, wc_ref, wh_ref, cwt_ref = refs[:5]
    if lay.mode == "carry":
        init_ref, ca_ref, st_ref, wbf_ref, work_ref, carry_ref = refs[5:]
    else:
        prev2_ref, prev1_ref, ca_ref, u_ref, wbf_ref, work_ref = refs[5:]
    j, i = pl.program_id(0), pl.program_id(1)

    @pl.when(i == 0)
    def _():
        for part, w_ref in enumerate((wb_ref, wc_ref, wh_ref)):
            _cast_into(wbf_ref, part * cw, 0, w_ref)

    if lay.mode == "carry":
        _load_halo(work_ref, init_ref, carry_ref, i, tiles_per_seq)
    else:
        work_ref[0:HALO, :] = jnp.zeros((HALO, cw), F32)
    rs = _row_tile(tm, SUB_ROWS)
    for r0 in range(0, tm, rs):
        p = _dot_nt(n_ref[r0:r0 + rs, :], wbf_ref[...])
        u = p[:, cw:2 * cw] * p[:, 2 * cw:]
        if lay.mode == "carry":
            uc = _conv3(u, cwt_ref, work_ref, lay, r0)
        else:
            uc = _conv3(u, cwt_ref, work_ref, lay, r0, prev2_ref, prev1_ref)
            u_ref[r0:r0 + rs, :] = u
        ca_ref[r0:r0 + rs, :] = (p[:, :cw] * uc).astype(BF16)
    if lay.mode == "carry":
        _store_tail(work_ref, carry_ref, st_ref, i, j, tiles_per_seq, tm)


def _inproj_conv(n, w_in_t, conv_w, lay, tm, cw, init=None, prev2=None, prev1=None):
    rows, d = n.shape
    dc = conv_w.shape[1]
    nj = dc // cw
    tiles_per_seq = (lay.seq_rows // tm) if lay.mode == "carry" else 1
    xargs, xspecs, xout_spec, xout_shape, xscratch = _conv_specs(lay, rows, tm, cw, nj, init, prev2, prev1)
    wspec = lambda part: pl.BlockSpec((None, cw, d), lambda j, i: (0, part * nj + j, 0))
    return pl.pallas_call(
        functools.partial(_inproj_conv_kernel, lay, tm, cw, tiles_per_seq),
        grid=(nj, rows // tm),
        in_specs=[pl.BlockSpec((tm, d), lambda j, i: (i, 0)), wspec(0), wspec(1), wspec(2),
                  pl.BlockSpec((CONV_W, cw), lambda j, i: (0, j))] + xspecs,
        out_specs=[pl.BlockSpec((tm, cw), lambda j, i: (i, j)), xout_spec],
        out_shape=[jax.ShapeDtypeStruct((rows, dc), BF16), xout_shape],
        scratch_shapes=[pltpu.VMEM((3 * cw, d), BF16), pltpu.VMEM((HALO + tm, cw), F32)] + xscratch,
        compiler_params=_cparams("arbitrary", "arbitrary"),
        name="inproj_conv_" + lay.mode)(n, w_in_t, w_in_t, w_in_t, conv_w, *xargs)


def _inproj_act_kernel(tm, n_lin, n_silu, q_scale, n_ref, w_ref, o_ref, wbf_ref):
    j, i = pl.program_id(0), pl.program_id(1)

    @pl.when(i == 0)
    def _():
        _cast_into(wbf_ref, 0, 0, w_ref.at[0] if len(w_ref.shape) == 3 else w_ref)

    rs = _row_tile(tm, SUB_ROWS)
    for r0 in range(0, tm, rs):
        p = _dot_nt(n_ref[r0:r0 + rs, :], wbf_ref[...])
        if n_lin:
            act = p * jnp.where(j == 0, q_scale, 1.0)
            if n_silu:
                act = jnp.where(j >= n_lin, p * _sigmoid(p), act)
        else:
            s = _sigmoid(p)
            act = jnp.where(j < n_silu, p * s, s) if n_silu else s
        o_ref[r0:r0 + rs, :] = act.astype(BF16)


def _inproj_act(n, w_t, w_spec, tiles, tm, tn, q_scale):
    rows, d = n.shape
    n_lin, n_silu, n_sig = tiles
    assert not (n_lin and n_sig)
    n_col = n_lin + n_silu + n_sig
    return pl.pallas_call(
        functools.partial(_inproj_act_kernel, tm, n_lin, n_silu, q_scale),
        grid=(n_col, rows // tm),
        in_specs=[pl.BlockSpec((tm, d), lambda j, i: (i, 0)), w_spec],
        out_specs=pl.BlockSpec((tm, tn), lambda j, i: (i, j)),
        out_shape=jax.ShapeDtypeStruct((rows, n_col * tn), BF16),
        scratch_shapes=[pltpu.VMEM((tn, d), BF16)],
        compiler_params=_cparams("arbitrary", "arbitrary"), name="inproj_act")(n, w_t)


def _gate_kernel(n_ref, wa_ref, wup_ref, b_ref, o_ref):
    a = _dot_nt(n_ref[...], wa_ref[...]).astype(BF16)
    z = _dot(a, wup_ref[...]) + b_ref[...]
    o_ref[...] = (jnp.minimum(z, 0.0) - jnp.log(1.0 + jnp.exp(-jnp.abs(z)))) * (1.0 / GATE_TAU)


def _gate(n, wa, wup, b, tm):
    rows, d = n.shape
    rp, dk = wup.shape
    return pl.pallas_call(
        _gate_kernel, grid=(rows // tm,),
        in_specs=[pl.BlockSpec((tm, d), lambda i: (i, 0)), pl.BlockSpec((rp, d), lambda i: (0, 0)),
                  pl.BlockSpec((rp, dk), lambda i: (0, 0)), pl.BlockSpec((1, dk), lambda i: (0, 0))],
        out_specs=pl.BlockSpec((tm, dk), lambda i: (i, 0)),
        out_shape=jax.ShapeDtypeStruct((rows, dk), F32),
        compiler_params=_cparams("arbitrary"), name="gate_lowrank")(n, wa, wup, b)


def _cumsum_groups(x, sub):
    rows = x.shape[0]
    pos = lax.broadcasted_iota(jnp.int32, (rows, 1), 0) % sub
    s = 1
    while s < sub:
        x = x + jnp.where(pos >= s, pltpu.roll(x, s, 0), 0.0)
        s *= 2
    return x


def _gla_chunk(q, k, v, lg, s, sub):
    c, dk = q.shape
    nsub = c // sub
    bt = _cumsum_groups(lg, sub)
    r = jnp.zeros((1, dk), F32)
    r_sub, b_rows = [], []
    for i in range(nsub):
        r_sub.append(r)
        b_rows.append(bt[i * sub:(i + 1) * sub, :] + r)
        r = r + bt[(i + 1) * sub - 1:(i + 1) * sub, :]
    b = jnp.concatenate(b_rows, axis=0) if nsub > 1 else b_rows[0]
    b_last = r
    row = lax.broadcasted_iota(jnp.int32, (c, 1), 0)
    qe = (q * jnp.exp(b)).astype(BF16)
    kd = (k * jnp.exp(b_last - b)).astype(BF16)
    qt = q * jnp.exp(bt)
    att_rows = []
    for i in range(nsub):
        arg = jnp.where(row < (i + 1) * sub, r_sub[i] - b, -jnp.inf)
        ke = (k * jnp.exp(arg)).astype(BF16)
        a = lax.dot_general(qt[i * sub:(i + 1) * sub, :].astype(BF16), ke,
                            (((1,), (1,)), ((), ())), preferred_element_type=F32)
        col = lax.broadcasted_iota(jnp.int32, (sub, c), 1)
        rloc = lax.broadcasted_iota(jnp.int32, (sub, c), 0)
        att_rows.append(jnp.where(col <= rloc + i * sub, a, 0.0))
    att = (jnp.concatenate(att_rows, axis=0) if nsub > 1 else att_rows[0]).astype(BF16)
    o = _dot(att, v) + _dot(qe, s.astype(BF16))
    upd = lax.dot_general(kd, v, (((0,), (0,)), ((), ())), preferred_element_type=F32)
    dl = jnp.exp(b_last)
    dl_col = jnp.transpose(jnp.broadcast_to(dl, (LANES, dk)))
    dv = s.shape[1]
    s_dec = jnp.concatenate([s[:, n * LANES:(n + 1) * LANES] * dl_col for n in range(dv // LANES)], axis=1)
    return o, s_dec + upd


def _gla_out(o, gn, gs):
    return (_rms(o, gn) * gs.astype(F32)).astype(BF16)


def _gla_seq_kernel(sub, q_ref, k_ref, v_ref, lg_ref, gs_ref, gn_ref, s0_ref, ob_ref, s_ref):
    @pl.when(pl.program_id(2) == 0)
    def _():
        s_ref[...] = s0_ref[...]

    o, s_new = _gla_chunk(q_ref[...].astype(F32), k_ref[...].astype(F32), v_ref[...], lg_ref[...],
                          s_ref[0, 0], sub)
    s_ref[0, 0] = s_new
    ob_ref[...] = _gla_out(o, gn_ref[...], gs_ref[...])


def _gla_seq(lin, logg, gn, s0, heads, dk, dv, nseq, seq_rows, chunk, sub, row0, shared_init):
    nchunk = seq_rows // chunk
    blk0 = row0 // chunk
    rb = lambda b, h, c: blk0 + b * nchunk + c
    kq = (heads * dk) // dk
    kv = (2 * heads * dk) // dv
    return pl.pallas_call(
        functools.partial(_gla_seq_kernel, sub),
        grid=(nseq, heads, nchunk),
        in_specs=[pl.BlockSpec((chunk, dk), lambda b, h, c: (rb(b, h, c), h)),
                  pl.BlockSpec((chunk, dk), lambda b, h, c: (rb(b, h, c), kq + h)),
                  pl.BlockSpec((chunk, dv), lambda b, h, c: (rb(b, h, c), kv + h)),
                  pl.BlockSpec((chunk, dk), lambda b, h, c: (rb(b, h, c), h)),
                  pl.BlockSpec((chunk, dv), lambda b, h, c: (rb(b, h, c), kv + heads + h)),
                  pl.BlockSpec((1, dv), lambda b, h, c: (0, 0)),
                  pl.BlockSpec((1, 1, dk, dv), lambda b, h, c: (0 if shared_init else b, h, 0, 0))],
        out_specs=[pl.BlockSpec((chunk, dv), lambda b, h, c: (b * nchunk + c, h)),
                   pl.BlockSpec((1, 1, dk, dv), lambda b, h, c: (b, h, 0, 0))],
        out_shape=[jax.ShapeDtypeStruct((nseq * seq_rows, heads * dv), BF16),
                   jax.ShapeDtypeStruct((nseq, heads, dk, dv), F32)],
        compiler_params=_cparams("arbitrary", "arbitrary", "arbitrary"),
        name="gla_seq")(lin, lin, lin, logg, lin, gn, s0)


def _gla_short_kernel(nb, sl, q_ref, k_ref, v_ref, lg_ref, gs_ref, gn_ref, s0_ref, ob_ref, s_ref):
    q = q_ref[...].astype(F32)
    k = k_ref[...].astype(F32)
    v = v_ref[...].astype(F32)
    lg = lg_ref[...]
    outs = []
    for n in range(nb):
        rs = slice(n * sl, (n + 1) * sl)
        o, s_new = _gla_chunk(q[rs], k[rs], v[rs].astype(BF16), lg[rs], s0_ref[n, 0], sl)
        s_ref[n, 0] = s_new
        outs.append(o)
    o = jnp.concatenate(outs, axis=0)
    ob_ref[...] = _gla_out(o, gn_ref[...], gs_ref[...])


def _gla_short(lin, logg, gn, s0, heads, dk, dv, nseq, sl, nb):
    rows = nb * sl
    kq = (heads * dk) // dk
    kv = (2 * heads * dk) // dv
    return pl.pallas_call(
        functools.partial(_gla_short_kernel, nb, sl),
        grid=(nseq // nb, heads),
        in_specs=[pl.BlockSpec((rows, dk), lambda b, h: (b, h)),
                  pl.BlockSpec((rows, dk), lambda b, h: (b, kq + h)),
                  pl.BlockSpec((rows, dv), lambda b, h: (b, kv + h)),
                  pl.BlockSpec((rows, dk), lambda b, h: (b, h)),
                  pl.BlockSpec((rows, dv), lambda b, h: (b, kv + heads + h)),
                  pl.BlockSpec((1, dv), lambda b, h: (0, 0)),
                  pl.BlockSpec((nb, 1, dk, dv), lambda b, h: (b, h, 0, 0))],
        out_specs=[pl.BlockSpec((rows, dv), lambda b, h: (b, h)),
                   pl.BlockSpec((nb, 1, dk, dv), lambda b, h: (b, h, 0, 0))],
        out_shape=[jax.ShapeDtypeStruct((nseq * sl, heads * dv), BF16),
                   jax.ShapeDtypeStruct((nseq, heads, dk, dv), F32)],
        compiler_params=_cparams("arbitrary", "arbitrary"),
        name="gla_short")(lin, lin, lin, logg, lin, gn, s0)


def _merge_kernel(ca_ref, ob_ref, ga_ref, gb_ref, wc_ref, wg_ref, o_ref):
    ya = _dot(ca_ref[...], wc_ref[...])
    yb = _dot(ob_ref[...], wg_ref[...])
    o_ref[...] = (ga_ref[...].astype(F32) * ya + gb_ref[...].astype(F32) * yb).astype(BF16)


def _merge(ca, ob, sig, wc, wg, tm, tn):
    rows, dc = ca.shape
    dg, d = wg.shape
    nn = d // tn
    return pl.pallas_call(
        _merge_kernel, grid=(rows // tm, nn),
        in_specs=[pl.BlockSpec((tm, dc), lambda i, j: (i, 0)), pl.BlockSpec((tm, dg), lambda i, j: (i, 0)),
                  pl.BlockSpec((tm, tn), lambda i, j: (i, j)), pl.BlockSpec((tm, tn), lambda i, j: (i, nn + j)),
                  pl.BlockSpec((dc, tn), lambda i, j: (0, j)), pl.BlockSpec((dg, tn), lambda i, j: (0, j))],
        out_specs=pl.BlockSpec((tm, tn), lambda i, j: (i, j)),
        out_shape=jax.ShapeDtypeStruct((rows, d), BF16),
        compiler_params=_cparams("arbitrary", "arbitrary"), name="merge")(ca, ob, sig, sig, wc, wg)


def _oproj_kernel(m_ref, x_ref, w_ref, g_ref, h_ref, n2_ref):
    h = x_ref[...] + _dot(m_ref[...], w_ref[...])
    h_ref[...] = h
    n2_ref[...] = _rms(h, g_ref[...]).astype(BF16)


def _oproj(m, x, w, g, tm):
    rows, d = x.shape
    row = lambda i: (i, 0)
    return pl.pallas_call(
        _oproj_kernel, grid=(rows // tm,),
        in_specs=[pl.BlockSpec((tm, d), row), pl.BlockSpec((tm, d), row),
                  pl.BlockSpec((d, d), lambda i: (0, 0)), pl.BlockSpec((1, d), lambda i: (0, 0))],
        out_specs=[pl.BlockSpec((tm, d), row), pl.BlockSpec((tm, d), row)],
        out_shape=[jax.ShapeDtypeStruct((rows, d), F32), jax.ShapeDtypeStruct((rows, d), BF16)],
        compiler_params=_cparams("arbitrary"), name="oproj")(m, x, w, g)


def _ffn_up_kernel(lay, tm, fw, tiles_per_seq, *refs):
    n_ref, wa_ref, wg_ref, cwt_ref, b_ref = refs[:5]
    if lay.mode == "carry":
        init_ref, act_ref, st_ref, wbf_ref, work_ref, carry_ref = refs[5:]
    else:
        prev2_ref, prev1_ref, act_ref, gt_ref, wbf_ref, work_ref = refs[5:]
    j, i = pl.program_id(0), pl.program_id(1)

    @pl.when(i == 0)
    def _():
        _cast_into(wbf_ref, 0, 0, wa_ref)
        _cast_into(wbf_ref, 0, fw, wg_ref)

    if lay.mode == "carry":
        _load_halo(work_ref, init_ref, carry_ref, i, tiles_per_seq)
    else:
        work_ref[0:HALO, :] = jnp.zeros((HALO, fw), F32)
    rs = _row_tile(tm, SUB_ROWS)
    for r0 in range(0, tm, rs):
        p = _dot(n_ref[r0:r0 + rs, :], wbf_ref[...])
        gt = p[:, fw:]
        if lay.mode == "carry":
            gc = _conv3(gt, cwt_ref, work_ref, lay, r0)
        else:
            gc = _conv3(gt, cwt_ref, work_ref, lay, r0, prev2_ref, prev1_ref)
            gt_ref[r0:r0 + rs, :] = gt
        z = gc + b_ref[...]
        act_ref[r0:r0 + rs, :] = (z * _sigmoid(z) * p[:, :fw]).astype(BF16)
    if lay.mode == "carry":
        _store_tail(work_ref, carry_ref, st_ref, i, j, tiles_per_seq, tm)


def _ffn_up(n2, w_up, conv_w, bias, lay, tm, fw, init=None, prev2=None, prev1=None):
    rows, d = n2.shape
    dff = conv_w.shape[1]
    nj = dff // fw
    tiles_per_seq = (lay.seq_rows // tm) if lay.mode == "carry" else 1
    xargs, xspecs, xout_spec, xout_shape, xscratch = _conv_specs(lay, rows, tm, fw, nj, init, prev2, prev1)
    wspec = lambda part: pl.BlockSpec((None, d, fw), lambda j, i: (0, 0, part * nj + j))
    return pl.pallas_call(
        functools.partial(_ffn_up_kernel, lay, tm, fw, tiles_per_seq),
        grid=(nj, rows // tm),
        in_specs=[pl.BlockSpec((tm, d), lambda j, i: (i, 0)), wspec(0), wspec(1),
                  pl.BlockSpec((CONV_W, fw), lambda j, i: (0, j)), pl.BlockSpec((1, fw), lambda j, i: (0, j))]
                 + xspecs,
        out_specs=[pl.BlockSpec((tm, fw), lambda j, i: (i, j)), xout_spec],
        out_shape=[jax.ShapeDtypeStruct((rows, dff), BF16), xout_shape],
        scratch_shapes=[pltpu.VMEM((d, 2 * fw), BF16), pltpu.VMEM((HALO + tm, fw), F32)] + xscratch,
        compiler_params=_cparams("arbitrary", "arbitrary"),
        name="ffn_up_" + lay.mode)(n2, w_up, w_up, conv_w, bias, *xargs)


def _ffn_down_kernel(act_ref, w_ref, h_ref, g_ref, y_ref):
    k = pl.program_id(1)
    part = _dot(act_ref[...], w_ref[...])

    @pl.when(k == 0)
    def _():
        y_ref[...] = h_ref[...] + part

    @pl.when(k > 0)
    def _():
        y_ref[...] += part

    @pl.when(k == pl.num_programs(1) - 1)
    def _():
        y_ref[...] = _rms(y_ref[...], g_ref[...])


def _ffn_down(act, w, h, g, tm, tk):
    rows, dff = act.shape
    d = w.shape[1]
    return pl.pallas_call(
        _ffn_down_kernel, grid=(rows // tm, dff // tk),
        in_specs=[pl.BlockSpec((tm, tk), lambda i, k: (i, k)), pl.BlockSpec((tk, d), lambda i, k: (k, 0)),
                  pl.BlockSpec((tm, d), lambda i, k: (i, 0)), pl.BlockSpec((1, d), lambda i, k: (0, 0))],
        out_specs=pl.BlockSpec((tm, d), lambda i, k: (i, 0)),
        out_shape=jax.ShapeDtypeStruct((rows, d), F32),
        compiler_params=_cparams("arbitrary", "arbitrary"), name="ffn_down")(act, w, h, g)


def _init_rows(state2):
    return jnp.pad(state2, ((HALO - (CONV_W - 1), 0), (0, 0)))


def kernel(x_prompt, x_sample, state_conv, state_gla, state_ffn_conv, meta_tokens, norm_mix_g, w_in, conv_mix_w, w_conv_out, w_gate_up, b_gate, gla_norm_g, w_gla_out, w_o, norm_ffn_g, w_ffn_up, ffn_conv_w, ffn_conv_b, w_ffn_down, final_norm_g):
    bp, seq, d = x_prompt.shape
    bs, sl, _ = x_sample.shape
    assert w_in.shape[0] == 1, "single-layer step"
    n_meta = meta_tokens.shape[0]
    dc = state_conv.shape[-1]
    _, _, heads, dk, dv = state_gla.shape
    dff = state_ffn_conv.shape[-1]
    rank = w_gate_up.shape[1]
    assert n_meta % GLA_SUB == 0 and seq % GLA_SUB == 0 and GLA_SUB % sl == 0 and sl >= CONV_W - 1

    o_q = 3 * dc
    o_a = o_q + 2 * heads * dk + 2 * heads * dv
    o_ga = o_a + rank
    tn = heads * dk
    assert o_q % tn == 0 and (heads * dv) % tn == 0 and d % tn == 0
    cw, fw = CONV_COLS, FFN_COLS
    tiles_qkvg = ((2 * heads * dk + heads * dv) // tn, heads * dv // tn, 0)
    tiles_gates = (0, 0, 2 * d // tn)
    spec_qkvg = pl.BlockSpec((None, tn, d), lambda j, i: (0, o_q // tn + j, 0))
    spec_gates = pl.BlockSpec((pl.Element(1), pl.Element(tn), pl.Element(d)), lambda j, i: (0, (o_ga // 8 + j * (tn // 8)) * 8, 0))
    assert o_ga % 8 == 0
    q_scale = float(dk) ** -0.5

    w_in_t = jnp.swapaxes(w_in, 1, 2)
    wa = jnp.pad(w_in_t[0, o_a:o_ga], ((0, LANES - rank), (0, 0))).astype(BF16)
    wup = jnp.pad(w_gate_up[0], ((0, LANES - rank), (0, 0))).astype(BF16)
    wc = w_conv_out[0].astype(BF16)
    wg = w_gla_out[0].astype(BF16)
    wo = w_o[0].astype(BF16)
    wd = w_ffn_down[0].astype(BF16)
    g1, g2, gf, gn = norm_mix_g[0][None], norm_ffn_g[0][None], final_norm_g[None], gla_norm_g[0][None]
    bg, fb, cmw, fcw = b_gate[0][None], ffn_conv_b[0][None], conv_mix_w[0], ffn_conv_w[0]

    def in_projections(n, tm, lay, **conv_kw):
        ca, conv_out = _inproj_conv(n, w_in_t, cmw, lay, tm, cw, **conv_kw)
        lin = _inproj_act(n, w_in_t, spec_qkvg, tiles_qkvg, tm, tn, q_scale)
        sig = _inproj_act(n, w_in_t, spec_gates, tiles_gates, tm, tn, 1.0)
        return ca, conv_out, lin, sig, _gate(n, wa, wup, bg, tm)

    n_short = bs * sl
    rows_s = n_short + n_meta
    xs = jnp.concatenate([x_sample.reshape(n_short, d), meta_tokens.astype(x_sample.dtype)], axis=0)
    lay_s = SeqLayout("table", n_short=n_short, short_len=sl)
    tm_s = rows_s
    rep = lambda st, r: jnp.repeat(st[:, r], sl, axis=0)
    n_s = _rmsnorm(xs, g1, tm_s)
    ca_s, u_s, lin_s, sig_s, logg_s = in_projections(
        n_s, tm_s, lay_s, prev2=rep(state_conv[0], 0), prev1=rep(state_conv[0], 1))
    s_zero = jnp.zeros((1, heads, dk, dv), F32)
    ob_m, s_meta = _gla_seq(lin_s, logg_s, gn, s_zero, heads, dk, dv, 1, n_meta, n_meta, GLA_SUB,
                            n_short, True)
    ob_smp, s_smp = _gla_short(lin_s, logg_s, gn, state_gla[0], heads, dk, dv, bs, sl,
                               SHORT_SEQS_PER_STEP)
    ob_s = jnp.concatenate([ob_smp, ob_m], axis=0)
    m_s = _merge(ca_s, ob_s, sig_s, wc, wg, tm_s, tn)
    h_s, n2_s = _oproj(m_s, xs, wo, g2, _row_tile(rows_s, ROW_TILE_RESIDENT))
    act_s, gt_s = _ffn_up(n2_s, w_ffn_up, fcw, fb, lay_s, tm_s, fw,
                          prev2=rep(state_ffn_conv[0], 0), prev1=rep(state_ffn_conv[0], 1))
    y_s = _ffn_down(act_s, wd, h_s, gf, tm_s, fw)

    rows_p = bp * seq
    xp = x_prompt.reshape(rows_p, d)
    lay_p = SeqLayout("carry", seq_rows=seq)
    tm_p = _row_tile(seq, ROW_TILE)
    tm_o = _row_tile(seq, ROW_TILE_RESIDENT)
    n_p = _rmsnorm(xp, g1, tm_o)
    ca_p, conv_p, lin_p, sig_p, logg_p = in_projections(
        n_p, tm_p, lay_p, init=_init_rows(u_s[rows_s - (CONV_W - 1):]))
    ob_p, s_p = _gla_seq(lin_p, logg_p, gn, s_meta, heads, dk, dv, bp, seq, _row_tile(seq, GLA_CHUNK),
                         GLA_SUB, 0, True)
    m_p = _merge(ca_p, ob_p, sig_p, wc, wg, tm_p, tn)
    h_p, n2_p = _oproj(m_p, xp, wo, g2, tm_o)
    act_p, ffn_p = _ffn_up(n2_p, w_ffn_up, fcw, fb, lay_p, tm_p, fw,
                           init=_init_rows(gt_s[rows_s - (CONV_W - 1):]))
    y_p = _ffn_down(act_p, wd, h_p, gf, tm_o, dff // 4)

    last_rows = lambda a, c: a[:n_short].reshape(bs, sl, c)[:, sl - (CONV_W - 1):]
    untile = lambda st: jnp.swapaxes(st, 1, 2).reshape(st.shape[0], CONV_W - 1, -1)
    return (y_p.reshape(bp, seq, d), y_s[:n_short].reshape(bs, sl, d),
            untile(conv_p)[None], s_p[None], untile(ffn_p)[None],
            last_rows(u_s, dc)[None], s_smp[None], last_rows(gt_s, dff)[None])
```

```python
import functools

import jax
import jax.numpy as jnp
from jax import lax
from jax.experimental import pallas as pl
from jax.experimental.pallas import tpu as pltpu

EPS = 1e-6
GATE_TAU = 16.0
CONV_W = 3
GLA_SUB = 16
HALO = 8
V7X_VMEM_LIMIT = 56 * 1024 * 1024
LANES = 128
ROW_TILE = 1024
ROW_TILE_RESIDENT = 512
GLA_CHUNK = 128
CONV_COLS = 256
FFN_COLS = 512
SHORT_SEQS_PER_STEP = 8
SUB_ROWS = 256
CAST_ROWS = 64
F32 = jnp.float32
BF16 = jnp.bfloat16


def _cparams(*sem):
    return pltpu.CompilerParams(dimension_semantics=sem, vmem_limit_bytes=V7X_VMEM_LIMIT)


def _row_tile(rows, target):
    best = None
    for t in range(16, min(rows, target) + 1, 16):
        if rows % t == 0:
            best = t
    return best or rows


def _rms(x, g):
    return x * lax.rsqrt(jnp.mean(x * x, axis=-1, keepdims=True) + EPS) * g


def _sigmoid(x):
    return 0.5 * jnp.tanh(0.5 * x) + 0.5


def _silu(x):
    h = 0.5 * x
    return h * jnp.tanh(h) + h


def _dot(a, b):
    return jnp.dot(a, b, preferred_element_type=F32)


def _dot_nt(a, b):
    return lax.dot_general(a, b, (((1,), (1,)), ((), ())), preferred_element_type=F32)


def _cast_into(dst_ref, row0, col0, src_ref):
    rows, cols = src_ref.shape

    def body(r, carry):
        off = pl.multiple_of(r * CAST_ROWS, CAST_ROWS)
        dst_ref[pl.ds(row0 + off, CAST_ROWS), col0:col0 + cols] = src_ref[pl.ds(off, CAST_ROWS), :].astype(BF16)
        return carry

    lax.fori_loop(0, rows // CAST_ROWS, body, 0)


def _rmsnorm_kernel(x_ref, g_ref, n_ref):
    n_ref[...] = _rms(x_ref[...], g_ref[...]).astype(BF16)


def _rmsnorm(x, g, tm):
    rows, d = x.shape
    return pl.pallas_call(
        _rmsnorm_kernel, grid=(rows // tm,),
        in_specs=[pl.BlockSpec((tm, d), lambda i: (i, 0)), pl.BlockSpec((1, d), lambda i: (0, 0))],
        out_specs=pl.BlockSpec((tm, d), lambda i: (i, 0)),
        out_shape=jax.ShapeDtypeStruct((rows, d), BF16),
        compiler_params=_cparams("arbitrary"), name="rmsnorm")(x, g)


class SeqLayout:
    def __init__(self, mode, seq_rows=None, n_short=None, short_len=None):
        self.mode, self.seq_rows, self.n_short, self.short_len = mode, seq_rows, n_short, short_len


def _load_halo(work_ref, init_ref, carry_ref, i, tiles_per_seq):
    first = (i % tiles_per_seq) == 0

    @pl.when(first)
    def _():
        work_ref[0:HALO, :] = init_ref[...]

    @pl.when(jnp.logical_not(first))
    def _():
        work_ref[0:HALO, :] = carry_ref[...]


def _store_tail(work_ref, carry_ref, st_ref, i, j, tiles_per_seq, tm):
    carry_ref[...] = work_ref[tm:tm + HALO, :]

    @pl.when((i % tiles_per_seq) == tiles_per_seq - 1)
    def _():
        st_ref[i // tiles_per_seq, j] = work_ref[HALO + tm - (CONV_W - 1):HALO + tm, :]


def _conv3(u, w_ref, work_ref, lay, r0, prev2_ref=None, prev1_ref=None):
    rs, c = u.shape
    work_ref[HALO + r0:HALO + r0 + rs, :] = u
    um1 = work_ref[HALO - 1 + r0:HALO - 1 + r0 + rs, :]
    um2 = work_ref[HALO - 2 + r0:HALO - 2 + r0 + rs, :]
    if lay.mode == "table":
        take = max(0, min(lay.n_short, r0 + rs) - r0)

        def table_rows(ref):
            parts = ([ref[r0:r0 + take, :]] if take else []) + ([jnp.zeros((rs - take, c), F32)] if take < rs else [])
            return parts[0] if len(parts) == 1 else jnp.concatenate(parts, axis=0)

        p2, p1 = table_rows(prev2_ref), table_rows(prev1_ref)
        t = r0 + lax.broadcasted_iota(jnp.int32, (rs, 1), 0)
        pos = jnp.where(t < lay.n_short, t % lay.short_len, t - lay.n_short)
        um1 = jnp.where(pos == 0, p1, um1)
        um2 = jnp.where(pos == 0, p2, jnp.where(pos == 1, p1, um2))
    w = w_ref[...]
    return w[0:1, :] * um2 + w[1:2, :] * um1 + w[2:3, :] * u


def _conv_specs(lay, rows, tm, c, nj, init, prev2, prev1):
    if lay.mode == "carry":
        nseq = rows // lay.seq_rows
        return ((init,), [pl.BlockSpec((HALO, c), lambda j, i: (0, j))],
                pl.BlockSpec((nseq, nj, CONV_W - 1, c), lambda j, i: (0, 0, 0, 0)),
                jax.ShapeDtypeStruct((nseq, nj, CONV_W - 1, c), F32),
                [pltpu.VMEM((HALO, c), F32)])
    return ((prev2, prev1), [pl.BlockSpec((lay.n_short, c), lambda j, i: (0, j))] * 2,
            pl.BlockSpec((tm, c), lambda j, i: (i, j)),
            jax.ShapeDtypeStruct((rows, nj * c), F32), [])


def _inproj_conv_kernel(lay, tm, cw, tiles_per_seq, *refs):
    n_ref, wb_ref, wc_ref, wh_ref, cwt_ref = refs[:5]
    if lay.mode == "carry":
        init_ref, ca_ref, st_ref, wbf_ref, work_ref, carry_ref = refs[5:]
    else:
        prev2_ref, prev1_ref, ca_ref, u_ref, wbf_ref, work_ref = refs[5:]
    j, i = pl.program_id(0), pl.program_id(1)

    @pl.when(i == 0)
    def _():
        for part, w_ref in enumerate((wb_ref, wc_ref, wh_ref)):
            _cast_into(wbf_ref, part * cw, 0, w_ref)

    if lay.mode == "carry":
        _load_halo(work_ref, init_ref, carry_ref, i, tiles_per_seq)
    else:
        work_ref[0:HALO, :] = jnp.zeros((HALO, cw), F32)
    rs = _row_tile(tm, SUB_ROWS)
    for r0 in range(0, tm, rs):
        p = _dot_nt(n_ref[r0:r0 + rs, :], wbf_ref[...])
        u = p[:, cw:2 * cw] * p[:, 2 * cw:]
        if lay.mode == "carry":
            uc = _conv3(u, cwt_ref, work_ref, lay, r0)
        else:
            uc = _conv3(u, cwt_ref, work_ref, lay, r0, prev2_ref, prev1_ref)
            u_ref[r0:r0 + rs, :] = u
        ca_ref[r0:r0 + rs, :] = (p[:, :cw] * uc).astype(BF16)
    if lay.mode == "carry":
        _store_tail(work_ref, carry_ref, st_ref, i, j, tiles_per_seq, tm)


def _inproj_conv(n, w_in_t, conv_w, lay, tm, cw, init=None, prev2=None, prev1=None):
    rows, d = n.shape
    dc = conv_w.shape[1]
    nj = dc // cw
    tiles_per_seq = (lay.seq_rows // tm) if lay.mode == "carry" else 1
    xargs, xspecs, xout_spec, xout_shape, xscratch = _conv_specs(lay, rows, tm, cw, nj, init, prev2, prev1)
    wspec = lambda part: pl.BlockSpec((None, cw, d), lambda j, i: (0, part * nj + j, 0))
    return pl.pallas_call(
        functools.partial(_inproj_conv_kernel, lay, tm, cw, tiles_per_seq),
        grid=(nj, rows // tm),
        in_specs=[pl.BlockSpec((tm, d), lambda j, i: (i, 0)), wspec(0), wspec(1), wspec(2),
                  pl.BlockSpec((CONV_W, cw), lambda j, i: (0, j))] + xspecs,
        out_specs=[pl.BlockSpec((tm, cw), lambda j, i: (i, j)), xout_spec],
        out_shape=[jax.ShapeDtypeStruct((rows, dc), BF16), xout_shape],
        scratch_shapes=[pltpu.VMEM((3 * cw, d), BF16), pltpu.VMEM((HALO + tm, cw), F32)] + xscratch,
        compiler_params=_cparams("arbitrary", "arbitrary"),
        name="inproj_conv_" + lay.mode)(n, w_in_t, w_in_t, w_in_t, conv_w, *xargs)


def _inproj_act_kernel(tm, n_lin, n_silu, q_scale, n_ref, w_ref, o_ref, wbf_ref):
    j, i = pl.program_id(0), pl.program_id(1)

    @pl.when(i == 0)
    def _():
        _cast_into(wbf_ref, 0, 0, w_ref.at[0] if len(w_ref.shape) == 3 else w_ref)

    rs = _row_tile(tm, SUB_ROWS)
    for r0 in range(0, tm, rs):
        p = _dot_nt(n_ref[r0:r0 + rs, :], wbf_ref[...])
        if n_lin:
            act = p * jnp.where(j == 0, q_scale, 1.0)
            if n_silu:
                act = jnp.where(j >= n_lin, _silu(p), act)
        else:
            act = jnp.where(j < n_silu, _silu(p), _sigmoid(p)) if n_silu else _sigmoid(p)
        o_ref[r0:r0 + rs, :] = act.astype(BF16)


def _inproj_act(n, w_t, w_spec, tiles, tm, tn, q_scale):
    rows, d = n.shape
    n_lin, n_silu, n_sig = tiles
    assert not (n_lin and n_sig)
    n_col = n_lin + n_silu + n_sig
    return pl.pallas_call(
        functools.partial(_inproj_act_kernel, tm, n_lin, n_silu, q_scale),
        grid=(n_col, rows // tm),
        in_specs=[pl.BlockSpec((tm, d), lambda j, i: (i, 0)), w_spec],
        out_specs=pl.BlockSpec((tm, tn), lambda j, i: (i, j)),
        out_shape=jax.ShapeDtypeStruct((rows, n_col * tn), BF16),
        scratch_shapes=[pltpu.VMEM((tn, d), BF16)],
        compiler_params=_cparams("arbitrary", "arbitrary"), name="inproj_act")(n, w_t)


def _gate_kernel(n_ref, wa_ref, wup_ref, b_ref, o_ref):
    a = _dot_nt(n_ref[...], wa_ref[...]).astype(BF16)
    z = _dot(a, wup_ref[...]) + b_ref[...]
    o_ref[...] = (jnp.minimum(z, 0.0) - jnp.log(1.0 + jnp.exp(-jnp.abs(z)))) * (1.0 / GATE_TAU)


def _gate(n, wa, wup, b, tm):
    rows, d = n.shape
    rp, dk = wup.shape
    return pl.pallas_call(
        _gate_kernel, grid=(rows // tm,),
        in_specs=[pl.BlockSpec((tm, d), lambda i: (i, 0)), pl.BlockSpec((rp, d), lambda i: (0, 0)),
                  pl.BlockSpec((rp, dk), lambda i: (0, 0)), pl.BlockSpec((1, dk), lambda i: (0, 0))],
        out_specs=pl.BlockSpec((tm, dk), lambda i: (i, 0)),
        out_shape=jax.ShapeDtypeStruct((rows, dk), F32),
        compiler_params=_cparams("arbitrary"), name="gate_lowrank")(n, wa, wup, b)


def _cumsum_groups(x, sub):
    rows = x.shape[0]
    pos = lax.broadcasted_iota(jnp.int32, (rows, 1), 0) % sub
    s = 1
    while s < sub:
        x = x + jnp.where(pos >= s, pltpu.roll(x, s, 0), 0.0)
        s *= 2
    return x


def _gla_chunk(q, k, v, lg, s, sub):
    c, dk = q.shape
    nsub = c // sub
    bt = _cumsum_groups(lg, sub)
    r = jnp.zeros((1, dk), F32)
    r_sub, b_rows = [], []
    for i in range(nsub):
        r_sub.append(r)
        b_rows.append(bt[i * sub:(i + 1) * sub, :] + r)
        r = r + bt[(i + 1) * sub - 1:(i + 1) * sub, :]
    b = jnp.concatenate(b_rows, axis=0) if nsub > 1 else b_rows[0]
    b_last = r
    qe = (q * jnp.exp(b)).astype(BF16)
    kd = (k * jnp.exp(b_last - b)).astype(BF16)
    qt = q * jnp.exp(bt)
    att_rows = []
    for i in range(nsub):
        seen = (i + 1) * sub
        ke = (k[:seen] * jnp.exp(r_sub[i] - b[:seen])).astype(BF16)
        if seen < c:
            ke = jnp.concatenate([ke, jnp.zeros((c - seen, dk), BF16)], axis=0)
        a = lax.dot_general(qt[i * sub:(i + 1) * sub, :].astype(BF16), ke,
                            (((1,), (1,)), ((), ())), preferred_element_type=F32)
        col = lax.broadcasted_iota(jnp.int32, (sub, c), 1)
        rloc = lax.broadcasted_iota(jnp.int32, (sub, c), 0)
        att_rows.append(jnp.where(col <= rloc + i * sub, a, 0.0))
    att = (jnp.concatenate(att_rows, axis=0) if nsub > 1 else att_rows[0]).astype(BF16)
    o = _dot(att, v) + _dot(qe, s.astype(BF16))
    upd = lax.dot_general(kd, v, (((0,), (0,)), ((), ())), preferred_element_type=F32)
    dl = jnp.exp(b_last)
    dl_col = jnp.transpose(jnp.broadcast_to(dl, (LANES, dk)))
    dv = s.shape[1]
    s_dec = jnp.concatenate([s[:, n * LANES:(n + 1) * LANES] * dl_col for n in range(dv // LANES)], axis=1)
    return o, s_dec + upd


def _gla_out(o, gn, gs):
    return (_rms(o, gn) * gs.astype(F32)).astype(BF16)


def _gla_seq_kernel(sub, heads, dk, dv, q_ref, k_ref, v_ref, lg_ref, gs_ref, gn_ref, s0_ref, ob_ref, s_ref):
    @pl.when(pl.program_id(1) == 0)
    def _():
        s_ref[...] = s0_ref[...]

    for h in range(heads):
        ck, cv = slice(h * dk, (h + 1) * dk), slice(h * dv, (h + 1) * dv)
        o, s_new = _gla_chunk(q_ref[:, ck].astype(F32), k_ref[:, ck].astype(F32), v_ref[:, cv], lg_ref[:, ck],
                              s_ref[0, h], sub)
        s_ref[0, h] = s_new
        ob_ref[:, cv] = _gla_out(o, gn_ref[...], gs_ref[:, cv])


def _gla_seq(lin, logg, gn, s0, heads, dk, dv, nseq, seq_rows, chunk, sub, row0, shared_init):
    nchunk = seq_rows // chunk
    blk0 = row0 // chunk
    rb = lambda b, c: blk0 + b * nchunk + c
    wk, wv = heads * dk, heads * dv
    assert (2 * wk) % wv == 0
    return pl.pallas_call(
        functools.partial(_gla_seq_kernel, sub, heads, dk, dv),
        grid=(nseq, nchunk),
        in_specs=[pl.BlockSpec((chunk, wk), lambda b, c: (rb(b, c), 0)),
                  pl.BlockSpec((chunk, wk), lambda b, c: (rb(b, c), 1)),
                  pl.BlockSpec((chunk, wv), lambda b, c: (rb(b, c), 2 * wk // wv)),
                  pl.BlockSpec((chunk, wk), lambda b, c: (rb(b, c), 0)),
                  pl.BlockSpec((chunk, wv), lambda b, c: (rb(b, c), 2 * wk // wv + 1)),
                  pl.BlockSpec((1, dv), lambda b, c: (0, 0)),
                  pl.BlockSpec((1, heads, dk, dv), lambda b, c: (0 if shared_init else b, 0, 0, 0))],
        out_specs=[pl.BlockSpec((chunk, wv), lambda b, c: (b * nchunk + c, 0)),
                   pl.BlockSpec((1, heads, dk, dv), lambda b, c: (b, 0, 0, 0))],
        out_shape=[jax.ShapeDtypeStruct((nseq * seq_rows, wv), BF16),
                   jax.ShapeDtypeStruct((nseq, heads, dk, dv), F32)],
        compiler_params=_cparams("arbitrary", "arbitrary"),
        name="gla_seq")(lin, lin, lin, logg, lin, gn, s0)


def _gla_short_kernel(nb, sl, q_ref, k_ref, v_ref, lg_ref, gs_ref, gn_ref, s0_ref, ob_ref, s_ref):
    q = q_ref[...].astype(F32)
    k = k_ref[...].astype(F32)
    v = v_ref[...].astype(F32)
    lg = lg_ref[...]
    outs = []
    for n in range(nb):
        rs = slice(n * sl, (n + 1) * sl)
        o, s_new = _gla_chunk(q[rs], k[rs], v[rs].astype(BF16), lg[rs], s0_ref[n, 0], sl)
        s_ref[n, 0] = s_new
        outs.append(o)
    o = jnp.concatenate(outs, axis=0)
    ob_ref[...] = _gla_out(o, gn_ref[...], gs_ref[...])


def _gla_short(lin, logg, gn, s0, heads, dk, dv, nseq, sl, nb):
    rows = nb * sl
    kq = (heads * dk) // dk
    kv = (2 * heads * dk) // dv
    return pl.pallas_call(
        functools.partial(_gla_short_kernel, nb, sl),
        grid=(nseq // nb, heads),
        in_specs=[pl.BlockSpec((rows, dk), lambda b, h: (b, h)),
                  pl.BlockSpec((rows, dk), lambda b, h: (b, kq + h)),
                  pl.BlockSpec((rows, dv), lambda b, h: (b, kv + h)),
                  pl.BlockSpec((rows, dk), lambda b, h: (b, h)),
                  pl.BlockSpec((rows, dv), lambda b, h: (b, kv + heads + h)),
                  pl.BlockSpec((1, dv), lambda b, h: (0, 0)),
                  pl.BlockSpec((nb, 1, dk, dv), lambda b, h: (b, h, 0, 0))],
        out_specs=[pl.BlockSpec((rows, dv), lambda b, h: (b, h)),
                   pl.BlockSpec((nb, 1, dk, dv), lambda b, h: (b, h, 0, 0))],
        out_shape=[jax.ShapeDtypeStruct((nseq * sl, heads * dv), BF16),
                   jax.ShapeDtypeStruct((nseq, heads, dk, dv), F32)],
        compiler_params=_cparams("arbitrary", "arbitrary"),
        name="gla_short")(lin, lin, lin, logg, lin, gn, s0)


def _merge_kernel(ca_ref, ob_ref, ga_ref, gb_ref, wc_ref, wg_ref, o_ref):
    ya = _dot(ca_ref[...], wc_ref[...])
    yb = _dot(ob_ref[...], wg_ref[...])
    o_ref[...] = (ga_ref[...].astype(F32) * ya + gb_ref[...].astype(F32) * yb).astype(BF16)


def _merge(ca, ob, sig, wc, wg, tm, tn):
    rows, dc = ca.shape
    dg, d = wg.shape
    nn = d // tn
    return pl.pallas_call(
        _merge_kernel, grid=(rows // tm, nn),
        in_specs=[pl.BlockSpec((tm, dc), lambda i, j: (i, 0)), pl.BlockSpec((tm, dg), lambda i, j: (i, 0)),
                  pl.BlockSpec((tm, tn), lambda i, j: (i, j)), pl.BlockSpec((tm, tn), lambda i, j: (i, nn + j)),
                  pl.BlockSpec((dc, tn), lambda i, j: (0, j)), pl.BlockSpec((dg, tn), lambda i, j: (0, j))],
        out_specs=pl.BlockSpec((tm, tn), lambda i, j: (i, j)),
        out_shape=jax.ShapeDtypeStruct((rows, d), BF16),
        compiler_params=_cparams("arbitrary", "arbitrary"), name="merge")(ca, ob, sig, sig, wc, wg)


def _oproj_kernel(m_ref, x_ref, w_ref, g_ref, h_ref, n2_ref):
    h = x_ref[...] + _dot(m_ref[...], w_ref[...])
    h_ref[...] = h
    n2_ref[...] = _rms(h, g_ref[...]).astype(BF16)


def _oproj(m, x, w, g, tm):
    rows, d = x.shape
    row = lambda i: (i, 0)
    return pl.pallas_call(
        _oproj_kernel, grid=(rows // tm,),
        in_specs=[pl.BlockSpec((tm, d), row), pl.BlockSpec((tm, d), row),
                  pl.BlockSpec((d, d), lambda i: (0, 0)), pl.BlockSpec((1, d), lambda i: (0, 0))],
        out_specs=[pl.BlockSpec((tm, d), row), pl.BlockSpec((tm, d), row)],
        out_shape=[jax.ShapeDtypeStruct((rows, d), F32), jax.ShapeDtypeStruct((rows, d), BF16)],
        compiler_params=_cparams("arbitrary"), name="oproj")(m, x, w, g)


def _ffn_up_kernel(lay, tm, fw, tiles_per_seq, *refs):
    n_ref, wa_ref, wg_ref, cwt_ref, b_ref = refs[:5]
    if lay.mode == "carry":
        init_ref, act_ref, st_ref, wbf_ref, work_ref, carry_ref = refs[5:]
    else:
        prev2_ref, prev1_ref, act_ref, gt_ref, wbf_ref, work_ref = refs[5:]
    j, i = pl.program_id(0), pl.program_id(1)

    @pl.when(i == 0)
    def _():
        _cast_into(wbf_ref, 0, 0, wa_ref)
        _cast_into(wbf_ref, 0, fw, wg_ref)

    if lay.mode == "carry":
        _load_halo(work_ref, init_ref, carry_ref, i, tiles_per_seq)
    else:
        work_ref[0:HALO, :] = jnp.zeros((HALO, fw), F32)
    rs = _row_tile(tm, SUB_ROWS)
    for r0 in range(0, tm, rs):
        p = _dot(n_ref[r0:r0 + rs, :], wbf_ref[...])
        gt = p[:, fw:]
        if lay.mode == "carry":
            gc = _conv3(gt, cwt_ref, work_ref, lay, r0)
        else:
            gc = _conv3(gt, cwt_ref, work_ref, lay, r0, prev2_ref, prev1_ref)
            gt_ref[r0:r0 + rs, :] = gt
        z = gc + b_ref[...]
        act_ref[r0:r0 + rs, :] = (_silu(z) * p[:, :fw]).astype(BF16)
    if lay.mode == "carry":
        _store_tail(work_ref, carry_ref, st_ref, i, j, tiles_per_seq, tm)


def _ffn_up(n2, w_up, conv_w, bias, lay, tm, fw, init=None, prev2=None, prev1=None):
    rows, d = n2.shape
    dff = conv_w.shape[1]
    nj = dff // fw
    tiles_per_seq = (lay.seq_rows // tm) if lay.mode == "carry" else 1
    xargs, xspecs, xout_spec, xout_shape, xscratch = _conv_specs(lay, rows, tm, fw, nj, init, prev2, prev1)
    wspec = lambda part: pl.BlockSpec((None, d, fw), lambda j, i: (0, 0, part * nj + j))
    return pl.pallas_call(
        functools.partial(_ffn_up_kernel, lay, tm, fw, tiles_per_seq),
        grid=(nj, rows // tm),
        in_specs=[pl.BlockSpec((tm, d), lambda j, i: (i, 0)), wspec(0), wspec(1),
                  pl.BlockSpec((CONV_W, fw), lambda j, i: (0, j)), pl.BlockSpec((1, fw), lambda j, i: (0, j))]
                 + xspecs,
        out_specs=[pl.BlockSpec((tm, fw), lambda j, i: (i, j)), xout_spec],
        out_shape=[jax.ShapeDtypeStruct((rows, dff), BF16), xout_shape],
        scratch_shapes=[pltpu.VMEM((d, 2 * fw), BF16), pltpu.VMEM((HALO + tm, fw), F32)] + xscratch,
        compiler_params=_cparams("arbitrary", "arbitrary"),
        name="ffn_up_" + lay.mode)(n2, w_up, w_up, conv_w, bias, *xargs)


def _ffn_down_kernel(act_ref, w_ref, h_ref, g_ref, y_ref):
    k = pl.program_id(1)
    part = _dot(act_ref[...], w_ref[...])

    @pl.when(k == 0)
    def _():
        y_ref[...] = h_ref[...] + part

    @pl.when(k > 0)
    def _():
        y_ref[...] += part

    @pl.when(k == pl.num_programs(1) - 1)
    def _():
        y_ref[...] = _rms(y_ref[...], g_ref[...])


def _ffn_down(act, w, h, g, tm, tk):
    rows, dff = act.shape
    d = w.shape[1]
    return pl.pallas_call(
        _ffn_down_kernel, grid=(rows // tm, dff // tk),
        in_specs=[pl.BlockSpec((tm, tk), lambda i, k: (i, k)), pl.BlockSpec((tk, d), lambda i, k: (k, 0)),
                  pl.BlockSpec((tm, d), lambda i, k: (i, 0)), pl.BlockSpec((1, d), lambda i, k: (0, 0))],
        out_specs=pl.BlockSpec((tm, d), lambda i, k: (i, 0)),
        out_shape=jax.ShapeDtypeStruct((rows, d), F32),
        compiler_params=_cparams("arbitrary", "arbitrary"), name="ffn_down")(act, w, h, g)


def _init_rows(state2):
    return jnp.pad(state2, ((HALO - (CONV_W - 1), 0), (0, 0)))


def kernel(x_prompt, x_sample, state_conv, state_gla, state_ffn_conv, meta_tokens, norm_mix_g, w_in, conv_mix_w, w_conv_out, w_gate_up, b_gate, gla_norm_g, w_gla_out, w_o, norm_ffn_g, w_ffn_up, ffn_conv_w, ffn_conv_b, w_ffn_down, final_norm_g):
    bp, seq, d = x_prompt.shape
    bs, sl, _ = x_sample.shape
    assert w_in.shape[0] == 1, "single-layer step"
    n_meta = meta_tokens.shape[0]
    dc = state_conv.shape[-1]
    _, _, heads, dk, dv = state_gla.shape
    dff = state_ffn_conv.shape[-1]
    rank = w_gate_up.shape[1]
    assert n_meta % GLA_SUB == 0 and seq % GLA_SUB == 0 and GLA_SUB % sl == 0 and sl >= CONV_W - 1

    o_q = 3 * dc
    o_a = o_q + 2 * heads * dk + 2 * heads * dv
    o_ga = o_a + rank
    tn = heads * dk
    assert o_q % tn == 0 and (heads * dv) % tn == 0 and d % tn == 0
    cw, fw = CONV_COLS, FFN_COLS
    tiles_qkvg = ((2 * heads * dk + heads * dv) // tn, heads * dv // tn, 0)
    tiles_gates = (0, 0, 2 * d // tn)
    spec_qkvg = pl.BlockSpec((None, tn, d), lambda j, i: (0, o_q // tn + j, 0))
    spec_gates = pl.BlockSpec((pl.Element(1), pl.Element(tn), pl.Element(d)), lambda j, i: (0, (o_ga // 8 + j * (tn // 8)) * 8, 0))
    assert o_ga % 8 == 0
    q_scale = float(dk) ** -0.5

    w_in_t = jnp.swapaxes(w_in, 1, 2)
    wa = jnp.pad(w_in_t[0, o_a:o_ga], ((0, LANES - rank), (0, 0))).astype(BF16)
    wup = jnp.pad(w_gate_up[0], ((0, LANES - rank), (0, 0))).astype(BF16)
    wc = w_conv_out[0].astype(BF16)
    wg = w_gla_out[0].astype(BF16)
    wo = w_o[0].astype(BF16)
    wd = w_ffn_down[0].astype(BF16)
    g1, g2, gf, gn = norm_mix_g[0][None], norm_ffn_g[0][None], final_norm_g[None], gla_norm_g[0][None]
    bg, fb, cmw, fcw = b_gate[0][None], ffn_conv_b[0][None], conv_mix_w[0], ffn_conv_w[0]

    def in_projections(n, tm, lay, **conv_kw):
        ca, conv_out = _inproj_conv(n, w_in_t, cmw, lay, tm, cw, **conv_kw)
        lin = _inproj_act(n, w_in_t, spec_qkvg, tiles_qkvg, tm, tn, q_scale)
        sig = _inproj_act(n, w_in_t, spec_gates, tiles_gates, tm, tn, 1.0)
        return ca, conv_out, lin, sig, _gate(n, wa, wup, bg, tm)

    n_short = bs * sl
    rows_s = n_short + n_meta
    xs = jnp.concatenate([x_sample.reshape(n_short, d), meta_tokens.astype(x_sample.dtype)], axis=0)
    lay_s = SeqLayout("table", n_short=n_short, short_len=sl)
    tm_s = rows_s
    rep = lambda st, r: jnp.repeat(st[:, r], sl, axis=0)
    n_s = _rmsnorm(xs, g1, tm_s)
    ca_s, u_s, lin_s, sig_s, logg_s = in_projections(
        n_s, tm_s, lay_s, prev2=rep(state_conv[0], 0), prev1=rep(state_conv[0], 1))
    s_zero = jnp.zeros((1, heads, dk, dv), F32)
    ob_m, s_meta = _gla_seq(lin_s, logg_s, gn, s_zero, heads, dk, dv, 1, n_meta, n_meta, GLA_SUB,
                            n_short, True)
    ob_smp, s_smp = _gla_short(lin_s, logg_s, gn, state_gla[0], heads, dk, dv, bs, sl,
                               SHORT_SEQS_PER_STEP)
    ob_s = jnp.concatenate([ob_smp, ob_m], axis=0)
    m_s = _merge(ca_s, ob_s, sig_s, wc, wg, tm_s, tn)
    h_s, n2_s = _oproj(m_s, xs, wo, g2, _row_tile(rows_s, ROW_TILE_RESIDENT))
    act_s, gt_s = _ffn_up(n2_s, w_ffn_up, fcw, fb, lay_s, tm_s, fw,
                          prev2=rep(state_ffn_conv[0], 0), prev1=rep(state_ffn_conv[0], 1))
    y_s = _ffn_down(act_s, wd, h_s, gf, tm_s, fw)

    rows_p = bp * seq
    xp = x_prompt.reshape(rows_p, d)
    lay_p = SeqLayout("carry", seq_rows=seq)
    tm_p = _row_tile(seq, ROW_TILE)
    tm_o = _row_tile(seq, ROW_TILE_RESIDENT)
    n_p = _rmsnorm(xp, g1, tm_o)
    ca_p, conv_p, lin_p, sig_p, logg_p = in_projections(
        n_p, tm_p, lay_p, init=_init_rows(u_s[rows_s - (CONV_W - 1):]))
    ob_p, s_p = _gla_seq(lin_p, logg_p, gn, s_meta, heads, dk, dv, bp, seq, _row_tile(seq, GLA_CHUNK),
                         GLA_SUB, 0, True)
    m_p = _merge(ca_p, ob_p, sig_p, wc, wg, tm_p, tn)
    h_p, n2_p = _oproj(m_p, xp, wo, g2, tm_o)
    act_p, ffn_p = _ffn_up(n2_p, w_ffn_up, fcw, fb, lay_p, tm_p, fw,
                           init=_init_rows(gt_s[rows_s - (CONV_W - 1):]))
    y_p = _ffn_down(act_p, wd, h_p, gf, tm_o, dff // 2)

    last_rows = lambda a, c: a[:n_short].reshape(bs, sl, c)[:, sl - (CONV_W - 1):]
    untile = lambda st: jnp.swapaxes(st, 1, 2).reshape(st.shape[0], CONV_W - 1, -1)
    return (y_p.reshape(bp, seq, d), y_s[:n_short].reshape(bs, sl, d),
            untile(conv_p)[None], s_p[None], untile(ffn_p)[None],
            last_rows(u_s, dc)[None], s_smp[None], last_rows(gt_s, dff)[None])
```

```python
import functools

import jax
import jax.numpy as jnp
from jax import lax
from jax.experimental import pallas as pl
from jax.experimental.pallas import tpu as pltpu

EPS = 1e-6
GATE_TAU = 16.0
CONV_W = 3
GLA_SUB = 16
HALO = 8
V7X_VMEM_LIMIT = 56 * 1024 * 1024
LANES = 128
ROW_TILE = 1024
ROW_TILE_RESIDENT = 512
ROW_TILE_WIDE_K = 256
GLA_CHUNK = 128
CONV_COLS = 256
FFN_COLS = 512
SHORT_SEQS_PER_STEP = 8
SUB_ROWS = 256
CAST_ROWS = 64
F32 = jnp.float32
BF16 = jnp.bfloat16


def _cparams(*sem):
    return pltpu.CompilerParams(dimension_semantics=sem, vmem_limit_bytes=V7X_VMEM_LIMIT)


def _row_tile(rows, target):
    best = None
    for t in range(16, min(rows, target) + 1, 16):
        if rows % t == 0:
            best = t
    return best or rows


def _rms(x, g):
    return x * lax.rsqrt(jnp.mean(x * x, axis=-1, keepdims=True) + EPS) * g


def _sigmoid(x):
    return 0.5 * jnp.tanh(0.5 * x) + 0.5


def _silu(x):
    h = 0.5 * x
    return h * jnp.tanh(h) + h


def _dot(a, b):
    return jnp.dot(a, b, preferred_element_type=F32)


def _dot_nt(a, b):
    return lax.dot_general(a, b, (((1,), (1,)), ((), ())), preferred_element_type=F32)


def _sliced(tm, matmul, epilogue):
    starts = list(range(0, tm, _row_tile(tm, SUB_ROWS)))
    rs = starts[1] if len(starts) > 1 else tm
    p = matmul(starts[0], rs)
    for idx, r0 in enumerate(starts):
        p_next = matmul(starts[idx + 1], rs) if idx + 1 < len(starts) else None
        epilogue(r0, rs, p)
        p = p_next


def _cast_into(dst_ref, row0, col0, src_ref):
    rows, cols = src_ref.shape

    def body(r, carry):
        off = pl.multiple_of(r * CAST_ROWS, CAST_ROWS)
        dst_ref[pl.ds(row0 + off, CAST_ROWS), col0:col0 + cols] = src_ref[pl.ds(off, CAST_ROWS), :].astype(BF16)
        return carry

    lax.fori_loop(0, rows // CAST_ROWS, body, 0)


def _rmsnorm_gate_kernel(x_ref, g_ref, wa_ref, wup_ref, b_ref, n_ref, lg_ref):
    n = _rms(x_ref[...], g_ref[...]).astype(BF16)
    n_ref[...] = n
    a = _dot_nt(n, wa_ref[...]).astype(BF16)
    z = _dot(a, wup_ref[...]) + b_ref[...]
    lg_ref[...] = (jnp.minimum(z, 0.0) - jnp.log(1.0 + jnp.exp(-jnp.abs(z)))) * (1.0 / GATE_TAU)


def _rmsnorm_gate(x, g, wa, wup, b, tm):
    rows, d = x.shape
    rp, dk = wup.shape
    const = lambda i: (0, 0)
    return pl.pallas_call(
        _rmsnorm_gate_kernel, grid=(rows // tm,),
        in_specs=[pl.BlockSpec((tm, d), lambda i: (i, 0)), pl.BlockSpec((1, d), const),
                  pl.BlockSpec((rp, d), const), pl.BlockSpec((rp, dk), const), pl.BlockSpec((1, dk), const)],
        out_specs=[pl.BlockSpec((tm, d), lambda i: (i, 0)), pl.BlockSpec((tm, dk), lambda i: (i, 0))],
        out_shape=[jax.ShapeDtypeStruct((rows, d), BF16), jax.ShapeDtypeStruct((rows, dk), F32)],
        compiler_params=_cparams("arbitrary"), name="rmsnorm_gate")(x, g, wa, wup, b)


class SeqLayout:
    def __init__(self, mode, seq_rows=None, n_short=None, short_len=None):
        self.mode, self.seq_rows, self.n_short, self.short_len = mode, seq_rows, n_short, short_len


def _load_halo(halo_ref, init_ref, carry_ref, i, tiles_per_seq):
    first = (i % tiles_per_seq) == 0

    @pl.when(first)
    def _():
        halo_ref[...] = init_ref[...]

    @pl.when(jnp.logical_not(first))
    def _():
        halo_ref[...] = carry_ref[...]


def _store_tail(tail, carry_ref, st_ref, i, j, tiles_per_seq):
    carry_ref[...] = tail

    @pl.when((i % tiles_per_seq) == tiles_per_seq - 1)
    def _():
        st_ref[i // tiles_per_seq, j] = tail[HALO - (CONV_W - 1):, :]


def _shift_rows(x, prev_row):
    r = pltpu.roll(x, 1, 0)
    first = lax.broadcasted_iota(jnp.int32, (HALO, 1), 0) == 0
    head = jnp.where(first, prev_row, r[:HALO])
    return jnp.concatenate([head, r[HALO:]], axis=0) if x.shape[0] > HALO else head


class _CausalConv:
    def __init__(self, w_ref, lay, halo, prev2_ref=None, prev1_ref=None):
        w = w_ref[...]
        self.w0, self.w1, self.w2 = w[0:1, :], w[1:2, :], w[2:3, :]
        self.lay, self.tail, self.prev2_ref, self.prev1_ref = lay, halo, prev2_ref, prev1_ref

    def __call__(self, u, r0):
        rs, c = u.shape
        lay, w0, w1 = self.lay, self.w0, self.w1
        um2, um1 = self.tail[HALO - 2:HALO - 1, :], self.tail[HALO - 1:HALO, :]
        s0 = _shift_rows(w0 * u, w0 * um1)
        s1_first = w1 * um1 + w0 * um2
        if lay.mode == "table":
            take = max(0, min(lay.n_short, r0 + rs) - r0)

            def table_rows(ref):
                parts = ([ref[r0:r0 + take, :]] if take else []) + (
                    [jnp.zeros((rs - take, c), F32)] if take < rs else [])
                return parts[0] if len(parts) == 1 else jnp.concatenate(parts, axis=0)

            p2, p1 = table_rows(self.prev2_ref), table_rows(self.prev1_ref)
            t = r0 + lax.broadcasted_iota(jnp.int32, (rs, 1), 0)
            starts = jnp.where(t < lay.n_short, t % lay.short_len, t - lay.n_short) == 0
            s0 = jnp.where(starts, w0 * p1, s0)
        a = w1 * u + s0
        s1 = _shift_rows(a, s1_first)
        if lay.mode == "table":
            s1 = jnp.where(starts, w1 * p1 + w0 * p2, s1)
        self.tail = u[rs - HALO:, :]
        return self.w2 * u + s1


def _conv_specs(lay, rows, tm, c, nj, init, prev2, prev1):
    if lay.mode == "carry":
        nseq = rows // lay.seq_rows
        return ((init,), [pl.BlockSpec((HALO, c), lambda j, i: (0, j))],
                pl.BlockSpec((nseq, nj, CONV_W - 1, c), lambda j, i: (0, 0, 0, 0)),
                jax.ShapeDtypeStruct((nseq, nj, CONV_W - 1, c), F32),
                [pltpu.VMEM((HALO, c), F32)] * 2)
    return ((prev2, prev1), [pl.BlockSpec((lay.n_short, c), lambda j, i: (0, j))] * 2,
            pl.BlockSpec((tm, c), lambda j, i: (i, j)),
            jax.ShapeDtypeStruct((rows, nj * c), F32), [])


def _inproj_conv_kernel(lay, tm, cw, tiles_per_seq, *refs):
    n_ref, wb_ref, wc_ref, wh_ref, cwt_ref = refs[:5]
    if lay.mode == "carry":
        init_ref, ca_ref, st_ref, wbf_ref, halo_ref, carry_ref = refs[5:]
    else:
        prev2_ref, prev1_ref, ca_ref, u_ref, wbf_ref = refs[5:]
    j, i = pl.program_id(0), pl.program_id(1)

    @pl.when(i == 0)
    def _():
        for part, w_ref in enumerate((wb_ref, wc_ref, wh_ref)):
            _cast_into(wbf_ref, part * cw, 0, w_ref)

    if lay.mode == "carry":
        _load_halo(halo_ref, init_ref, carry_ref, i, tiles_per_seq)
        conv = _CausalConv(cwt_ref, lay, halo_ref[...])
    else:
        conv = _CausalConv(cwt_ref, lay, jnp.zeros((HALO, cw), F32), prev2_ref, prev1_ref)

    def matmul(r0, rs):
        return _dot_nt(n_ref[r0:r0 + rs, :], wbf_ref[...])

    def epilogue(r0, rs, p):
        u = p[:, cw:2 * cw] * p[:, 2 * cw:]
        if lay.mode == "table":
            u_ref[r0:r0 + rs, :] = u
        ca_ref[r0:r0 + rs, :] = (p[:, :cw] * conv(u, r0)).astype(BF16)

    _sliced(tm, matmul, epilogue)
    if lay.mode == "carry":
        _store_tail(conv.tail, carry_ref, st_ref, i, j, tiles_per_seq)


def _inproj_conv(n, w_in_t, conv_w, lay, tm, cw, init=None, prev2=None, prev1=None):
    rows, d = n.shape
    dc = conv_w.shape[1]
    nj = dc // cw
    tiles_per_seq = (lay.seq_rows // tm) if lay.mode == "carry" else 1
    xargs, xspecs, xout_spec, xout_shape, xscratch = _conv_specs(lay, rows, tm, cw, nj, init, prev2, prev1)
    wspec = lambda part: pl.BlockSpec((None, cw, d), lambda j, i: (0, part * nj + j, 0))
    return pl.pallas_call(
        functools.partial(_inproj_conv_kernel, lay, tm, cw, tiles_per_seq),
        grid=(nj, rows // tm),
        in_specs=[pl.BlockSpec((tm, d), lambda j, i: (i, 0)), wspec(0), wspec(1), wspec(2),
                  pl.BlockSpec((CONV_W, cw), lambda j, i: (0, j))] + xspecs,
        out_specs=[pl.BlockSpec((tm, cw), lambda j, i: (i, j)), xout_spec],
        out_shape=[jax.ShapeDtypeStruct((rows, dc), BF16), xout_shape],
        scratch_shapes=[pltpu.VMEM((3 * cw, d), BF16)] + xscratch,
        compiler_params=_cparams("arbitrary", "arbitrary"),
        name="inproj_conv_" + lay.mode)(n, w_in_t, w_in_t, w_in_t, conv_w, *xargs)


def _inproj_act_kernel(tm, n_lin, n_silu, q_scale, n_ref, w_ref, o_ref, wbf_ref):
    j, i = pl.program_id(0), pl.program_id(1)

    @pl.when(i == 0)
    def _():
        _cast_into(wbf_ref, 0, 0, w_ref.at[0] if len(w_ref.shape) == 3 else w_ref)

    def matmul(r0, rs):
        return _dot_nt(n_ref[r0:r0 + rs, :], wbf_ref[...])

    def epilogue(r0, rs, p):
        if n_lin:
            act = p * jnp.where(j == 0, q_scale, 1.0)
            if n_silu:
                act = jnp.where(j >= n_lin, _silu(p), act)
        else:
            act = jnp.where(j < n_silu, _silu(p), _sigmoid(p)) if n_silu else _sigmoid(p)
        o_ref[r0:r0 + rs, :] = act.astype(BF16)

    _sliced(tm, matmul, epilogue)


def _inproj_act(n, w_t, w_spec, tiles, tm, tn, q_scale):
    rows, d = n.shape
    n_lin, n_silu, n_sig = tiles
    assert not (n_lin and n_sig)
    n_col = n_lin + n_silu + n_sig
    return pl.pallas_call(
        functools.partial(_inproj_act_kernel, tm, n_lin, n_silu, q_scale),
        grid=(n_col, rows // tm),
        in_specs=[pl.BlockSpec((tm, d), lambda j, i: (i, 0)), w_spec],
        out_specs=pl.BlockSpec((tm, tn), lambda j, i: (i, j)),
        out_shape=jax.ShapeDtypeStruct((rows, n_col * tn), BF16),
        scratch_shapes=[pltpu.VMEM((tn, d), BF16)],
        compiler_params=_cparams("arbitrary", "arbitrary"), name="inproj_act")(n, w_t)


def _cumsum_groups(x, sub):
    rows = x.shape[0]
    pos = lax.broadcasted_iota(jnp.int32, (rows, 1), 0) % sub
    s = 1
    while s < sub:
        x = x + jnp.where(pos >= s, pltpu.roll(x, s, 0), 0.0)
        s *= 2
    return x


def _gla_chunk(q, k, v, lg, s, sub):
    c, dk = q.shape
    nsub = c // sub
    bt = _cumsum_groups(lg, sub)
    r = jnp.zeros((1, dk), F32)
    r_sub, b_rows = [], []
    for i in range(nsub):
        r_sub.append(r)
        b_rows.append(bt[i * sub:(i + 1) * sub, :] + r)
        r = r + bt[(i + 1) * sub - 1:(i + 1) * sub, :]
    b = jnp.concatenate(b_rows, axis=0) if nsub > 1 else b_rows[0]
    b_last = r
    qe = (q * jnp.exp(b)).astype(BF16)
    kd = (k * jnp.exp(b_last - b)).astype(BF16)
    qt = q * jnp.exp(bt)
    att_rows = []
    for i in range(nsub):
        seen = (i + 1) * sub
        ke = (k[:seen] * jnp.exp(r_sub[i] - b[:seen])).astype(BF16)
        if seen < c:
            ke = jnp.concatenate([ke, jnp.zeros((c - seen, dk), BF16)], axis=0)
        a = lax.dot_general(qt[i * sub:(i + 1) * sub, :].astype(BF16), ke,
                            (((1,), (1,)), ((), ())), preferred_element_type=F32)
        col = lax.broadcasted_iota(jnp.int32, (sub, c), 1)
        rloc = lax.broadcasted_iota(jnp.int32, (sub, c), 0)
        att_rows.append(jnp.where(col <= rloc + i * sub, a, 0.0))
    att = (jnp.concatenate(att_rows, axis=0) if nsub > 1 else att_rows[0]).astype(BF16)
    o = _dot(att, v) + _dot(qe, s.astype(BF16))
    upd = lax.dot_general(kd, v, (((0,), (0,)), ((), ())), preferred_element_type=F32)
    dl = jnp.exp(b_last)
    dl_col = jnp.transpose(jnp.broadcast_to(dl, (LANES, dk)))
    dv = s.shape[1]
    s_dec = jnp.concatenate([s[:, n * LANES:(n + 1) * LANES] * dl_col for n in range(dv // LANES)], axis=1)
    return o, s_dec + upd


def _gla_out(o, gn, gs):
    return (_rms(o, gn) * gs.astype(F32)).astype(BF16)


def _gla_seq_kernel(sub, heads, dk, dv, q_ref, k_ref, v_ref, lg_ref, gs_ref, gn_ref, s0_ref, ob_ref, s_ref):
    @pl.when(pl.program_id(1) == 0)
    def _():
        s_ref[...] = s0_ref[...]

    for h in range(heads):
        ck, cv = slice(h * dk, (h + 1) * dk), slice(h * dv, (h + 1) * dv)
        o, s_new = _gla_chunk(q_ref[:, ck].astype(F32), k_ref[:, ck].astype(F32), v_ref[:, cv], lg_ref[:, ck],
                              s_ref[0, h], sub)
        s_ref[0, h] = s_new
        ob_ref[:, cv] = _gla_out(o, gn_ref[...], gs_ref[:, cv])


def _gla_seq(lin, logg, gn, s0, heads, dk, dv, nseq, seq_rows, chunk, sub, row0, shared_init):
    nchunk = seq_rows // chunk
    blk0 = row0 // chunk
    rb = lambda b, c: blk0 + b * nchunk + c
    wk, wv = heads * dk, heads * dv
    assert (2 * wk) % wv == 0
    return pl.pallas_call(
        functools.partial(_gla_seq_kernel, sub, heads, dk, dv),
        grid=(nseq, nchunk),
        in_specs=[pl.BlockSpec((chunk, wk), lambda b, c: (rb(b, c), 0)),
                  pl.BlockSpec((chunk, wk), lambda b, c: (rb(b, c), 1)),
                  pl.BlockSpec((chunk, wv), lambda b, c: (rb(b, c), 2 * wk // wv)),
                  pl.BlockSpec((chunk, wk), lambda b, c: (rb(b, c), 0)),
                  pl.BlockSpec((chunk, wv), lambda b, c: (rb(b, c), 2 * wk // wv + 1)),
                  pl.BlockSpec((1, dv), lambda b, c: (0, 0)),
                  pl.BlockSpec((1, heads, dk, dv), lambda b, c: (0 if shared_init else b, 0, 0, 0))],
        out_specs=[pl.BlockSpec((chunk, wv), lambda b, c: (b * nchunk + c, 0)),
                   pl.BlockSpec((1, heads, dk, dv), lambda b, c: (b, 0, 0, 0))],
        out_shape=[jax.ShapeDtypeStruct((nseq * seq_rows, wv), BF16),
                   jax.ShapeDtypeStruct((nseq, heads, dk, dv), F32)],
        compiler_params=_cparams("arbitrary", "arbitrary"),
        name="gla_seq")(lin, lin, lin, logg, lin, gn, s0)


def _gla_short_kernel(nb, sl, q_ref, k_ref, v_ref, lg_ref, gs_ref, gn_ref, s0_ref, ob_ref, s_ref):
    q = q_ref[...].astype(F32)
    k = k_ref[...].astype(F32)
    v = v_ref[...].astype(F32)
    lg = lg_ref[...]
    outs = []
    for n in range(nb):
        rs = slice(n * sl, (n + 1) * sl)
        o, s_new = _gla_chunk(q[rs], k[rs], v[rs].astype(BF16), lg[rs], s0_ref[n, 0], sl)
        s_ref[n, 0] = s_new
        outs.append(o)
    o = jnp.concatenate(outs, axis=0)
    ob_ref[...] = _gla_out(o, gn_ref[...], gs_ref[...])


def _gla_short(lin, logg, gn, s0, heads, dk, dv, nseq, sl, nb):
    rows = nb * sl
    kq = (heads * dk) // dk
    kv = (2 * heads * dk) // dv
    return pl.pallas_call(
        functools.partial(_gla_short_kernel, nb, sl),
        grid=(nseq // nb, heads),
        in_specs=[pl.BlockSpec((rows, dk), lambda b, h: (b, h)),
                  pl.BlockSpec((rows, dk), lambda b, h: (b, kq + h)),
                  pl.BlockSpec((rows, dv), lambda b, h: (b, kv + h)),
                  pl.BlockSpec((rows, dk), lambda b, h: (b, h)),
                  pl.BlockSpec((rows, dv), lambda b, h: (b, kv + heads + h)),
                  pl.BlockSpec((1, dv), lambda b, h: (0, 0)),
                  pl.BlockSpec((nb, 1, dk, dv), lambda b, h: (b, h, 0, 0))],
        out_specs=[pl.BlockSpec((rows, dv), lambda b, h: (b, h)),
                   pl.BlockSpec((nb, 1, dk, dv), lambda b, h: (b, h, 0, 0))],
        out_shape=[jax.ShapeDtypeStruct((nseq * sl, heads * dv), BF16),
                   jax.ShapeDtypeStruct((nseq, heads, dk, dv), F32)],
        compiler_params=_cparams("arbitrary", "arbitrary"),
        name="gla_short")(lin, lin, lin, logg, lin, gn, s0)


def _merge_kernel(ca_ref, ob_ref, ga_ref, gb_ref, wc_ref, wg_ref, o_ref):
    ya = _dot(ca_ref[...], wc_ref[...])
    yb = _dot(ob_ref[...], wg_ref[...])
    o_ref[...] = (ga_ref[...].astype(F32) * ya + gb_ref[...].astype(F32) * yb).astype(BF16)


def _merge(ca, ob, sig, wc, wg, tm, tn):
    rows, dc = ca.shape
    dg, d = wg.shape
    nn = d // tn
    return pl.pallas_call(
        _merge_kernel, grid=(rows // tm, nn),
        in_specs=[pl.BlockSpec((tm, dc), lambda i, j: (i, 0)), pl.BlockSpec((tm, dg), lambda i, j: (i, 0)),
                  pl.BlockSpec((tm, tn), lambda i, j: (i, j)), pl.BlockSpec((tm, tn), lambda i, j: (i, nn + j)),
                  pl.BlockSpec((dc, tn), lambda i, j: (0, j)), pl.BlockSpec((dg, tn), lambda i, j: (0, j))],
        out_specs=pl.BlockSpec((tm, tn), lambda i, j: (i, j)),
        out_shape=jax.ShapeDtypeStruct((rows, d), BF16),
        compiler_params=_cparams("arbitrary", "arbitrary"), name="merge")(ca, ob, sig, sig, wc, wg)


def _oproj_kernel(m_ref, x_ref, w_ref, g_ref, h_ref, n2_ref):
    h = x_ref[...] + _dot(m_ref[...], w_ref[...])
    h_ref[...] = h
    n2_ref[...] = _rms(h, g_ref[...]).astype(BF16)


def _oproj(m, x, w, g, tm):
    rows, d = x.shape
    row = lambda i: (i, 0)
    return pl.pallas_call(
        _oproj_kernel, grid=(rows // tm,),
        in_specs=[pl.BlockSpec((tm, d), row), pl.BlockSpec((tm, d), row),
                  pl.BlockSpec((d, d), lambda i: (0, 0)), pl.BlockSpec((1, d), lambda i: (0, 0))],
        out_specs=[pl.BlockSpec((tm, d), row), pl.BlockSpec((tm, d), row)],
        out_shape=[jax.ShapeDtypeStruct((rows, d), F32), jax.ShapeDtypeStruct((rows, d), BF16)],
        compiler_params=_cparams("arbitrary"), name="oproj")(m, x, w, g)


def _ffn_up_kernel(lay, tm, fw, tiles_per_seq, *refs):
    n_ref, wa_ref, wg_ref, cwt_ref, b_ref = refs[:5]
    if lay.mode == "carry":
        init_ref, act_ref, st_ref, wbf_ref, halo_ref, carry_ref = refs[5:]
    else:
        prev2_ref, prev1_ref, act_ref, gt_ref, wbf_ref = refs[5:]
    j, i = pl.program_id(0), pl.program_id(1)

    @pl.when(i == 0)
    def _():
        _cast_into(wbf_ref, 0, 0, wa_ref)
        _cast_into(wbf_ref, 0, fw, wg_ref)

    if lay.mode == "carry":
        _load_halo(halo_ref, init_ref, carry_ref, i, tiles_per_seq)
        conv = _CausalConv(cwt_ref, lay, halo_ref[...])
    else:
        conv = _CausalConv(cwt_ref, lay, jnp.zeros((HALO, fw), F32), prev2_ref, prev1_ref)

    def matmul(r0, rs):
        return _dot(n_ref[r0:r0 + rs, :], wbf_ref[...])

    def epilogue(r0, rs, p):
        gt = p[:, fw:]
        if lay.mode == "table":
            gt_ref[r0:r0 + rs, :] = gt
        z = conv(gt, r0) + b_ref[...]
        act_ref[r0:r0 + rs, :] = (_silu(z) * p[:, :fw]).astype(BF16)

    _sliced(tm, matmul, epilogue)
    if lay.mode == "carry":
        _store_tail(conv.tail, carry_ref, st_ref, i, j, tiles_per_seq)


def _ffn_up(n2, w_up, conv_w, bias, lay, tm, fw, init=None, prev2=None, prev1=None):
    rows, d = n2.shape
    dff = conv_w.shape[1]
    nj = dff // fw
    tiles_per_seq = (lay.seq_rows // tm) if lay.mode == "carry" else 1
    xargs, xspecs, xout_spec, xout_shape, xscratch = _conv_specs(lay, rows, tm, fw, nj, init, prev2, prev1)
    wspec = lambda part: pl.BlockSpec((None, d, fw), lambda j, i: (0, 0, part * nj + j))
    return pl.pallas_call(
        functools.partial(_ffn_up_kernel, lay, tm, fw, tiles_per_seq),
        grid=(nj, rows // tm),
        in_specs=[pl.BlockSpec((tm, d), lambda j, i: (i, 0)), wspec(0), wspec(1),
                  pl.BlockSpec((CONV_W, fw), lambda j, i: (0, j)), pl.BlockSpec((1, fw), lambda j, i: (0, j))]
                 + xspecs,
        out_specs=[pl.BlockSpec((tm, fw), lambda j, i: (i, j)), xout_spec],
        out_shape=[jax.ShapeDtypeStruct((rows, dff), BF16), xout_shape],
        scratch_shapes=[pltpu.VMEM((d, 2 * fw), BF16)] + xscratch,
        compiler_params=_cparams("arbitrary", "arbitrary"),
        name="ffn_up_" + lay.mode)(n2, w_up, w_up, conv_w, bias, *xargs)


def _ffn_down_kernel(act_ref, w_ref, h_ref, g_ref, y_ref):
    y_ref[...] = _rms(h_ref[...] + _dot(act_ref[...], w_ref[...]), g_ref[...])


def _ffn_down(act, w, h, g, tm):
    rows, dff = act.shape
    d = w.shape[1]
    row = lambda i: (i, 0)
    return pl.pallas_call(
        _ffn_down_kernel, grid=(rows // tm,),
        in_specs=[pl.BlockSpec((tm, dff), row),
                  pl.BlockSpec((dff, d), lambda i: (0, 0), pipeline_mode=pl.Buffered(1)),
                  pl.BlockSpec((tm, d), row), pl.BlockSpec((1, d), lambda i: (0, 0))],
        out_specs=pl.BlockSpec((tm, d), row),
        out_shape=jax.ShapeDtypeStruct((rows, d), F32),
        compiler_params=_cparams("arbitrary"), name="ffn_down")(act, w, h, g)


def _init_rows(state2):
    return jnp.pad(state2, ((HALO - (CONV_W - 1), 0), (0, 0)))


def kernel(x_prompt, x_sample, state_conv, state_gla, state_ffn_conv, meta_tokens, norm_mix_g, w_in, conv_mix_w, w_conv_out, w_gate_up, b_gate, gla_norm_g, w_gla_out, w_o, norm_ffn_g, w_ffn_up, ffn_conv_w, ffn_conv_b, w_ffn_down, final_norm_g):
    bp, seq, d = x_prompt.shape
    bs, sl, _ = x_sample.shape
    assert w_in.shape[0] == 1, "single-layer step"
    n_meta = meta_tokens.shape[0]
    dc = state_conv.shape[-1]
    _, _, heads, dk, dv = state_gla.shape
    dff = state_ffn_conv.shape[-1]
    rank = w_gate_up.shape[1]
    assert n_meta % GLA_SUB == 0 and seq % GLA_SUB == 0 and GLA_SUB % sl == 0 and sl >= CONV_W - 1

    o_q = 3 * dc
    o_a = o_q + 2 * heads * dk + 2 * heads * dv
    o_ga = o_a + rank
    tn = heads * dk
    assert o_q % tn == 0 and (heads * dv) % tn == 0 and d % tn == 0
    cw, fw = CONV_COLS, FFN_COLS
    tiles_qkvg = ((2 * heads * dk + heads * dv) // tn, heads * dv // tn, 0)
    tiles_gates = (0, 0, 2 * d // tn)
    spec_qkvg = pl.BlockSpec((None, tn, d), lambda j, i: (0, o_q // tn + j, 0))
    spec_gates = pl.BlockSpec((pl.Element(1), pl.Element(tn), pl.Element(d)), lambda j, i: (0, (o_ga // 8 + j * (tn // 8)) * 8, 0))
    assert o_ga % 8 == 0
    q_scale = float(dk) ** -0.5

    w_in_t = jnp.swapaxes(w_in, 1, 2)
    wa = jnp.pad(w_in_t[0, o_a:o_ga], ((0, LANES - rank), (0, 0))).astype(BF16)
    wup = jnp.pad(w_gate_up[0], ((0, LANES - rank), (0, 0))).astype(BF16)
    wc = w_conv_out[0].astype(BF16)
    wg = w_gla_out[0].astype(BF16)
    wo = w_o[0].astype(BF16)
    wd = w_ffn_down[0].astype(BF16)
    g1, g2, gf, gn = norm_mix_g[0][None], norm_ffn_g[0][None], final_norm_g[None], gla_norm_g[0][None]
    bg, fb, cmw, fcw = b_gate[0][None], ffn_conv_b[0][None], conv_mix_w[0], ffn_conv_w[0]

    def in_projections(n, tm, lay, **conv_kw):
        ca, conv_out = _inproj_conv(n, w_in_t, cmw, lay, tm, cw, **conv_kw)
        lin = _inproj_act(n, w_in_t, spec_qkvg, tiles_qkvg, tm, tn, q_scale)
        sig = _inproj_act(n, w_in_t, spec_gates, tiles_gates, tm, tn, 1.0)
        return ca, conv_out, lin, sig

    n_short = bs * sl
    rows_s = n_short + n_meta
    xs = jnp.concatenate([x_sample.reshape(n_short, d), meta_tokens.astype(x_sample.dtype)], axis=0)
    lay_s = SeqLayout("table", n_short=n_short, short_len=sl)
    tm_s = rows_s
    rep = lambda st, r: jnp.repeat(st[:, r], sl, axis=0)
    n_s, logg_s = _rmsnorm_gate(xs, g1, wa, wup, bg, _row_tile(rows_s, ROW_TILE_RESIDENT))
    ca_s, u_s, lin_s, sig_s = in_projections(
        n_s, tm_s, lay_s, prev2=rep(state_conv[0], 0), prev1=rep(state_conv[0], 1))
    s_zero = jnp.zeros((1, heads, dk, dv), F32)
    ob_m, s_meta = _gla_seq(lin_s, logg_s, gn, s_zero, heads, dk, dv, 1, n_meta, n_meta, GLA_SUB,
                            n_short, True)
    ob_smp, s_smp = _gla_short(lin_s, logg_s, gn, state_gla[0], heads, dk, dv, bs, sl,
                               SHORT_SEQS_PER_STEP)
    ob_s = jnp.concatenate([ob_smp, ob_m], axis=0)
    m_s = _merge(ca_s, ob_s, sig_s, wc, wg, tm_s, tn)
    h_s, n2_s = _oproj(m_s, xs, wo, g2, _row_tile(rows_s, ROW_TILE_RESIDENT))
    act_s, gt_s = _ffn_up(n2_s, w_ffn_up, fcw, fb, lay_s, tm_s, fw,
                          prev2=rep(state_ffn_conv[0], 0), prev1=rep(state_ffn_conv[0], 1))
    y_s = _ffn_down(act_s, wd, h_s, gf, _row_tile(rows_s, ROW_TILE_WIDE_K))

    rows_p = bp * seq
    xp = x_prompt.reshape(rows_p, d)
    lay_p = SeqLayout("carry", seq_rows=seq)
    tm_p = _row_tile(seq, ROW_TILE)
    tm_o = _row_tile(seq, ROW_TILE_RESIDENT)
    n_p, logg_p = _rmsnorm_gate(xp, g1, wa, wup, bg, tm_o)
    ca_p, conv_p, lin_p, sig_p = in_projections(
        n_p, tm_p, lay_p, init=_init_rows(u_s[rows_s - (CONV_W - 1):]))
    ob_p, s_p = _gla_seq(lin_p, logg_p, gn, s_meta, heads, dk, dv, bp, seq, _row_tile(seq, GLA_CHUNK),
                         GLA_SUB, 0, True)
    m_p = _merge(ca_p, ob_p, sig_p, wc, wg, tm_p, tn)
    h_p, n2_p = _oproj(m_p, xp, wo, g2, tm_o)
    act_p, ffn_p = _ffn_up(n2_p, w_ffn_up, fcw, fb, lay_p, tm_p, fw,
                           init=_init_rows(gt_s[rows_s - (CONV_W - 1):]))
    y_p = _ffn_down(act_p, wd, h_p, gf, _row_tile(seq, ROW_TILE_WIDE_K))

    last_rows = lambda a, c: a[:n_short].reshape(bs, sl, c)[:, sl - (CONV_W - 1):]
    untile = lambda st: jnp.swapaxes(st, 1, 2).reshape(st.shape[0], CONV_W - 1, -1)
    return (y_p.reshape(bp, seq, d), y_s[:n_short].reshape(bs, sl, d),
            untile(conv_p)[None], s_p[None], untile(ffn_p)[None],
            last_rows(u_s, dc)[None], s_smp[None], last_rows(gt_s, dff)[None])
```

```python
import functools

import jax
import jax.numpy as jnp
from jax import lax
from jax.experimental import pallas as pl
from jax.experimental.pallas import tpu as pltpu

EPS = 1e-6
GATE_TAU = 16.0
CONV_W = 3
GLA_SUB = 16
HALO = 8
V7X_VMEM_LIMIT = 56 * 1024 * 1024
LANES = 128
ROW_TILE = 2048
ROW_TILE_RESIDENT = 512
ROW_TILE_MERGE = 1024
ROW_TILE_WIDE_K = 256
GLA_CHUNK = 128
CONV_COLS = 256
FFN_COLS = 512
SHORT_SEQS_PER_STEP = 16
SUB_ROWS = 256
CAST_ROWS = 64
F32 = jnp.float32
BF16 = jnp.bfloat16


def _cparams(*sem):
    return pltpu.CompilerParams(dimension_semantics=sem, vmem_limit_bytes=V7X_VMEM_LIMIT)


def _row_tile(rows, target):
    best = None
    for t in range(16, min(rows, target) + 1, 16):
        if rows % t == 0:
            best = t
    return best or rows


def _rms(x, g):
    return x * lax.rsqrt(jnp.mean(x * x, axis=-1, keepdims=True) + EPS) * g


def _sigmoid(x):
    return 0.5 * jnp.tanh(0.5 * x) + 0.5


def _silu(x):
    h = 0.5 * x
    return h * jnp.tanh(h) + h


def _dot(a, b):
    return jnp.dot(a, b, preferred_element_type=F32)


def _dot_nt(a, b):
    return lax.dot_general(a, b, (((1,), (1,)), ((), ())), preferred_element_type=F32)


def _sliced(tm, matmul, epilogue):
    starts = list(range(0, tm, _row_tile(tm, SUB_ROWS)))
    rs = starts[1] if len(starts) > 1 else tm
    p = matmul(starts[0], rs)
    for idx, r0 in enumerate(starts):
        p_next = matmul(starts[idx + 1], rs) if idx + 1 < len(starts) else None
        epilogue(r0, rs, p)
        p = p_next


def _cast_into(dst_ref, row0, col0, src_ref):
    rows, cols = src_ref.shape

    def body(r, carry):
        off = pl.multiple_of(r * CAST_ROWS, CAST_ROWS)
        dst_ref[pl.ds(row0 + off, CAST_ROWS), col0:col0 + cols] = src_ref[pl.ds(off, CAST_ROWS), :].astype(BF16)
        return carry

    lax.fori_loop(0, rows // CAST_ROWS, body, 0)


def _rmsnorm_gate_kernel(x_ref, g_ref, wa_ref, wup_ref, b_ref, n_ref, lg_ref):
    n = _rms(x_ref[...], g_ref[...]).astype(BF16)
    n_ref[...] = n
    a = _dot_nt(n, wa_ref[...]).astype(BF16)
    z = _dot(a, wup_ref[...]) + b_ref[...]
    lg_ref[...] = (jnp.minimum(z, 0.0) - jnp.log(1.0 + jnp.exp(-jnp.abs(z)))) * (1.0 / GATE_TAU)


def _rmsnorm_gate(x, g, wa, wup, b, tm):
    rows, d = x.shape
    rp, dk = wup.shape
    const = lambda i: (0, 0)
    return pl.pallas_call(
        _rmsnorm_gate_kernel, grid=(rows // tm,),
        in_specs=[pl.BlockSpec((tm, d), lambda i: (i, 0)), pl.BlockSpec((1, d), const),
                  pl.BlockSpec((rp, d), const), pl.BlockSpec((rp, dk), const), pl.BlockSpec((1, dk), const)],
        out_specs=[pl.BlockSpec((tm, d), lambda i: (i, 0)), pl.BlockSpec((tm, dk), lambda i: (i, 0))],
        out_shape=[jax.ShapeDtypeStruct((rows, d), BF16), jax.ShapeDtypeStruct((rows, dk), F32)],
        compiler_params=_cparams("arbitrary"), name="rmsnorm_gate")(x, g, wa, wup, b)


class SeqLayout:
    def __init__(self, mode, seq_rows=None, n_short=None, short_len=None):
        self.mode, self.seq_rows, self.n_short, self.short_len = mode, seq_rows, n_short, short_len


def _load_halo(halo_ref, init_ref, carry_ref, i, tiles_per_seq):
    first = (i % tiles_per_seq) == 0

    @pl.when(first)
    def _():
        halo_ref[...] = init_ref[...]

    @pl.when(jnp.logical_not(first))
    def _():
        halo_ref[...] = carry_ref[...]


def _store_tail(tail, carry_ref, st_ref, i, j, tiles_per_seq):
    carry_ref[...] = tail

    @pl.when((i % tiles_per_seq) == tiles_per_seq - 1)
    def _():
        st_ref[i // tiles_per_seq, j] = tail[HALO - (CONV_W - 1):, :]


def _shift_rows(x, prev_row):
    r = pltpu.roll(x, 1, 0)
    first = lax.broadcasted_iota(jnp.int32, (HALO, 1), 0) == 0
    head = jnp.where(first, prev_row, r[:HALO])
    return jnp.concatenate([head, r[HALO:]], axis=0) if x.shape[0] > HALO else head


class _CausalConv:
    def __init__(self, w_ref, lay, halo, prev2_ref=None, prev1_ref=None):
        w = w_ref[...]
        self.w0, self.w1, self.w2 = w[0:1, :], w[1:2, :], w[2:3, :]
        self.lay, self.tail, self.prev2_ref, self.prev1_ref = lay, halo, prev2_ref, prev1_ref

    def __call__(self, u, r0):
        rs, c = u.shape
        lay, w0, w1 = self.lay, self.w0, self.w1
        um2, um1 = self.tail[HALO - 2:HALO - 1, :], self.tail[HALO - 1:HALO, :]
        s0 = _shift_rows(w0 * u, w0 * um1)
        s1_first = w1 * um1 + w0 * um2
        if lay.mode == "table":
            take = max(0, min(lay.n_short, r0 + rs) - r0)

            def table_rows(ref):
                parts = ([ref[r0:r0 + take, :]] if take else []) + (
                    [jnp.zeros((rs - take, c), F32)] if take < rs else [])
                return parts[0] if len(parts) == 1 else jnp.concatenate(parts, axis=0)

            p2, p1 = table_rows(self.prev2_ref), table_rows(self.prev1_ref)
            t = r0 + lax.broadcasted_iota(jnp.int32, (rs, 1), 0)
            starts = jnp.where(t < lay.n_short, t % lay.short_len, t - lay.n_short) == 0
            s0 = jnp.where(starts, w0 * p1, s0)
        a = w1 * u + s0
        s1 = _shift_rows(a, s1_first)
        if lay.mode == "table":
            s1 = jnp.where(starts, w1 * p1 + w0 * p2, s1)
        self.tail = u[rs - HALO:, :]
        return self.w2 * u + s1


def _conv_specs(lay, rows, tm, c, nj, init, prev2, prev1):
    if lay.mode == "carry":
        nseq = rows // lay.seq_rows
        return ((init,), [pl.BlockSpec((HALO, c), lambda j, i: (0, j))],
                pl.BlockSpec((nseq, nj, CONV_W - 1, c), lambda j, i: (0, 0, 0, 0)),
                jax.ShapeDtypeStruct((nseq, nj, CONV_W - 1, c), F32),
                [pltpu.VMEM((HALO, c), F32)] * 2)
    return ((prev2, prev1), [pl.BlockSpec((lay.n_short, c), lambda j, i: (0, j))] * 2,
            pl.BlockSpec((tm, c), lambda j, i: (i, j)),
            jax.ShapeDtypeStruct((rows, nj * c), F32), [])


def _inproj_conv_kernel(lay, tm, cw, tiles_per_seq, *refs):
    n_ref, wb_ref, wc_ref, wh_ref, cwt_ref = refs[:5]
    if lay.mode == "carry":
        init_ref, ca_ref, st_ref, wbf_ref, halo_ref, carry_ref = refs[5:]
    else:
        prev2_ref, prev1_ref, ca_ref, u_ref, wbf_ref = refs[5:]
    j, i = pl.program_id(0), pl.program_id(1)

    @pl.when(i == 0)
    def _():
        for part, w_ref in enumerate((wb_ref, wc_ref, wh_ref)):
            _cast_into(wbf_ref, part * cw, 0, w_ref)

    if lay.mode == "carry":
        _load_halo(halo_ref, init_ref, carry_ref, i, tiles_per_seq)
        conv = _CausalConv(cwt_ref, lay, halo_ref[...])
    else:
        conv = _CausalConv(cwt_ref, lay, jnp.zeros((HALO, cw), F32), prev2_ref, prev1_ref)

    def matmul(r0, rs):
        return _dot_nt(n_ref[r0:r0 + rs, :], wbf_ref[...])

    def epilogue(r0, rs, p):
        u = p[:, cw:2 * cw] * p[:, 2 * cw:]
        if lay.mode == "table":
            u_ref[r0:r0 + rs, :] = u
        ca_ref[r0:r0 + rs, :] = (p[:, :cw] * conv(u, r0)).astype(BF16)

    _sliced(tm, matmul, epilogue)
    if lay.mode == "carry":
        _store_tail(conv.tail, carry_ref, st_ref, i, j, tiles_per_seq)


def _inproj_conv(n, w_in_t, conv_w, lay, tm, cw, init=None, prev2=None, prev1=None):
    rows, d = n.shape
    dc = conv_w.shape[1]
    nj = dc // cw
    tiles_per_seq = (lay.seq_rows // tm) if lay.mode == "carry" else 1
    xargs, xspecs, xout_spec, xout_shape, xscratch = _conv_specs(lay, rows, tm, cw, nj, init, prev2, prev1)
    wspec = lambda part: pl.BlockSpec((None, cw, d), lambda j, i: (0, part * nj + j, 0))
    return pl.pallas_call(
        functools.partial(_inproj_conv_kernel, lay, tm, cw, tiles_per_seq),
        grid=(nj, rows // tm),
        in_specs=[pl.BlockSpec((tm, d), lambda j, i: (i, 0)), wspec(0), wspec(1), wspec(2),
                  pl.BlockSpec((CONV_W, cw), lambda j, i: (0, j))] + xspecs,
        out_specs=[pl.BlockSpec((tm, cw), lambda j, i: (i, j)), xout_spec],
        out_shape=[jax.ShapeDtypeStruct((rows, dc), BF16), xout_shape],
        scratch_shapes=[pltpu.VMEM((3 * cw, d), BF16)] + xscratch,
        compiler_params=_cparams("arbitrary", "arbitrary"),
        name="inproj_conv_" + lay.mode)(n, w_in_t, w_in_t, w_in_t, conv_w, *xargs)


def _inproj_act_kernel(tm, n_lin, n_silu, q_scale, n_ref, w_ref, o_ref, wbf_ref):
    j, i = pl.program_id(0), pl.program_id(1)

    @pl.when(i == 0)
    def _():
        _cast_into(wbf_ref, 0, 0, w_ref.at[0] if len(w_ref.shape) == 3 else w_ref)

    def matmul(r0, rs):
        return _dot_nt(n_ref[r0:r0 + rs, :], wbf_ref[...])

    def epilogue(r0, rs, p):
        if n_lin:
            act = p * jnp.where(j == 0, q_scale, 1.0)
            if n_silu:
                act = jnp.where(j >= n_lin, _silu(p), act)
        else:
            act = jnp.where(j < n_silu, _silu(p), _sigmoid(p)) if n_silu else _sigmoid(p)
        o_ref[r0:r0 + rs, :] = act.astype(BF16)

    _sliced(tm, matmul, epilogue)


def _inproj_act(n, w_t, w_spec, tiles, tm, tn, q_scale):
    rows, d = n.shape
    n_lin, n_silu, n_sig = tiles
    assert not (n_lin and n_sig)
    n_col = n_lin + n_silu + n_sig
    return pl.pallas_call(
        functools.partial(_inproj_act_kernel, tm, n_lin, n_silu, q_scale),
        grid=(n_col, rows // tm),
        in_specs=[pl.BlockSpec((tm, d), lambda j, i: (i, 0)), w_spec],
        out_specs=pl.BlockSpec((tm, tn), lambda j, i: (i, j)),
        out_shape=jax.ShapeDtypeStruct((rows, n_col * tn), BF16),
        scratch_shapes=[pltpu.VMEM((tn, d), BF16)],
        compiler_params=_cparams("arbitrary", "arbitrary"), name="inproj_act")(n, w_t)


def _cumsum_groups(x, sub):
    rows = x.shape[0]
    pos = lax.broadcasted_iota(jnp.int32, (rows, 1), 0) % sub
    s = 1
    while s < sub:
        x = x + jnp.where(pos >= s, pltpu.roll(x, s, 0), 0.0)
        s *= 2
    return x


def _gla_chunk(q, k, v, lg, s, sub):
    c, dk = q.shape
    nsub = c // sub
    bt = _cumsum_groups(lg, sub)
    r = jnp.zeros((1, dk), F32)
    r_sub, b_rows = [], []
    for i in range(nsub):
        r_sub.append(r)
        b_rows.append(bt[i * sub:(i + 1) * sub, :] + r)
        r = r + bt[(i + 1) * sub - 1:(i + 1) * sub, :]
    b = jnp.concatenate(b_rows, axis=0) if nsub > 1 else b_rows[0]
    b_last = r
    qe = (q * jnp.exp(b)).astype(BF16)
    kd = (k * jnp.exp(b_last - b)).astype(BF16)
    qt = q * jnp.exp(bt)
    att_rows = []
    for i in range(nsub):
        seen = (i + 1) * sub
        ke = (k[:seen] * jnp.exp(r_sub[i] - b[:seen])).astype(BF16)
        if seen < c:
            ke = jnp.concatenate([ke, jnp.zeros((c - seen, dk), BF16)], axis=0)
        a = lax.dot_general(qt[i * sub:(i + 1) * sub, :].astype(BF16), ke,
                            (((1,), (1,)), ((), ())), preferred_element_type=F32)
        col = lax.broadcasted_iota(jnp.int32, (sub, c), 1)
        rloc = lax.broadcasted_iota(jnp.int32, (sub, c), 0)
        att_rows.append(jnp.where(col <= rloc + i * sub, a, 0.0))
    att = (jnp.concatenate(att_rows, axis=0) if nsub > 1 else att_rows[0]).astype(BF16)
    o = _dot(att, v) + _dot(qe, s.astype(BF16))
    upd = lax.dot_general(kd, v, (((0,), (0,)), ((), ())), preferred_element_type=F32)
    dl = jnp.exp(b_last)
    dl_col = jnp.transpose(jnp.broadcast_to(dl, (LANES, dk)))
    dv = s.shape[1]
    s_dec = jnp.concatenate([s[:, n * LANES:(n + 1) * LANES] * dl_col for n in range(dv // LANES)], axis=1)
    return o, s_dec + upd


def _gla_out(o, gn, gs):
    return (_rms(o, gn) * gs.astype(F32)).astype(BF16)


def _gla_seq_kernel(sub, heads, dk, dv, q_ref, k_ref, v_ref, lg_ref, gs_ref, gn_ref, s0_ref, ob_ref, s_ref):
    @pl.when(pl.program_id(1) == 0)
    def _():
        s_ref[...] = s0_ref[...]

    for h in range(heads):
        ck, cv = slice(h * dk, (h + 1) * dk), slice(h * dv, (h + 1) * dv)
        o, s_new = _gla_chunk(q_ref[:, ck].astype(F32), k_ref[:, ck].astype(F32), v_ref[:, cv], lg_ref[:, ck],
                              s_ref[0, h], sub)
        s_ref[0, h] = s_new
        ob_ref[:, cv] = _gla_out(o, gn_ref[...], gs_ref[:, cv])


def _gla_seq(lin, logg, gn, s0, heads, dk, dv, nseq, seq_rows, chunk, sub, row0, shared_init):
    nchunk = seq_rows // chunk
    blk0 = row0 // chunk
    rb = lambda b, c: blk0 + b * nchunk + c
    wk, wv = heads * dk, heads * dv
    assert (2 * wk) % wv == 0
    return pl.pallas_call(
        functools.partial(_gla_seq_kernel, sub, heads, dk, dv),
        grid=(nseq, nchunk),
        in_specs=[pl.BlockSpec((chunk, wk), lambda b, c: (rb(b, c), 0)),
                  pl.BlockSpec((chunk, wk), lambda b, c: (rb(b, c), 1)),
                  pl.BlockSpec((chunk, wv), lambda b, c: (rb(b, c), 2 * wk // wv)),
                  pl.BlockSpec((chunk, wk), lambda b, c: (rb(b, c), 0)),
                  pl.BlockSpec((chunk, wv), lambda b, c: (rb(b, c), 2 * wk // wv + 1)),
                  pl.BlockSpec((1, dv), lambda b, c: (0, 0)),
                  pl.BlockSpec((1, heads, dk, dv), lambda b, c: (0 if shared_init else b, 0, 0, 0))],
        out_specs=[pl.BlockSpec((chunk, wv), lambda b, c: (b * nchunk + c, 0)),
                   pl.BlockSpec((1, heads, dk, dv), lambda b, c: (b, 0, 0, 0))],
        out_shape=[jax.ShapeDtypeStruct((nseq * seq_rows, wv), BF16),
                   jax.ShapeDtypeStruct((nseq, heads, dk, dv), F32)],
        compiler_params=_cparams("arbitrary", "arbitrary"),
        name="gla_seq")(lin, lin, lin, logg, lin, gn, s0)


def _gla_short_kernel(nb, sl, q_ref, k_ref, v_ref, lg_ref, gs_ref, gn_ref, s0_ref, ob_ref, s_ref):
    q = q_ref[...].astype(F32)
    k = k_ref[...].astype(F32)
    v = v_ref[...].astype(F32)
    lg = lg_ref[...]
    outs = []
    for n in range(nb):
        rs = slice(n * sl, (n + 1) * sl)
        o, s_new = _gla_chunk(q[rs], k[rs], v[rs].astype(BF16), lg[rs], s0_ref[n, 0], sl)
        s_ref[n, 0] = s_new
        outs.append(o)
    o = jnp.concatenate(outs, axis=0)
    ob_ref[...] = _gla_out(o, gn_ref[...], gs_ref[...])


def _gla_short(lin, logg, gn, s0, heads, dk, dv, nseq, sl, nb):
    rows = nb * sl
    kq = (heads * dk) // dk
    kv = (2 * heads * dk) // dv
    return pl.pallas_call(
        functools.partial(_gla_short_kernel, nb, sl),
        grid=(nseq // nb, heads),
        in_specs=[pl.BlockSpec((rows, dk), lambda b, h: (b, h)),
                  pl.BlockSpec((rows, dk), lambda b, h: (b, kq + h)),
                  pl.BlockSpec((rows, dv), lambda b, h: (b, kv + h)),
                  pl.BlockSpec((rows, dk), lambda b, h: (b, h)),
                  pl.BlockSpec((rows, dv), lambda b, h: (b, kv + heads + h)),
                  pl.BlockSpec((1, dv), lambda b, h: (0, 0)),
                  pl.BlockSpec((nb, 1, dk, dv), lambda b, h: (b, h, 0, 0))],
        out_specs=[pl.BlockSpec((rows, dv), lambda b, h: (b, h)),
                   pl.BlockSpec((nb, 1, dk, dv), lambda b, h: (b, h, 0, 0))],
        out_shape=[jax.ShapeDtypeStruct((nseq * sl, heads * dv), BF16),
                   jax.ShapeDtypeStruct((nseq, heads, dk, dv), F32)],
        compiler_params=_cparams("arbitrary", "arbitrary"),
        name="gla_short")(lin, lin, lin, logg, lin, gn, s0)


def _merge_kernel(ca_ref, ob_ref, ga_ref, gb_ref, wc_ref, wg_ref, o_ref):
    ya = _dot(ca_ref[...], wc_ref[...])
    yb = _dot(ob_ref[...], wg_ref[...])
    o_ref[...] = (ga_ref[...].astype(F32) * ya + gb_ref[...].astype(F32) * yb).astype(BF16)


def _merge(ca, ob, sig, wc, wg, tm, tn):
    rows, dc = ca.shape
    dg, d = wg.shape
    nn = d // tn
    return pl.pallas_call(
        _merge_kernel, grid=(rows // tm, nn),
        in_specs=[pl.BlockSpec((tm, dc), lambda i, j: (i, 0)), pl.BlockSpec((tm, dg), lambda i, j: (i, 0)),
                  pl.BlockSpec((tm, tn), lambda i, j: (i, j)), pl.BlockSpec((tm, tn), lambda i, j: (i, nn + j)),
                  pl.BlockSpec((dc, tn), lambda i, j: (0, j)), pl.BlockSpec((dg, tn), lambda i, j: (0, j))],
        out_specs=pl.BlockSpec((tm, tn), lambda i, j: (i, j)),
        out_shape=jax.ShapeDtypeStruct((rows, d), BF16),
        compiler_params=_cparams("arbitrary", "arbitrary"), name="merge")(ca, ob, sig, sig, wc, wg)


def _oproj_kernel(m_ref, x_ref, w_ref, g_ref, h_ref, n2_ref):
    h = x_ref[...] + _dot(m_ref[...], w_ref[...])
    h_ref[...] = h
    n2_ref[...] = _rms(h, g_ref[...]).astype(BF16)


def _oproj(m, x, w, g, tm):
    rows, d = x.shape
    row = lambda i: (i, 0)
    return pl.pallas_call(
        _oproj_kernel, grid=(rows // tm,),
        in_specs=[pl.BlockSpec((tm, d), row), pl.BlockSpec((tm, d), row),
                  pl.BlockSpec((d, d), lambda i: (0, 0)), pl.BlockSpec((1, d), lambda i: (0, 0))],
        out_specs=[pl.BlockSpec((tm, d), row), pl.BlockSpec((tm, d), row)],
        out_shape=[jax.ShapeDtypeStruct((rows, d), F32), jax.ShapeDtypeStruct((rows, d), BF16)],
        compiler_params=_cparams("arbitrary"), name="oproj")(m, x, w, g)


def _ffn_up_kernel(lay, tm, fw, tiles_per_seq, *refs):
    n_ref, wa_ref, wg_ref, cwt_ref, b_ref = refs[:5]
    if lay.mode == "carry":
        init_ref, act_ref, st_ref, wbf_ref, halo_ref, carry_ref = refs[5:]
    else:
        prev2_ref, prev1_ref, act_ref, gt_ref, wbf_ref = refs[5:]
    j, i = pl.program_id(0), pl.program_id(1)

    @pl.when(i == 0)
    def _():
        _cast_into(wbf_ref, 0, 0, wa_ref)
        _cast_into(wbf_ref, 0, fw, wg_ref)

    if lay.mode == "carry":
        _load_halo(halo_ref, init_ref, carry_ref, i, tiles_per_seq)
        conv = _CausalConv(cwt_ref, lay, halo_ref[...])
    else:
        conv = _CausalConv(cwt_ref, lay, jnp.zeros((HALO, fw), F32), prev2_ref, prev1_ref)

    def matmul(r0, rs):
        return _dot(n_ref[r0:r0 + rs, :], wbf_ref[...])

    def epilogue(r0, rs, p):
        gt = p[:, fw:]
        if lay.mode == "table":
            gt_ref[r0:r0 + rs, :] = gt
        z = conv(gt, r0) + b_ref[...]
        act_ref[r0:r0 + rs, :] = (_silu(z) * p[:, :fw]).astype(BF16)

    _sliced(tm, matmul, epilogue)
    if lay.mode == "carry":
        _store_tail(conv.tail, carry_ref, st_ref, i, j, tiles_per_seq)


def _ffn_up(n2, w_up, conv_w, bias, lay, tm, fw, init=None, prev2=None, prev1=None):
    rows, d = n2.shape
    dff = conv_w.shape[1]
    nj = dff // fw
    tiles_per_seq = (lay.seq_rows // tm) if lay.mode == "carry" else 1
    xargs, xspecs, xout_spec, xout_shape, xscratch = _conv_specs(lay, rows, tm, fw, nj, init, prev2, prev1)
    wspec = lambda part: pl.BlockSpec((None, d, fw), lambda j, i: (0, 0, part * nj + j))
    return pl.pallas_call(
        functools.partial(_ffn_up_kernel, lay, tm, fw, tiles_per_seq),
        grid=(nj, rows // tm),
        in_specs=[pl.BlockSpec((tm, d), lambda j, i: (i, 0)), wspec(0), wspec(1),
                  pl.BlockSpec((CONV_W, fw), lambda j, i: (0, j)), pl.BlockSpec((1, fw), lambda j, i: (0, j))]
                 + xspecs,
        out_specs=[pl.BlockSpec((tm, fw), lambda j, i: (i, j)), xout_spec],
        out_shape=[jax.ShapeDtypeStruct((rows, dff), BF16), xout_shape],
        scratch_shapes=[pltpu.VMEM((d, 2 * fw), BF16)] + xscratch,
        compiler_params=_cparams("arbitrary", "arbitrary"),
        name="ffn_up_" + lay.mode)(n2, w_up, w_up, conv_w, bias, *xargs)


def _ffn_down_kernel(act_ref, w_ref, h_ref, g_ref, y_ref):
    y_ref[...] = _rms(h_ref[...] + _dot(act_ref[...], w_ref[...]), g_ref[...])


def _ffn_down(act, w, h, g, tm):
    rows, dff = act.shape
    d = w.shape[1]
    row = lambda i: (i, 0)
    return pl.pallas_call(
        _ffn_down_kernel, grid=(rows // tm,),
        in_specs=[pl.BlockSpec((tm, dff), row),
                  pl.BlockSpec((dff, d), lambda i: (0, 0), pipeline_mode=pl.Buffered(1)),
                  pl.BlockSpec((tm, d), row), pl.BlockSpec((1, d), lambda i: (0, 0))],
        out_specs=pl.BlockSpec((tm, d), row),
        out_shape=jax.ShapeDtypeStruct((rows, d), F32),
        compiler_params=_cparams("arbitrary"), name="ffn_down")(act, w, h, g)


def _init_rows(state2):
    return jnp.pad(state2, ((HALO - (CONV_W - 1), 0), (0, 0)))


def kernel(x_prompt, x_sample, state_conv, state_gla, state_ffn_conv, meta_tokens, norm_mix_g, w_in, conv_mix_w, w_conv_out, w_gate_up, b_gate, gla_norm_g, w_gla_out, w_o, norm_ffn_g, w_ffn_up, ffn_conv_w, ffn_conv_b, w_ffn_down, final_norm_g):
    bp, seq, d = x_prompt.shape
    bs, sl, _ = x_sample.shape
    assert w_in.shape[0] == 1, "single-layer step"
    n_meta = meta_tokens.shape[0]
    dc = state_conv.shape[-1]
    _, _, heads, dk, dv = state_gla.shape
    dff = state_ffn_conv.shape[-1]
    rank = w_gate_up.shape[1]
    assert n_meta % GLA_SUB == 0 and seq % GLA_SUB == 0 and GLA_SUB % sl == 0 and sl >= CONV_W - 1

    o_q = 3 * dc
    o_a = o_q + 2 * heads * dk + 2 * heads * dv
    o_ga = o_a + rank
    tn = heads * dk
    assert o_q % tn == 0 and (heads * dv) % tn == 0 and d % tn == 0
    cw, fw = CONV_COLS, FFN_COLS
    tiles_qkvg = ((2 * heads * dk + heads * dv) // tn, heads * dv // tn, 0)
    tiles_gates = (0, 0, 2 * d // tn)
    spec_qkvg = pl.BlockSpec((None, tn, d), lambda j, i: (0, o_q // tn + j, 0))
    spec_gates = pl.BlockSpec((pl.Element(1), pl.Element(tn), pl.Element(d)), lambda j, i: (0, (o_ga // 8 + j * (tn // 8)) * 8, 0))
    assert o_ga % 8 == 0
    q_scale = float(dk) ** -0.5

    w_in_t = jnp.swapaxes(w_in, 1, 2)
    wa = jnp.pad(w_in_t[0, o_a:o_ga], ((0, LANES - rank), (0, 0))).astype(BF16)
    wup = jnp.pad(w_gate_up[0], ((0, LANES - rank), (0, 0))).astype(BF16)
    wc = w_conv_out[0].astype(BF16)
    wg = w_gla_out[0].astype(BF16)
    wo = w_o[0].astype(BF16)
    wd = w_ffn_down[0].astype(BF16)
    g1, g2, gf, gn = norm_mix_g[0][None], norm_ffn_g[0][None], final_norm_g[None], gla_norm_g[0][None]
    bg, fb, cmw, fcw = b_gate[0][None], ffn_conv_b[0][None], conv_mix_w[0], ffn_conv_w[0]

    def in_projections(n, tm, lay, **conv_kw):
        ca, conv_out = _inproj_conv(n, w_in_t, cmw, lay, tm, cw, **conv_kw)
        lin = _inproj_act(n, w_in_t, spec_qkvg, tiles_qkvg, tm, tn, q_scale)
        sig = _inproj_act(n, w_in_t, spec_gates, tiles_gates, tm, tn, 1.0)
        return ca, conv_out, lin, sig

    n_short = bs * sl
    rows_s = n_short + n_meta
    xs = jnp.concatenate([x_sample.reshape(n_short, d), meta_tokens.astype(x_sample.dtype)], axis=0)
    lay_s = SeqLayout("table", n_short=n_short, short_len=sl)
    tm_s = rows_s
    rep = lambda st, r: jnp.repeat(st[:, r], sl, axis=0)
    n_s, logg_s = _rmsnorm_gate(xs, g1, wa, wup, bg, _row_tile(rows_s, ROW_TILE_RESIDENT))
    ca_s, u_s, lin_s, sig_s = in_projections(
        n_s, tm_s, lay_s, prev2=rep(state_conv[0], 0), prev1=rep(state_conv[0], 1))
    s_zero = jnp.zeros((1, heads, dk, dv), F32)
    ob_m, s_meta = _gla_seq(lin_s, logg_s, gn, s_zero, heads, dk, dv, 1, n_meta, n_meta, GLA_SUB,
                            n_short, True)
    ob_smp, s_smp = _gla_short(lin_s, logg_s, gn, state_gla[0], heads, dk, dv, bs, sl,
                               SHORT_SEQS_PER_STEP)
    ob_s = jnp.concatenate([ob_smp, ob_m], axis=0)
    m_s = _merge(ca_s, ob_s, sig_s, wc, wg, tm_s, tn)
    h_s, n2_s = _oproj(m_s, xs, wo, g2, _row_tile(rows_s, ROW_TILE_RESIDENT))
    act_s, gt_s = _ffn_up(n2_s, w_ffn_up, fcw, fb, lay_s, tm_s, fw,
                          prev2=rep(state_ffn_conv[0], 0), prev1=rep(state_ffn_conv[0], 1))
    y_s = _ffn_down(act_s, wd, h_s, gf, _row_tile(rows_s, ROW_TILE_WIDE_K))

    rows_p = bp * seq
    xp = x_prompt.reshape(rows_p, d)
    lay_p = SeqLayout("carry", seq_rows=seq)
    tm_p = _row_tile(seq, ROW_TILE)
    tm_o = _row_tile(seq, ROW_TILE_RESIDENT)
    n_p, logg_p = _rmsnorm_gate(xp, g1, wa, wup, bg, tm_o)
    ca_p, conv_p, lin_p, sig_p = in_projections(
        n_p, tm_p, lay_p, init=_init_rows(u_s[rows_s - (CONV_W - 1):]))
    ob_p, s_p = _gla_seq(lin_p, logg_p, gn, s_meta, heads, dk, dv, bp, seq, _row_tile(seq, GLA_CHUNK),
                         GLA_SUB, 0, True)
    m_p = _merge(ca_p, ob_p, sig_p, wc, wg, _row_tile(seq, ROW_TILE_MERGE), tn)
    h_p, n2_p = _oproj(m_p, xp, wo, g2, tm_o)
    act_p, ffn_p = _ffn_up(n2_p, w_ffn_up, fcw, fb, lay_p, tm_p, fw,
                           init=_init_rows(gt_s[rows_s - (CONV_W - 1):]))
    y_p = _ffn_down(act_p, wd, h_p, gf, _row_tile(seq, ROW_TILE_WIDE_K))

    last_rows = lambda a, c: a[:n_short].reshape(bs, sl, c)[:, sl - (CONV_W - 1):]
    untile = lambda st: jnp.swapaxes(st, 1, 2).reshape(st.shape[0], CONV_W - 1, -1)
    return (y_p.reshape(bp, seq, d), y_s[:n_short].reshape(bs, sl, d),
            untile(conv_p)[None], s_p[None], untile(ffn_p)[None],
            last_rows(u_s, dc)[None], s_smp[None], last_rows(gt_s, dff)[None])
```

```python
import functools

import jax
import jax.numpy as jnp
from jax import lax
from jax.experimental import pallas as pl
from jax.experimental.pallas import tpu as pltpu

EPS = 1e-6
GATE_TAU = 16.0
CONV_W = 3
GLA_SUB = 16
HALO = 8
V7X_VMEM_LIMIT = 56 * 1024 * 1024
LANES = 128
ROW_TILE = 2048
ROW_TILE_RESIDENT = 512
ROW_TILE_MERGE = 1024
ROW_TILE_WIDE_K = 256
GLA_CHUNK = 128
CONV_COLS = 256
FFN_COLS = 512
SHORT_SEQS_PER_STEP = 16
SUB_ROWS = 256
CAST_ROWS = 64
F32 = jnp.float32
BF16 = jnp.bfloat16


def _cparams(*sem):
    return pltpu.CompilerParams(dimension_semantics=sem, vmem_limit_bytes=V7X_VMEM_LIMIT)


def _row_tile(rows, target):
    best = None
    for t in range(16, min(rows, target) + 1, 16):
        if rows % t == 0:
            best = t
    return best or rows


def _rms(x, g):
    return x * lax.rsqrt(jnp.mean(x * x, axis=-1, keepdims=True) + EPS) * g


def _sigmoid(x):
    return 0.5 * jnp.tanh(0.5 * x) + 0.5


def _silu(x):
    h = 0.5 * x
    return h * jnp.tanh(h) + h


def _dot(a, b):
    return jnp.dot(a, b, preferred_element_type=F32)


def _dot_nt(a, b):
    return lax.dot_general(a, b, (((1,), (1,)), ((), ())), preferred_element_type=F32)


def _sliced(tm, matmul, epilogue):
    n = max(1, tm // SUB_ROWS)
    sizes = [SUB_ROWS] * (n - 1) + [tm - SUB_ROWS * (n - 1)]
    r0 = 0
    for rs in sizes:
        epilogue(r0, rs, matmul(r0, rs))
        r0 += rs


def _cast_into(dst_ref, row0, col0, src_ref):
    rows, cols = src_ref.shape

    def body(r, carry):
        off = pl.multiple_of(r * CAST_ROWS, CAST_ROWS)
        dst_ref[pl.ds(row0 + off, CAST_ROWS), col0:col0 + cols] = src_ref[pl.ds(off, CAST_ROWS), :].astype(BF16)
        return carry

    lax.fori_loop(0, rows // CAST_ROWS, body, 0)


def _rmsnorm_gate_kernel(x_ref, g_ref, wa_ref, wup_ref, b_ref, n_ref, lg_ref):
    n = _rms(x_ref[...], g_ref[...]).astype(BF16)
    n_ref[...] = n
    a = _dot_nt(n, wa_ref[...]).astype(BF16)
    z = _dot(a, wup_ref[...]) + b_ref[...]
    lg_ref[...] = (jnp.minimum(z, 0.0) - jnp.log(1.0 + jnp.exp(-jnp.abs(z)))) * (1.0 / GATE_TAU)


def _rmsnorm_gate(x, g, wa, wup, b, tm):
    rows, d = x.shape
    rp, dk = wup.shape
    const = lambda i: (0, 0)
    return pl.pallas_call(
        _rmsnorm_gate_kernel, grid=(rows // tm,),
        in_specs=[pl.BlockSpec((tm, d), lambda i: (i, 0)), pl.BlockSpec((1, d), const),
                  pl.BlockSpec((rp, d), const), pl.BlockSpec((rp, dk), const), pl.BlockSpec((1, dk), const)],
        out_specs=[pl.BlockSpec((tm, d), lambda i: (i, 0)), pl.BlockSpec((tm, dk), lambda i: (i, 0))],
        out_shape=[jax.ShapeDtypeStruct((rows, d), BF16), jax.ShapeDtypeStruct((rows, dk), F32)],
        compiler_params=_cparams("arbitrary"), name="rmsnorm_gate")(x, g, wa, wup, b)


class SeqLayout:
    def __init__(self, mode, seq_rows=None, n_short=None, short_len=None):
        self.mode, self.seq_rows, self.n_short, self.short_len = mode, seq_rows, n_short, short_len


def _load_halo(halo_ref, init_ref, carry_ref, i, tiles_per_seq):
    first = (i % tiles_per_seq) == 0

    @pl.when(first)
    def _():
        halo_ref[...] = init_ref[...]

    @pl.when(jnp.logical_not(first))
    def _():
        halo_ref[...] = carry_ref[...]


def _store_tail(tail, carry_ref, st_ref, i, j, tiles_per_seq):
    carry_ref[...] = tail

    @pl.when((i % tiles_per_seq) == tiles_per_seq - 1)
    def _():
        st_ref[i // tiles_per_seq, j] = tail[HALO - (CONV_W - 1):, :]


def _shift_rows(x, prev_row):
    r = pltpu.roll(x, 1, 0)
    first = lax.broadcasted_iota(jnp.int32, (HALO, 1), 0) == 0
    head = jnp.where(first, prev_row, r[:HALO])
    return jnp.concatenate([head, r[HALO:]], axis=0) if x.shape[0] > HALO else head


class _CausalConv:
    def __init__(self, w_ref, lay, halo, prev2_ref=None, prev1_ref=None):
        w = w_ref[...]
        self.w0, self.w1, self.w2 = w[0:1, :], w[1:2, :], w[2:3, :]
        self.lay, self.tail, self.prev2_ref, self.prev1_ref = lay, halo, prev2_ref, prev1_ref

    def __call__(self, u, r0):
        rs, c = u.shape
        lay, w0, w1 = self.lay, self.w0, self.w1
        um2, um1 = self.tail[HALO - 2:HALO - 1, :], self.tail[HALO - 1:HALO, :]
        s0 = _shift_rows(w0 * u, w0 * um1)
        s1_first = w1 * um1 + w0 * um2
        if lay.mode == "table":
            take = max(0, min(lay.n_short, r0 + rs) - r0)

            def table_rows(ref):
                parts = ([ref[r0:r0 + take, :]] if take else []) + (
                    [jnp.zeros((rs - take, c), F32)] if take < rs else [])
                return parts[0] if len(parts) == 1 else jnp.concatenate(parts, axis=0)

            p2, p1 = table_rows(self.prev2_ref), table_rows(self.prev1_ref)
            t = r0 + lax.broadcasted_iota(jnp.int32, (rs, 1), 0)
            starts = jnp.where(t < lay.n_short, t % lay.short_len, t - lay.n_short) == 0
            s0 = jnp.where(starts, w0 * p1, s0)
        a = w1 * u + s0
        s1 = _shift_rows(a, s1_first)
        if lay.mode == "table":
            s1 = jnp.where(starts, w1 * p1 + w0 * p2, s1)
        self.tail = u[rs - HALO:, :]
        return self.w2 * u + s1


def _conv_specs(lay, rows, tm, c, nj, init, prev2, prev1):
    if lay.mode == "carry":
        nseq = rows // lay.seq_rows
        return ((init,), [pl.BlockSpec((HALO, c), lambda j, i: (0, j))],
                pl.BlockSpec((nseq, nj, CONV_W - 1, c), lambda j, i: (0, 0, 0, 0)),
                jax.ShapeDtypeStruct((nseq, nj, CONV_W - 1, c), F32),
                [pltpu.VMEM((HALO, c), F32)] * 2)
    return ((prev2, prev1), [pl.BlockSpec((lay.n_short, c), lambda j, i: (0, j))] * 2,
            pl.BlockSpec((tm, c), lambda j, i: (i, j)),
            jax.ShapeDtypeStruct((rows, nj * c), F32), [])


def _inproj_conv_kernel(lay, tm, cw, tiles_per_seq, *refs):
    n_ref, wb_ref, wc_ref, wh_ref, cwt_ref = refs[:5]
    if lay.mode == "carry":
        init_ref, ca_ref, st_ref, wbf_ref, halo_ref, carry_ref = refs[5:]
    else:
        prev2_ref, prev1_ref, ca_ref, u_ref, wbf_ref = refs[5:]
    j, i = pl.program_id(0), pl.program_id(1)

    @pl.when(i == 0)
    def _():
        for part, w_ref in enumerate((wb_ref, wc_ref, wh_ref)):
            _cast_into(wbf_ref, part * cw, 0, w_ref)

    if lay.mode == "carry":
        _load_halo(halo_ref, init_ref, carry_ref, i, tiles_per_seq)
        conv = _CausalConv(cwt_ref, lay, halo_ref[...])
    else:
        conv = _CausalConv(cwt_ref, lay, jnp.zeros((HALO, cw), F32), prev2_ref, prev1_ref)

    def matmul(r0, rs):
        return _dot_nt(n_ref[r0:r0 + rs, :], wbf_ref[...])

    def epilogue(r0, rs, p):
        u = p[:, cw:2 * cw] * p[:, 2 * cw:]
        if lay.mode == "table":
            u_ref[r0:r0 + rs, :] = u
        ca_ref[r0:r0 + rs, :] = (p[:, :cw] * conv(u, r0)).astype(BF16)

    _sliced(tm, matmul, epilogue)
    if lay.mode == "carry":
        _store_tail(conv.tail, carry_ref, st_ref, i, j, tiles_per_seq)


def _inproj_conv(n, w_in_t, conv_w, lay, tm, cw, init=None, prev2=None, prev1=None):
    rows, d = n.shape
    dc = conv_w.shape[1]
    nj = dc // cw
    tiles_per_seq = (lay.seq_rows // tm) if lay.mode == "carry" else 1
    xargs, xspecs, xout_spec, xout_shape, xscratch = _conv_specs(lay, rows, tm, cw, nj, init, prev2, prev1)
    wspec = lambda part: pl.BlockSpec((None, cw, d), lambda j, i: (0, part * nj + j, 0))
    return pl.pallas_call(
        functools.partial(_inproj_conv_kernel, lay, tm, cw, tiles_per_seq),
        grid=(nj, rows // tm),
        in_specs=[pl.BlockSpec((tm, d), lambda j, i: (i, 0)), wspec(0), wspec(1), wspec(2),
                  pl.BlockSpec((CONV_W, cw), lambda j, i: (0, j))] + xspecs,
        out_specs=[pl.BlockSpec((tm, cw), lambda j, i: (i, j)), xout_spec],
        out_shape=[jax.ShapeDtypeStruct((rows, dc), BF16), xout_shape],
        scratch_shapes=[pltpu.VMEM((3 * cw, d), BF16)] + xscratch,
        compiler_params=_cparams("arbitrary", "arbitrary"),
        name="inproj_conv_" + lay.mode)(n, w_in_t, w_in_t, w_in_t, conv_w, *xargs)


def _inproj_act_kernel(tm, n_lin, n_silu, q_scale, n_ref, w_ref, o_ref, wbf_ref):
    j, i = pl.program_id(0), pl.program_id(1)

    @pl.when(i == 0)
    def _():
        _cast_into(wbf_ref, 0, 0, w_ref.at[0] if len(w_ref.shape) == 3 else w_ref)

    def matmul(r0, rs):
        return _dot_nt(n_ref[r0:r0 + rs, :], wbf_ref[...])

    def epilogue(r0, rs, p):
        if n_lin:
            act = p * jnp.where(j == 0, q_scale, 1.0)
            if n_silu:
                act = jnp.where(j >= n_lin, _silu(p), act)
        else:
            act = jnp.where(j < n_silu, _silu(p), _sigmoid(p)) if n_silu else _sigmoid(p)
        o_ref[r0:r0 + rs, :] = act.astype(BF16)

    _sliced(tm, matmul, epilogue)


def _inproj_act(n, w_t, w_spec, tiles, tm, tn, q_scale):
    rows, d = n.shape
    n_lin, n_silu, n_sig = tiles
    assert not (n_lin and n_sig)
    n_col = n_lin + n_silu + n_sig
    return pl.pallas_call(
        functools.partial(_inproj_act_kernel, tm, n_lin, n_silu, q_scale),
        grid=(n_col, rows // tm),
        in_specs=[pl.BlockSpec((tm, d), lambda j, i: (i, 0)), w_spec],
        out_specs=pl.BlockSpec((tm, tn), lambda j, i: (i, j)),
        out_shape=jax.ShapeDtypeStruct((rows, n_col * tn), BF16),
        scratch_shapes=[pltpu.VMEM((tn, d), BF16)],
        compiler_params=_cparams("arbitrary", "arbitrary"), name="inproj_act")(n, w_t)


def _cumsum_groups(x, sub):
    rows = x.shape[0]
    pos = lax.broadcasted_iota(jnp.int32, (rows, 1), 0) % sub
    s = 1
    while s < sub:
        x = x + jnp.where(pos >= s, pltpu.roll(x, s, 0), 0.0)
        s *= 2
    return x


def _gla_chunk(q, k, v, lg, s, sub):
    c, dk = q.shape
    ngrp = c // sub
    gpu = 2 if ngrp % 2 == 0 else 1
    unit = gpu * sub
    bt = _cumsum_groups(lg, sub)
    r = jnp.zeros((1, dk), F32)
    r_grp, b_rows = [], []
    for g in range(ngrp):
        r_grp.append(r)
        b_rows.append(bt[g * sub:(g + 1) * sub, :] + r)
        r = r + bt[(g + 1) * sub - 1:(g + 1) * sub, :]
    b = jnp.concatenate(b_rows, axis=0) if ngrp > 1 else b_rows[0]
    b_last = r
    qe = (q * jnp.exp(b)).astype(BF16)
    kd = (k * jnp.exp(b_last - b)).astype(BF16)
    att_rows = []
    for i in range(c // unit):
        ref = r_grp[i * gpu + gpu - 1]
        rows = slice(i * unit, (i + 1) * unit)
        seen = (i + 1) * unit
        qt = (q[rows] * jnp.exp(b[rows] - ref)).astype(BF16)
        ke = (k[:seen] * jnp.exp(ref - b[:seen])).astype(BF16)
        if seen < c:
            ke = jnp.concatenate([ke, jnp.zeros((c - seen, dk), BF16)], axis=0)
        a = _dot_nt(qt, ke)
        col = lax.broadcasted_iota(jnp.int32, (unit, c), 1)
        rloc = lax.broadcasted_iota(jnp.int32, (unit, c), 0)
        att_rows.append(jnp.where(col <= rloc + i * unit, a, 0.0))
    att = (jnp.concatenate(att_rows, axis=0) if len(att_rows) > 1 else att_rows[0]).astype(BF16)
    o = _dot(att, v) + _dot(qe, s.astype(BF16))
    upd = lax.dot_general(kd, v, (((0,), (0,)), ((), ())), preferred_element_type=F32)
    dl = jnp.exp(b_last)
    dl_col = jnp.transpose(jnp.broadcast_to(dl, (LANES, dk)))
    dv = s.shape[1]
    s_dec = jnp.concatenate([s[:, n * LANES:(n + 1) * LANES] * dl_col for n in range(dv // LANES)], axis=1)
    return o, s_dec + upd


def _gla_out(o, gn, gs):
    return (_rms(o, gn) * gs.astype(F32)).astype(BF16)


def _gla_seq_kernel(sub, heads, dk, dv, q_ref, k_ref, v_ref, lg_ref, gs_ref, gn_ref, s0_ref, ob_ref, s_ref):
    @pl.when(pl.program_id(1) == 0)
    def _():
        def copy_head(h, carry):
            s_ref[0, h] = s0_ref[0, h]
            return carry

        lax.fori_loop(0, heads, copy_head, 0)

    for h in range(heads):
        ck, cv = slice(h * dk, (h + 1) * dk), slice(h * dv, (h + 1) * dv)
        o, s_new = _gla_chunk(q_ref[:, ck].astype(F32), k_ref[:, ck].astype(F32), v_ref[:, cv], lg_ref[:, ck],
                              s_ref[0, h], sub)
        s_ref[0, h] = s_new
        ob_ref[:, cv] = _gla_out(o, gn_ref[...], gs_ref[:, cv])


def _gla_seq(lin, logg, gn, s0, heads, dk, dv, nseq, seq_rows, chunk, sub, row0, shared_init):
    nchunk = seq_rows // chunk
    blk0 = row0 // chunk
    rb = lambda b, c: blk0 + b * nchunk + c
    wk, wv = heads * dk, heads * dv
    assert (2 * wk) % wv == 0
    return pl.pallas_call(
        functools.partial(_gla_seq_kernel, sub, heads, dk, dv),
        grid=(nseq, nchunk),
        in_specs=[pl.BlockSpec((chunk, wk), lambda b, c: (rb(b, c), 0)),
                  pl.BlockSpec((chunk, wk), lambda b, c: (rb(b, c), 1)),
                  pl.BlockSpec((chunk, wv), lambda b, c: (rb(b, c), 2 * wk // wv)),
                  pl.BlockSpec((chunk, wk), lambda b, c: (rb(b, c), 0)),
                  pl.BlockSpec((chunk, wv), lambda b, c: (rb(b, c), 2 * wk // wv + 1)),
                  pl.BlockSpec((1, dv), lambda b, c: (0, 0)),
                  pl.BlockSpec((1, heads, dk, dv), lambda b, c: (0 if shared_init else b, 0, 0, 0))],
        out_specs=[pl.BlockSpec((chunk, wv), lambda b, c: (b * nchunk + c, 0)),
                   pl.BlockSpec((1, heads, dk, dv), lambda b, c: (b, 0, 0, 0))],
        out_shape=[jax.ShapeDtypeStruct((nseq * seq_rows, wv), BF16),
                   jax.ShapeDtypeStruct((nseq, heads, dk, dv), F32)],
        compiler_params=_cparams("arbitrary", "arbitrary"),
        name="gla_seq")(lin, lin, lin, logg, lin, gn, s0)


def _gla_short_kernel(nb, sl, q_ref, k_ref, v_ref, lg_ref, gs_ref, gn_ref, s0_ref, ob_ref, s_ref):
    q = q_ref[...].astype(F32)
    k = k_ref[...].astype(F32)
    v = v_ref[...].astype(F32)
    lg = lg_ref[...]
    outs = []
    for n in range(nb):
        rs = slice(n * sl, (n + 1) * sl)
        o, s_new = _gla_chunk(q[rs], k[rs], v[rs].astype(BF16), lg[rs], s0_ref[n, 0], sl)
        s_ref[n, 0] = s_new
        outs.append(o)
    o = jnp.concatenate(outs, axis=0)
    ob_ref[...] = _gla_out(o, gn_ref[...], gs_ref[...])


def _gla_short(lin, logg, gn, s0, heads, dk, dv, nseq, sl, nb):
    rows = nb * sl
    kq = (heads * dk) // dk
    kv = (2 * heads * dk) // dv
    return pl.pallas_call(
        functools.partial(_gla_short_kernel, nb, sl),
        grid=(nseq // nb, heads),
        in_specs=[pl.BlockSpec((rows, dk), lambda b, h: (b, h)),
                  pl.BlockSpec((rows, dk), lambda b, h: (b, kq + h)),
                  pl.BlockSpec((rows, dv), lambda b, h: (b, kv + h)),
                  pl.BlockSpec((rows, dk), lambda b, h: (b, h)),
                  pl.BlockSpec((rows, dv), lambda b, h: (b, kv + heads + h)),
                  pl.BlockSpec((1, dv), lambda b, h: (0, 0)),
                  pl.BlockSpec((nb, 1, dk, dv), lambda b, h: (b, h, 0, 0))],
        out_specs=[pl.BlockSpec((rows, dv), lambda b, h: (b, h)),
                   pl.BlockSpec((nb, 1, dk, dv), lambda b, h: (b, h, 0, 0))],
        out_shape=[jax.ShapeDtypeStruct((nseq * sl, heads * dv), BF16),
                   jax.ShapeDtypeStruct((nseq, heads, dk, dv), F32)],
        compiler_params=_cparams("arbitrary", "arbitrary"),
        name="gla_short")(lin, lin, lin, logg, lin, gn, s0)


def _merge_kernel(ca_ref, ob_ref, ga_ref, gb_ref, wc_ref, wg_ref, o_ref):
    ya = _dot(ca_ref[...], wc_ref[...])
    yb = _dot(ob_ref[...], wg_ref[...])
    o_ref[...] = (ga_ref[...].astype(F32) * ya + gb_ref[...].astype(F32) * yb).astype(BF16)


def _merge(ca, ob, sig, wc, wg, tm, tn):
    rows, dc = ca.shape
    dg, d = wg.shape
    nn = d // tn
    return pl.pallas_call(
        _merge_kernel, grid=(rows // tm, nn),
        in_specs=[pl.BlockSpec((tm, dc), lambda i, j: (i, 0)), pl.BlockSpec((tm, dg), lambda i, j: (i, 0)),
                  pl.BlockSpec((tm, tn), lambda i, j: (i, j)), pl.BlockSpec((tm, tn), lambda i, j: (i, nn + j)),
                  pl.BlockSpec((dc, tn), lambda i, j: (0, j)), pl.BlockSpec((dg, tn), lambda i, j: (0, j))],
        out_specs=pl.BlockSpec((tm, tn), lambda i, j: (i, j)),
        out_shape=jax.ShapeDtypeStruct((rows, d), BF16),
        compiler_params=_cparams("arbitrary", "arbitrary"), name="merge")(ca, ob, sig, sig, wc, wg)


def _oproj_kernel(m_ref, x_ref, w_ref, g_ref, h_ref, n2_ref):
    h = x_ref[...] + _dot(m_ref[...], w_ref[...])
    h_ref[...] = h
    n2_ref[...] = _rms(h, g_ref[...]).astype(BF16)


def _oproj(m, x, w, g, tm):
    rows, d = x.shape
    row = lambda i: (i, 0)
    return pl.pallas_call(
        _oproj_kernel, grid=(rows // tm,),
        in_specs=[pl.BlockSpec((tm, d), row), pl.BlockSpec((tm, d), row),
                  pl.BlockSpec((d, d), lambda i: (0, 0)), pl.BlockSpec((1, d), lambda i: (0, 0))],
        out_specs=[pl.BlockSpec((tm, d), row), pl.BlockSpec((tm, d), row)],
        out_shape=[jax.ShapeDtypeStruct((rows, d), F32), jax.ShapeDtypeStruct((rows, d), BF16)],
        compiler_params=_cparams("arbitrary"), name="oproj")(m, x, w, g)


def _ffn_up_kernel(lay, tm, fw, tiles_per_seq, *refs):
    n_ref, wa_ref, wg_ref, cwt_ref, b_ref = refs[:5]
    if lay.mode == "carry":
        init_ref, act_ref, st_ref, wbf_ref, halo_ref, carry_ref = refs[5:]
    else:
        prev2_ref, prev1_ref, act_ref, gt_ref, wbf_ref = refs[5:]
    j, i = pl.program_id(0), pl.program_id(1)

    @pl.when(i == 0)
    def _():
        _cast_into(wbf_ref, 0, 0, wa_ref)
        _cast_into(wbf_ref, 0, fw, wg_ref)

    if lay.mode == "carry":
        _load_halo(halo_ref, init_ref, carry_ref, i, tiles_per_seq)
        conv = _CausalConv(cwt_ref, lay, halo_ref[...])
    else:
        conv = _CausalConv(cwt_ref, lay, jnp.zeros((HALO, fw), F32), prev2_ref, prev1_ref)

    def matmul(r0, rs):
        return _dot(n_ref[r0:r0 + rs, :], wbf_ref[...])

    def epilogue(r0, rs, p):
        gt = p[:, fw:]
        if lay.mode == "table":
            gt_ref[r0:r0 + rs, :] = gt
        z = conv(gt, r0) + b_ref[...]
        act_ref[r0:r0 + rs, :] = (_silu(z) * p[:, :fw]).astype(BF16)

    _sliced(tm, matmul, epilogue)
    if lay.mode == "carry":
        _store_tail(conv.tail, carry_ref, st_ref, i, j, tiles_per_seq)


def _ffn_up(n2, w_up, conv_w, bias, lay, tm, fw, init=None, prev2=None, prev1=None):
    rows, d = n2.shape
    dff = conv_w.shape[1]
    nj = dff // fw
    tiles_per_seq = (lay.seq_rows // tm) if lay.mode == "carry" else 1
    xargs, xspecs, xout_spec, xout_shape, xscratch = _conv_specs(lay, rows, tm, fw, nj, init, prev2, prev1)
    wspec = lambda part: pl.BlockSpec((None, d, fw), lambda j, i: (0, 0, part * nj + j))
    return pl.pallas_call(
        functools.partial(_ffn_up_kernel, lay, tm, fw, tiles_per_seq),
        grid=(nj, rows // tm),
        in_specs=[pl.BlockSpec((tm, d), lambda j, i: (i, 0)), wspec(0), wspec(1),
                  pl.BlockSpec((CONV_W, fw), lambda j, i: (0, j)), pl.BlockSpec((1, fw), lambda j, i: (0, j))]
                 + xspecs,
        out_specs=[pl.BlockSpec((tm, fw), lambda j, i: (i, j)), xout_spec],
        out_shape=[jax.ShapeDtypeStruct((rows, dff), BF16), xout_shape],
        scratch_shapes=[pltpu.VMEM((d, 2 * fw), BF16)] + xscratch,
        compiler_params=_cparams("arbitrary", "arbitrary"),
        name="ffn_up_" + lay.mode)(n2, w_up, w_up, conv_w, bias, *xargs)


def _ffn_down_kernel(act_ref, w_ref, h_ref, g_ref, y_ref):
    y_ref[...] = _rms(h_ref[...] + _dot(act_ref[...], w_ref[...]), g_ref[...])


def _ffn_down(act, w, h, g, tm):
    rows, dff = act.shape
    d = w.shape[1]
    row = lambda i: (i, 0)
    return pl.pallas_call(
        _ffn_down_kernel, grid=(rows // tm,),
        in_specs=[pl.BlockSpec((tm, dff), row),
                  pl.BlockSpec((dff, d), lambda i: (0, 0), pipeline_mode=pl.Buffered(1)),
                  pl.BlockSpec((tm, d), row), pl.BlockSpec((1, d), lambda i: (0, 0))],
        out_specs=pl.BlockSpec((tm, d), row),
        out_shape=jax.ShapeDtypeStruct((rows, d), F32),
        compiler_params=_cparams("arbitrary"), name="ffn_down")(act, w, h, g)


def _init_rows(state2):
    return jnp.pad(state2, ((HALO - (CONV_W - 1), 0), (0, 0)))


def kernel(x_prompt, x_sample, state_conv, state_gla, state_ffn_conv, meta_tokens, norm_mix_g, w_in, conv_mix_w, w_conv_out, w_gate_up, b_gate, gla_norm_g, w_gla_out, w_o, norm_ffn_g, w_ffn_up, ffn_conv_w, ffn_conv_b, w_ffn_down, final_norm_g):
    bp, seq, d = x_prompt.shape
    bs, sl, _ = x_sample.shape
    assert w_in.shape[0] == 1, "single-layer step"
    n_meta = meta_tokens.shape[0]
    dc = state_conv.shape[-1]
    _, _, heads, dk, dv = state_gla.shape
    dff = state_ffn_conv.shape[-1]
    rank = w_gate_up.shape[1]
    assert n_meta % GLA_SUB == 0 and seq % GLA_SUB == 0 and GLA_SUB % sl == 0 and sl >= CONV_W - 1

    o_q = 3 * dc
    o_a = o_q + 2 * heads * dk + 2 * heads * dv
    o_ga = o_a + rank
    tn = heads * dk
    assert o_q % tn == 0 and (heads * dv) % tn == 0 and d % tn == 0
    cw, fw = CONV_COLS, FFN_COLS
    tiles_qkvg = ((2 * heads * dk + heads * dv) // tn, heads * dv // tn, 0)
    tiles_gates = (0, 0, 2 * d // tn)
    spec_qkvg = pl.BlockSpec((None, tn, d), lambda j, i: (0, o_q // tn + j, 0))
    spec_gates = pl.BlockSpec((pl.Element(1), pl.Element(tn), pl.Element(d)), lambda j, i: (0, (o_ga // 8 + j * (tn // 8)) * 8, 0))
    assert o_ga % 8 == 0
    q_scale = float(dk) ** -0.5

    w_in_t = jnp.swapaxes(w_in, 1, 2)
    wa = jnp.pad(w_in_t[0, o_a:o_ga], ((0, LANES - rank), (0, 0))).astype(BF16)
    wup = jnp.pad(w_gate_up[0], ((0, LANES - rank), (0, 0))).astype(BF16)
    wc = w_conv_out[0].astype(BF16)
    wg = w_gla_out[0].astype(BF16)
    wo = w_o[0].astype(BF16)
    wd = w_ffn_down[0].astype(BF16)
    g1, g2, gf, gn = norm_mix_g[0][None], norm_ffn_g[0][None], final_norm_g[None], gla_norm_g[0][None]
    bg, fb, cmw, fcw = b_gate[0][None], ffn_conv_b[0][None], conv_mix_w[0], ffn_conv_w[0]

    def in_projections(n, tm, lay, **conv_kw):
        ca, conv_out = _inproj_conv(n, w_in_t, cmw, lay, tm, cw, **conv_kw)
        lin = _inproj_act(n, w_in_t, spec_qkvg, tiles_qkvg, tm, tn, q_scale)
        sig = _inproj_act(n, w_in_t, spec_gates, tiles_gates, tm, tn, 1.0)
        return ca, conv_out, lin, sig

    n_short = bs * sl
    rows_s = n_short + n_meta
    xs = jnp.concatenate([x_sample.reshape(n_short, d), meta_tokens.astype(x_sample.dtype)], axis=0)
    lay_s = SeqLayout("table", n_short=n_short, short_len=sl)
    tm_s = rows_s
    rep = lambda st, r: jnp.repeat(st[:, r], sl, axis=0)
    n_s, logg_s = _rmsnorm_gate(xs, g1, wa, wup, bg, _row_tile(rows_s, ROW_TILE_RESIDENT))
    ca_s, u_s, lin_s, sig_s = in_projections(
        n_s, tm_s, lay_s, prev2=rep(state_conv[0], 0), prev1=rep(state_conv[0], 1))
    s_zero = jnp.zeros((1, heads, dk, dv), F32)
    ob_m, s_meta = _gla_seq(lin_s, logg_s, gn, s_zero, heads, dk, dv, 1, n_meta, n_meta, GLA_SUB,
                            n_short, True)
    ob_smp, s_smp = _gla_short(lin_s, logg_s, gn, state_gla[0], heads, dk, dv, bs, sl,
                               SHORT_SEQS_PER_STEP)
    ob_s = jnp.concatenate([ob_smp, ob_m], axis=0)
    m_s = _merge(ca_s, ob_s, sig_s, wc, wg, tm_s, tn)
    h_s, n2_s = _oproj(m_s, xs, wo, g2, _row_tile(rows_s, ROW_TILE_RESIDENT))
    act_s, gt_s = _ffn_up(n2_s, w_ffn_up, fcw, fb, lay_s, tm_s, fw,
                          prev2=rep(state_ffn_conv[0], 0), prev1=rep(state_ffn_conv[0], 1))
    y_s = _ffn_down(act_s, wd, h_s, gf, _row_tile(rows_s, ROW_TILE_WIDE_K))

    rows_p = bp * seq
    xp = x_prompt.reshape(rows_p, d)
    lay_p = SeqLayout("carry", seq_rows=seq)
    tm_p = _row_tile(seq, ROW_TILE)
    tm_o = _row_tile(seq, ROW_TILE_RESIDENT)
    n_p, logg_p = _rmsnorm_gate(xp, g1, wa, wup, bg, tm_o)
    ca_p, conv_p, lin_p, sig_p = in_projections(
        n_p, tm_p, lay_p, init=_init_rows(u_s[rows_s - (CONV_W - 1):]))
    ob_p, s_p = _gla_seq(lin_p, logg_p, gn, s_meta, heads, dk, dv, bp, seq, _row_tile(seq, GLA_CHUNK),
                         GLA_SUB, 0, True)
    m_p = _merge(ca_p, ob_p, sig_p, wc, wg, _row_tile(seq, ROW_TILE_MERGE), tn)
    h_p, n2_p = _oproj(m_p, xp, wo, g2, tm_o)
    act_p, ffn_p = _ffn_up(n2_p, w_ffn_up, fcw, fb, lay_p, tm_p, fw,
                           init=_init_rows(gt_s[rows_s - (CONV_W - 1):]))
    y_p = _ffn_down(act_p, wd, h_p, gf, _row_tile(seq, ROW_TILE_WIDE_K))

    last_rows = lambda a, c: a[:n_short].reshape(bs, sl, c)[:, sl - (CONV_W - 1):]
    untile = lambda st: jnp.swapaxes(st, 1, 2).reshape(st.shape[0], CONV_W - 1, -1)
    return (y_p.reshape(bp, seq, d), y_s[:n_short].reshape(bs, sl, d),
            untile(conv_p)[None], s_p[None], untile(ffn_p)[None],
            last_rows(u_s, dc)[None], s_smp[None], last_rows(gt_s, dff)[None])
```

```python
import functools

import jax
import jax.numpy as jnp
from jax import lax
from jax.experimental import pallas as pl
from jax.experimental.pallas import tpu as pltpu

EPS = 1e-6
GATE_TAU = 16.0
CONV_W = 3
GLA_SUB = 16
HALO = 8
V7X_VMEM_LIMIT = 56 * 1024 * 1024
LANES = 128
ROW_TILE = 2048
ROW_TILE_RESIDENT = 512
ROW_TILE_MERGE = 1024
ROW_TILE_WIDE_K = 256
GLA_CHUNK = 128
CONV_COLS = 256
FFN_COLS = 512
SHORT_SEQS_PER_STEP = 16
SUB_ROWS = 256
CAST_ROWS = 64
F32 = jnp.float32
BF16 = jnp.bfloat16


def _cparams(*sem):
    return pltpu.CompilerParams(dimension_semantics=sem, vmem_limit_bytes=V7X_VMEM_LIMIT)


def _row_tile(rows, target):
    best = None
    for t in range(16, min(rows, target) + 1, 16):
        if rows % t == 0:
            best = t
    return best or rows


def _rms(x, g):
    return x * lax.rsqrt(jnp.mean(x * x, axis=-1, keepdims=True) + EPS) * g


def _sigmoid(x):
    return 0.5 * jnp.tanh(0.5 * x) + 0.5


def _silu(x):
    h = 0.5 * x
    return h * jnp.tanh(h) + h


def _dot(a, b):
    return jnp.dot(a, b, preferred_element_type=F32)


def _dot_nt(a, b):
    return lax.dot_general(a, b, (((1,), (1,)), ((), ())), preferred_element_type=F32)


def _sliced(tm, matmul, epilogue):
    n = max(1, tm // SUB_ROWS)
    sizes = [SUB_ROWS] * (n - 1) + [tm - SUB_ROWS * (n - 1)]
    r0 = 0
    for rs in sizes:
        epilogue(r0, rs, matmul(r0, rs))
        r0 += rs


def _cast_into(dst_ref, row0, col0, src_ref):
    rows, cols = src_ref.shape

    def body(r, carry):
        off = pl.multiple_of(r * CAST_ROWS, CAST_ROWS)
        dst_ref[pl.ds(row0 + off, CAST_ROWS), col0:col0 + cols] = src_ref[pl.ds(off, CAST_ROWS), :].astype(BF16)
        return carry

    lax.fori_loop(0, rows // CAST_ROWS, body, 0)


def _rmsnorm_gate_kernel(x_ref, g_ref, wa_ref, wup_ref, b_ref, n_ref, lg_ref):
    n = _rms(x_ref[...], g_ref[...]).astype(BF16)
    n_ref[...] = n
    a = _dot_nt(n, wa_ref[...]).astype(BF16)
    z = _dot(a, wup_ref[...]) + b_ref[...]
    lg_ref[...] = (jnp.minimum(z, 0.0) - jnp.log(1.0 + jnp.exp(-jnp.abs(z)))) * (1.0 / GATE_TAU)


def _rmsnorm_gate(x, g, wa, wup, b, tm):
    rows, d = x.shape
    rp, dk = wup.shape
    const = lambda i: (0, 0)
    return pl.pallas_call(
        _rmsnorm_gate_kernel, grid=(rows // tm,),
        in_specs=[pl.BlockSpec((tm, d), lambda i: (i, 0)), pl.BlockSpec((1, d), const),
                  pl.BlockSpec((rp, d), const), pl.BlockSpec((rp, dk), const), pl.BlockSpec((1, dk), const)],
        out_specs=[pl.BlockSpec((tm, d), lambda i: (i, 0)), pl.BlockSpec((tm, dk), lambda i: (i, 0))],
        out_shape=[jax.ShapeDtypeStruct((rows, d), BF16), jax.ShapeDtypeStruct((rows, dk), F32)],
        compiler_params=_cparams("arbitrary"), name="rmsnorm_gate")(x, g, wa, wup, b)


class SeqLayout:
    def __init__(self, mode, seq_rows=None, n_short=None, short_len=None):
        self.mode, self.seq_rows, self.n_short, self.short_len = mode, seq_rows, n_short, short_len


def _load_halo(halo_ref, init_ref, carry_ref, i, tiles_per_seq):
    first = (i % tiles_per_seq) == 0

    @pl.when(first)
    def _():
        halo_ref[...] = init_ref[...]

    @pl.when(jnp.logical_not(first))
    def _():
        halo_ref[...] = carry_ref[...]


def _store_tail(tail, carry_ref, st_ref, i, j, tiles_per_seq):
    carry_ref[...] = tail

    @pl.when((i % tiles_per_seq) == tiles_per_seq - 1)
    def _():
        st_ref[i // tiles_per_seq, j] = tail[HALO - (CONV_W - 1):, :]


def _shift_rows(x, prev_row):
    r = pltpu.roll(x, 1, 0)
    first = lax.broadcasted_iota(jnp.int32, (HALO, 1), 0) == 0
    head = jnp.where(first, prev_row, r[:HALO])
    return jnp.concatenate([head, r[HALO:]], axis=0) if x.shape[0] > HALO else head


class _CausalConv:
    def __init__(self, w_ref, lay, halo, prev2_ref=None, prev1_ref=None):
        w = w_ref[...]
        self.w0, self.w1, self.w2 = w[0:1, :], w[1:2, :], w[2:3, :]
        self.lay, self.tail, self.prev2_ref, self.prev1_ref = lay, halo, prev2_ref, prev1_ref

    def __call__(self, u, r0):
        rs, c = u.shape
        lay, w0, w1 = self.lay, self.w0, self.w1
        um2, um1 = self.tail[HALO - 2:HALO - 1, :], self.tail[HALO - 1:HALO, :]
        s0 = _shift_rows(w0 * u, w0 * um1)
        s1_first = w1 * um1 + w0 * um2
        if lay.mode == "table":
            take = max(0, min(lay.n_short, r0 + rs) - r0)

            def table_rows(ref):
                parts = ([_lane_chunks_load(ref, r0, take)] if take else []) + (
                    [jnp.zeros((rs - take, c), F32)] if take < rs else [])
                return parts[0] if len(parts) == 1 else jnp.concatenate(parts, axis=0)

            p2, p1 = table_rows(self.prev2_ref), table_rows(self.prev1_ref)
            t = r0 + lax.broadcasted_iota(jnp.int32, (rs, 1), 0)
            starts = jnp.where(t < lay.n_short, t % lay.short_len, t - lay.n_short) == 0
            s0 = jnp.where(starts, w0 * p1, s0)
        a = w1 * u + s0
        s1 = _shift_rows(a, s1_first)
        if lay.mode == "table":
            s1 = jnp.where(starts, w1 * p1 + w0 * p2, s1)
        self.tail = u[rs - HALO:, :]
        return self.w2 * u + s1


def _conv_specs(lay, rows, tm, c, nj, init, prev2, prev1):
    col = lambda j, i: (0, j)
    if lay.mode == "carry":
        nseq = rows // lay.seq_rows
        return ((init,), [pl.BlockSpec((HALO, c), col)],
                [pl.BlockSpec((nseq, nj, CONV_W - 1, c), lambda j, i: (0, 0, 0, 0))],
                [jax.ShapeDtypeStruct((nseq, nj, CONV_W - 1, c), F32)],
                [pltpu.VMEM((HALO, c), F32)] * 2)
    assert tm == rows and lay.short_len >= CONV_W - 1
    nseq = lay.n_short // lay.short_len
    return ((prev2, prev1), [pl.BlockSpec((nseq, c), col)] * 2,
            [pl.BlockSpec((nseq, c), col)] * 2 + [pl.BlockSpec((HALO, c), col)],
            [jax.ShapeDtypeStruct((nseq, nj * c), F32)] * 2 + [jax.ShapeDtypeStruct((HALO, nj * c), F32)],
            [pltpu.VMEM((c // LANES, lay.n_short, LANES), F32)] * 2 + [pltpu.VMEM((c // LANES, tm, LANES), F32)])


def _lane_chunks_load(ref, r0, rows):
    return jnp.concatenate([ref[k, r0:r0 + rows, :] for k in range(ref.shape[0])], axis=1)


def _lane_chunks_store(ref, r0, x):
    for k in range(ref.shape[0]):
        ref[k, r0:r0 + x.shape[0], :] = x[:, k * LANES:(k + 1) * LANES]


def _fill_tables(lay, tab2_ref, tab1_ref, prev2_ref, prev1_ref):
    nseq = lay.n_short // lay.short_len
    first_rows = pl.ds(0, nseq, stride=lay.short_len)
    for tab_ref, prev_ref in ((tab2_ref, prev2_ref), (tab1_ref, prev1_ref)):
        tab_ref[...] = jnp.zeros(tab_ref.shape, F32)
        for k in range(tab_ref.shape[0]):
            tab_ref[k, first_rows, :] = prev_ref[:, k * LANES:(k + 1) * LANES]


def _store_short_states(lay, rows_ref, tail, last2_ref, last1_ref, tail_ref):
    nseq = lay.n_short // lay.short_len
    for k in range(rows_ref.shape[0]):
        cols = slice(k * LANES, (k + 1) * LANES)
        last2_ref[:, cols] = rows_ref[k, pl.ds(lay.short_len - 2, nseq, stride=lay.short_len), :]
        last1_ref[:, cols] = rows_ref[k, pl.ds(lay.short_len - 1, nseq, stride=lay.short_len), :]
    tail_ref[...] = tail


def _inproj_conv_kernel(lay, tm, cw, tiles_per_seq, *refs):
    n_ref, wb_ref, wc_ref, wh_ref, cwt_ref = refs[:5]
    if lay.mode == "carry":
        init_ref, ca_ref, st_ref, wbf_ref, halo_ref, carry_ref = refs[5:]
    else:
        prev2_ref, prev1_ref, ca_ref, last2_ref, last1_ref, tail_ref, wbf_ref, tab2_ref, tab1_ref, u_ref = refs[5:]
    j, i = pl.program_id(0), pl.program_id(1)

    @pl.when(i == 0)
    def _():
        for part, w_ref in enumerate((wb_ref, wc_ref, wh_ref)):
            _cast_into(wbf_ref, part * cw, 0, w_ref)

    if lay.mode == "carry":
        _load_halo(halo_ref, init_ref, carry_ref, i, tiles_per_seq)
        conv = _CausalConv(cwt_ref, lay, halo_ref[...])
    else:
        _fill_tables(lay, tab2_ref, tab1_ref, prev2_ref, prev1_ref)
        conv = _CausalConv(cwt_ref, lay, jnp.zeros((HALO, cw), F32), tab2_ref, tab1_ref)

    def matmul(r0, rs):
        return _dot_nt(n_ref[r0:r0 + rs, :], wbf_ref[...])

    def epilogue(r0, rs, p):
        u = p[:, cw:2 * cw] * p[:, 2 * cw:]
        if lay.mode == "table":
            _lane_chunks_store(u_ref, r0, u)
        ca_ref[r0:r0 + rs, :] = (p[:, :cw] * conv(u, r0)).astype(BF16)

    _sliced(tm, matmul, epilogue)
    if lay.mode == "carry":
        _store_tail(conv.tail, carry_ref, st_ref, i, j, tiles_per_seq)
    else:
        _store_short_states(lay, u_ref, conv.tail, last2_ref, last1_ref, tail_ref)


def _inproj_conv(n, w_in_t, conv_w, lay, tm, cw, init=None, prev2=None, prev1=None):
    rows, d = n.shape
    dc = conv_w.shape[1]
    nj = dc // cw
    tiles_per_seq = (lay.seq_rows // tm) if lay.mode == "carry" else 1
    xargs, xspecs, xout_specs, xout_shapes, xscratch = _conv_specs(lay, rows, tm, cw, nj, init, prev2, prev1)
    wspec = lambda part: pl.BlockSpec((None, cw, d), lambda j, i: (0, part * nj + j, 0))
    return pl.pallas_call(
        functools.partial(_inproj_conv_kernel, lay, tm, cw, tiles_per_seq),
        grid=(nj, rows // tm),
        in_specs=[pl.BlockSpec((tm, d), lambda j, i: (i, 0)), wspec(0), wspec(1), wspec(2),
                  pl.BlockSpec((CONV_W, cw), lambda j, i: (0, j))] + xspecs,
        out_specs=[pl.BlockSpec((tm, cw), lambda j, i: (i, j))] + xout_specs,
        out_shape=[jax.ShapeDtypeStruct((rows, dc), BF16)] + xout_shapes,
        scratch_shapes=[pltpu.VMEM((3 * cw, d), BF16)] + xscratch,
        compiler_params=_cparams("arbitrary", "arbitrary"),
        name="inproj_conv_" + lay.mode)(n, w_in_t, w_in_t, w_in_t, conv_w, *xargs)


def _inproj_act_kernel(tm, n_lin, n_silu, q_scale, n_ref, w_ref, o_ref, wbf_ref):
    j, i = pl.program_id(0), pl.program_id(1)

    @pl.when(i == 0)
    def _():
        _cast_into(wbf_ref, 0, 0, w_ref.at[0] if len(w_ref.shape) == 3 else w_ref)

    def matmul(r0, rs):
        return _dot_nt(n_ref[r0:r0 + rs, :], wbf_ref[...])

    def epilogue(r0, rs, p):
        if n_lin:
            act = p * jnp.where(j == 0, q_scale, 1.0)
            if n_silu:
                act = jnp.where(j >= n_lin, _silu(p), act)
        else:
            act = jnp.where(j < n_silu, _silu(p), _sigmoid(p)) if n_silu else _sigmoid(p)
        o_ref[r0:r0 + rs, :] = act.astype(BF16)

    _sliced(tm, matmul, epilogue)


def _inproj_act(n, w_t, w_spec, tiles, tm, tn, q_scale):
    rows, d = n.shape
    n_lin, n_silu, n_sig = tiles
    assert not (n_lin and n_sig)
    n_col = n_lin + n_silu + n_sig
    return pl.pallas_call(
        functools.partial(_inproj_act_kernel, tm, n_lin, n_silu, q_scale),
        grid=(n_col, rows // tm),
        in_specs=[pl.BlockSpec((tm, d), lambda j, i: (i, 0)), w_spec],
        out_specs=pl.BlockSpec((tm, tn), lambda j, i: (i, j)),
        out_shape=jax.ShapeDtypeStruct((rows, n_col * tn), BF16),
        scratch_shapes=[pltpu.VMEM((tn, d), BF16)],
        compiler_params=_cparams("arbitrary", "arbitrary"), name="inproj_act")(n, w_t)


def _cumsum_groups(x, sub):
    rows = x.shape[0]
    pos = lax.broadcasted_iota(jnp.int32, (rows, 1), 0) % sub
    s = 1
    while s < sub:
        x = x + jnp.where(pos >= s, pltpu.roll(x, s, 0), 0.0)
        s *= 2
    return x


def _gla_chunk(q, k, v, lg, s, sub):
    c, dk = q.shape
    ngrp = c // sub
    gpu = 2 if ngrp % 2 == 0 else 1
    unit = gpu * sub
    bt = _cumsum_groups(lg, sub)
    r = jnp.zeros((1, dk), F32)
    r_grp, b_rows = [], []
    for g in range(ngrp):
        r_grp.append(r)
        b_rows.append(bt[g * sub:(g + 1) * sub, :] + r)
        r = r + bt[(g + 1) * sub - 1:(g + 1) * sub, :]
    b = jnp.concatenate(b_rows, axis=0) if ngrp > 1 else b_rows[0]
    b_last = r
    qe = (q * jnp.exp(b)).astype(BF16)
    kd = (k * jnp.exp(b_last - b)).astype(BF16)
    att_rows = []
    for i in range(c // unit):
        ref = r_grp[i * gpu + gpu - 1]
        rows = slice(i * unit, (i + 1) * unit)
        seen = (i + 1) * unit
        qt = (q[rows] * jnp.exp(b[rows] - ref)).astype(BF16)
        ke = (k[:seen] * jnp.exp(ref - b[:seen])).astype(BF16)
        if seen < c:
            ke = jnp.concatenate([ke, jnp.zeros((c - seen, dk), BF16)], axis=0)
        a = _dot_nt(qt, ke)
        col = lax.broadcasted_iota(jnp.int32, (unit, c), 1)
        rloc = lax.broadcasted_iota(jnp.int32, (unit, c), 0)
        att_rows.append(jnp.where(col <= rloc + i * unit, a, 0.0))
    att = (jnp.concatenate(att_rows, axis=0) if len(att_rows) > 1 else att_rows[0]).astype(BF16)
    o = _dot(att, v) + _dot(qe, s.astype(BF16))
    upd = lax.dot_general(kd, v, (((0,), (0,)), ((), ())), preferred_element_type=F32)
    dl = jnp.exp(b_last)
    dl_col = jnp.transpose(jnp.broadcast_to(dl, (LANES, dk)))
    dv = s.shape[1]
    s_dec = jnp.concatenate([s[:, n * LANES:(n + 1) * LANES] * dl_col for n in range(dv // LANES)], axis=1)
    return o, s_dec + upd


def _gla_out(o, gn, gs):
    return (_rms(o, gn) * gs.astype(F32)).astype(BF16)


def _gla_seq_kernel(sub, heads, dk, dv, q_ref, k_ref, v_ref, lg_ref, gs_ref, gn_ref, s0_ref, ob_ref, s_ref):
    @pl.when(pl.program_id(1) == 0)
    def _():
        def copy_head(h, carry):
            s_ref[0, h] = s0_ref[0, h]
            return carry

        lax.fori_loop(0, heads, copy_head, 0)

    for h in range(heads):
        ck, cv = slice(h * dk, (h + 1) * dk), slice(h * dv, (h + 1) * dv)
        o, s_new = _gla_chunk(q_ref[:, ck].astype(F32), k_ref[:, ck].astype(F32), v_ref[:, cv], lg_ref[:, ck],
                              s_ref[0, h], sub)
        s_ref[0, h] = s_new
        ob_ref[:, cv] = _gla_out(o, gn_ref[...], gs_ref[:, cv])


def _gla_seq(lin, logg, gn, s0, heads, dk, dv, nseq, seq_rows, chunk, sub, row0, shared_init):
    nchunk = seq_rows // chunk
    blk0 = row0 // chunk
    rb = lambda b, c: blk0 + b * nchunk + c
    wk, wv = heads * dk, heads * dv
    assert (2 * wk) % wv == 0
    return pl.pallas_call(
        functools.partial(_gla_seq_kernel, sub, heads, dk, dv),
        grid=(nseq, nchunk),
        in_specs=[pl.BlockSpec((chunk, wk), lambda b, c: (rb(b, c), 0)),
                  pl.BlockSpec((chunk, wk), lambda b, c: (rb(b, c), 1)),
                  pl.BlockSpec((chunk, wv), lambda b, c: (rb(b, c), 2 * wk // wv)),
                  pl.BlockSpec((chunk, wk), lambda b, c: (rb(b, c), 0)),
                  pl.BlockSpec((chunk, wv), lambda b, c: (rb(b, c), 2 * wk // wv + 1)),
                  pl.BlockSpec((1, dv), lambda b, c: (0, 0)),
                  pl.BlockSpec((1, heads, dk, dv), lambda b, c: (0 if shared_init else b, 0, 0, 0))],
        out_specs=[pl.BlockSpec((chunk, wv), lambda b, c: (b * nchunk + c, 0)),
                   pl.BlockSpec((1, heads, dk, dv), lambda b, c: (b, 0, 0, 0))],
        out_shape=[jax.ShapeDtypeStruct((nseq * seq_rows, wv), BF16),
                   jax.ShapeDtypeStruct((nseq, heads, dk, dv), F32)],
        compiler_params=_cparams("arbitrary", "arbitrary"),
        name="gla_seq")(lin, lin, lin, logg, lin, gn, s0)


def _gla_short_kernel(nb, sl, q_ref, k_ref, v_ref, lg_ref, gs_ref, gn_ref, s0_ref, ob_ref, s_ref):
    q = q_ref[...].astype(F32)
    k = k_ref[...].astype(F32)
    v = v_ref[...].astype(F32)
    lg = lg_ref[...]
    outs = []
    for n in range(nb):
        rs = slice(n * sl, (n + 1) * sl)
        o, s_new = _gla_chunk(q[rs], k[rs], v[rs].astype(BF16), lg[rs], s0_ref[n, 0], sl)
        s_ref[n, 0] = s_new
        outs.append(o)
    o = jnp.concatenate(outs, axis=0)
    ob_ref[...] = _gla_out(o, gn_ref[...], gs_ref[...])


def _gla_short(lin, logg, gn, s0, heads, dk, dv, nseq, sl, nb):
    rows = nb * sl
    kq = (heads * dk) // dk
    kv = (2 * heads * dk) // dv
    return pl.pallas_call(
        functools.partial(_gla_short_kernel, nb, sl),
        grid=(nseq // nb, heads),
        in_specs=[pl.BlockSpec((rows, dk), lambda b, h: (b, h)),
                  pl.BlockSpec((rows, dk), lambda b, h: (b, kq + h)),
                  pl.BlockSpec((rows, dv), lambda b, h: (b, kv + h)),
                  pl.BlockSpec((rows, dk), lambda b, h: (b, h)),
                  pl.BlockSpec((rows, dv), lambda b, h: (b, kv + heads + h)),
                  pl.BlockSpec((1, dv), lambda b, h: (0, 0)),
                  pl.BlockSpec((nb, 1, dk, dv), lambda b, h: (b, h, 0, 0))],
        out_specs=[pl.BlockSpec((rows, dv), lambda b, h: (b, h)),
                   pl.BlockSpec((nb, 1, dk, dv), lambda b, h: (b, h, 0, 0))],
        out_shape=[jax.ShapeDtypeStruct((nseq * sl, heads * dv), BF16),
                   jax.ShapeDtypeStruct((nseq, heads, dk, dv), F32)],
        compiler_params=_cparams("arbitrary", "arbitrary"),
        name="gla_short")(lin, lin, lin, logg, lin, gn, s0)


def _merge_kernel(ca_ref, ob_ref, ga_ref, gb_ref, wc_ref, wg_ref, o_ref):
    ya = _dot(ca_ref[...], wc_ref[...])
    yb = _dot(ob_ref[...], wg_ref[...])
    o_ref[...] = (ga_ref[...].astype(F32) * ya + gb_ref[...].astype(F32) * yb).astype(BF16)


def _merge(ca, ob, sig, wc, wg, tm, tn):
    rows, dc = ca.shape
    dg, d = wg.shape
    nn = d // tn
    return pl.pallas_call(
        _merge_kernel, grid=(rows // tm, nn),
        in_specs=[pl.BlockSpec((tm, dc), lambda i, j: (i, 0)), pl.BlockSpec((tm, dg), lambda i, j: (i, 0)),
                  pl.BlockSpec((tm, tn), lambda i, j: (i, j)), pl.BlockSpec((tm, tn), lambda i, j: (i, nn + j)),
                  pl.BlockSpec((dc, tn), lambda i, j: (0, j)), pl.BlockSpec((dg, tn), lambda i, j: (0, j))],
        out_specs=pl.BlockSpec((tm, tn), lambda i, j: (i, j)),
        out_shape=jax.ShapeDtypeStruct((rows, d), BF16),
        compiler_params=_cparams("arbitrary", "arbitrary"), name="merge")(ca, ob, sig, sig, wc, wg)


def _oproj_kernel(m_ref, x_ref, w_ref, g_ref, h_ref, n2_ref):
    h = x_ref[...] + _dot(m_ref[...], w_ref[...])
    h_ref[...] = h
    n2_ref[...] = _rms(h, g_ref[...]).astype(BF16)


def _oproj(m, x, w, g, tm):
    rows, d = x.shape
    row = lambda i: (i, 0)
    return pl.pallas_call(
        _oproj_kernel, grid=(rows // tm,),
        in_specs=[pl.BlockSpec((tm, d), row), pl.BlockSpec((tm, d), row),
                  pl.BlockSpec((d, d), lambda i: (0, 0)), pl.BlockSpec((1, d), lambda i: (0, 0))],
        out_specs=[pl.BlockSpec((tm, d), row), pl.BlockSpec((tm, d), row)],
        out_shape=[jax.ShapeDtypeStruct((rows, d), F32), jax.ShapeDtypeStruct((rows, d), BF16)],
        compiler_params=_cparams("arbitrary"), name="oproj")(m, x, w, g)


def _ffn_up_kernel(lay, tm, fw, tiles_per_seq, *refs):
    n_ref, wa_ref, wg_ref, cwt_ref, b_ref = refs[:5]
    if lay.mode == "carry":
        init_ref, act_ref, st_ref, wbf_ref, halo_ref, carry_ref = refs[5:]
    else:
        prev2_ref, prev1_ref, act_ref, last2_ref, last1_ref, tail_ref, wbf_ref, tab2_ref, tab1_ref, gt_ref = refs[5:]
    j, i = pl.program_id(0), pl.program_id(1)

    @pl.when(i == 0)
    def _():
        _cast_into(wbf_ref, 0, 0, wa_ref)
        _cast_into(wbf_ref, 0, fw, wg_ref)

    if lay.mode == "carry":
        _load_halo(halo_ref, init_ref, carry_ref, i, tiles_per_seq)
        conv = _CausalConv(cwt_ref, lay, halo_ref[...])
    else:
        _fill_tables(lay, tab2_ref, tab1_ref, prev2_ref, prev1_ref)
        conv = _CausalConv(cwt_ref, lay, jnp.zeros((HALO, fw), F32), tab2_ref, tab1_ref)

    def matmul(r0, rs):
        return _dot(n_ref[r0:r0 + rs, :], wbf_ref[...])

    def epilogue(r0, rs, p):
        gt = p[:, fw:]
        if lay.mode == "table":
            _lane_chunks_store(gt_ref, r0, gt)
        z = conv(gt, r0) + b_ref[...]
        act_ref[r0:r0 + rs, :] = (_silu(z) * p[:, :fw]).astype(BF16)

    _sliced(tm, matmul, epilogue)
    if lay.mode == "carry":
        _store_tail(conv.tail, carry_ref, st_ref, i, j, tiles_per_seq)
    else:
        _store_short_states(lay, gt_ref, conv.tail, last2_ref, last1_ref, tail_ref)


def _ffn_up(n2, w_up, conv_w, bias, lay, tm, fw, init=None, prev2=None, prev1=None):
    rows, d = n2.shape
    dff = conv_w.shape[1]
    nj = dff // fw
    tiles_per_seq = (lay.seq_rows // tm) if lay.mode == "carry" else 1
    xargs, xspecs, xout_specs, xout_shapes, xscratch = _conv_specs(lay, rows, tm, fw, nj, init, prev2, prev1)
    wspec = lambda part: pl.BlockSpec((None, d, fw), lambda j, i: (0, 0, part * nj + j))
    return pl.pallas_call(
        functools.partial(_ffn_up_kernel, lay, tm, fw, tiles_per_seq),
        grid=(nj, rows // tm),
        in_specs=[pl.BlockSpec((tm, d), lambda j, i: (i, 0)), wspec(0), wspec(1),
                  pl.BlockSpec((CONV_W, fw), lambda j, i: (0, j)), pl.BlockSpec((1, fw), lambda j, i: (0, j))]
                 + xspecs,
        out_specs=[pl.BlockSpec((tm, fw), lambda j, i: (i, j))] + xout_specs,
        out_shape=[jax.ShapeDtypeStruct((rows, dff), BF16)] + xout_shapes,
        scratch_shapes=[pltpu.VMEM((d, 2 * fw), BF16)] + xscratch,
        compiler_params=_cparams("arbitrary", "arbitrary"),
        name="ffn_up_" + lay.mode)(n2, w_up, w_up, conv_w, bias, *xargs)


def _ffn_down_kernel(act_ref, w_ref, h_ref, g_ref, y_ref):
    y_ref[...] = _rms(h_ref[...] + _dot(act_ref[...], w_ref[...]), g_ref[...])


def _ffn_down(act, w, h, g, rows, tm):
    dff = act.shape[1]
    d = w.shape[1]
    row = lambda i: (i, 0)
    return pl.pallas_call(
        _ffn_down_kernel, grid=(rows // tm,),
        in_specs=[pl.BlockSpec((tm, dff), row),
                  pl.BlockSpec((dff, d), lambda i: (0, 0), pipeline_mode=pl.Buffered(1)),
                  pl.BlockSpec((tm, d), row), pl.BlockSpec((1, d), lambda i: (0, 0))],
        out_specs=pl.BlockSpec((tm, d), row),
        out_shape=jax.ShapeDtypeStruct((rows, d), F32),
        compiler_params=_cparams("arbitrary"), name="ffn_down")(act, w, h, g)


def kernel(x_prompt, x_sample, state_conv, state_gla, state_ffn_conv, meta_tokens, norm_mix_g, w_in, conv_mix_w, w_conv_out, w_gate_up, b_gate, gla_norm_g, w_gla_out, w_o, norm_ffn_g, w_ffn_up, ffn_conv_w, ffn_conv_b, w_ffn_down, final_norm_g):
    bp, seq, d = x_prompt.shape
    bs, sl, _ = x_sample.shape
    assert w_in.shape[0] == 1, "single-layer step"
    n_meta = meta_tokens.shape[0]
    dc = state_conv.shape[-1]
    _, _, heads, dk, dv = state_gla.shape
    dff = state_ffn_conv.shape[-1]
    rank = w_gate_up.shape[1]
    assert n_meta % GLA_SUB == 0 and seq % GLA_SUB == 0 and GLA_SUB % sl == 0 and sl >= CONV_W - 1

    o_q = 3 * dc
    o_a = o_q + 2 * heads * dk + 2 * heads * dv
    o_ga = o_a + rank
    tn = heads * dk
    assert o_q % tn == 0 and (heads * dv) % tn == 0 and d % tn == 0
    cw, fw = CONV_COLS, FFN_COLS
    tiles_qkvg = ((2 * heads * dk + heads * dv) // tn, heads * dv // tn, 0)
    tiles_gates = (0, 0, 2 * d // tn)
    spec_qkvg = pl.BlockSpec((None, tn, d), lambda j, i: (0, o_q // tn + j, 0))
    spec_gates = pl.BlockSpec((pl.Element(1), pl.Element(tn), pl.Element(d)), lambda j, i: (0, (o_ga // 8 + j * (tn // 8)) * 8, 0))
    assert o_ga % 8 == 0
    q_scale = float(dk) ** -0.5

    w_in_t = jnp.swapaxes(w_in, 1, 2)
    wa = jnp.pad(w_in_t[0, o_a:o_ga], ((0, LANES - rank), (0, 0))).astype(BF16)
    wup = jnp.pad(w_gate_up[0], ((0, LANES - rank), (0, 0))).astype(BF16)
    wc = w_conv_out[0].astype(BF16)
    wg = w_gla_out[0].astype(BF16)
    wo = w_o[0].astype(BF16)
    wd = w_ffn_down[0].astype(BF16)
    g1, g2, gf, gn = norm_mix_g[0][None], norm_ffn_g[0][None], final_norm_g[None], gla_norm_g[0][None]
    bg, fb, cmw, fcw = b_gate[0][None], ffn_conv_b[0][None], conv_mix_w[0], ffn_conv_w[0]

    def in_projections(n, tm, lay, **conv_kw):
        conv_outs = _inproj_conv(n, w_in_t, cmw, lay, tm, cw, **conv_kw)
        lin = _inproj_act(n, w_in_t, spec_qkvg, tiles_qkvg, tm, tn, q_scale)
        sig = _inproj_act(n, w_in_t, spec_gates, tiles_gates, tm, tn, 1.0)
        return conv_outs, lin, sig

    n_short = bs * sl
    rows_s = n_short + n_meta
    xs = jnp.concatenate([x_sample.reshape(n_short, d), meta_tokens.astype(x_sample.dtype)], axis=0)
    lay_s = SeqLayout("table", n_short=n_short, short_len=sl)
    tm_s = rows_s
    n_s, logg_s = _rmsnorm_gate(xs, g1, wa, wup, bg, _row_tile(rows_s, ROW_TILE_RESIDENT))
    (ca_s, conv_s2, conv_s1, conv_tail), lin_s, sig_s = in_projections(
        n_s, tm_s, lay_s, prev2=state_conv[0, :, 0], prev1=state_conv[0, :, 1])
    s_zero = jnp.zeros((1, heads, dk, dv), F32)
    ob_m, s_meta = _gla_seq(lin_s, logg_s, gn, s_zero, heads, dk, dv, 1, n_meta, n_meta, GLA_SUB,
                            n_short, True)
    ob_smp, s_smp = _gla_short(lin_s, logg_s, gn, state_gla[0], heads, dk, dv, bs, sl,
                               SHORT_SEQS_PER_STEP)
    ob_s = jnp.concatenate([ob_smp, ob_m], axis=0)
    m_s = _merge(ca_s, ob_s, sig_s, wc, wg, tm_s, tn)
    h_s, n2_s = _oproj(m_s, xs, wo, g2, _row_tile(rows_s, ROW_TILE_RESIDENT))
    act_s, ffn_s2, ffn_s1, ffn_tail = _ffn_up(n2_s, w_ffn_up, fcw, fb, lay_s, tm_s, fw,
                                              prev2=state_ffn_conv[0, :, 0], prev1=state_ffn_conv[0, :, 1])
    y_s = _ffn_down(act_s, wd, h_s, gf, n_short, _row_tile(n_short, ROW_TILE_WIDE_K))

    rows_p = bp * seq
    xp = x_prompt.reshape(rows_p, d)
    lay_p = SeqLayout("carry", seq_rows=seq)
    tm_p = _row_tile(seq, ROW_TILE)
    tm_o = _row_tile(seq, ROW_TILE_RESIDENT)
    n_p, logg_p = _rmsnorm_gate(xp, g1, wa, wup, bg, tm_o)
    (ca_p, conv_p), lin_p, sig_p = in_projections(n_p, tm_p, lay_p, init=conv_tail)
    ob_p, s_p = _gla_seq(lin_p, logg_p, gn, s_meta, heads, dk, dv, bp, seq, _row_tile(seq, GLA_CHUNK),
                         GLA_SUB, 0, True)
    m_p = _merge(ca_p, ob_p, sig_p, wc, wg, _row_tile(seq, ROW_TILE_MERGE), tn)
    h_p, n2_p = _oproj(m_p, xp, wo, g2, tm_o)
    act_p, ffn_p = _ffn_up(n2_p, w_ffn_up, fcw, fb, lay_p, tm_p, fw, init=ffn_tail)
    y_p = _ffn_down(act_p, wd, h_p, gf, rows_p, _row_tile(seq, ROW_TILE_WIDE_K))

    untile = lambda st: jnp.swapaxes(st, 1, 2).reshape(st.shape[0], CONV_W - 1, -1)
    return (y_p.reshape(bp, seq, d), y_s.reshape(bs, sl, d),
            untile(conv_p)[None], s_p[None], untile(ffn_p)[None],
            jnp.stack([conv_s2, conv_s1], axis=1)[None], s_smp[None], jnp.stack([ffn_s2, ffn_s1], axis=1)[None])
```

```python
import functools

import jax
import jax.numpy as jnp
from jax import lax
from jax.experimental import pallas as pl
from jax.experimental.pallas import tpu as pltpu

EPS = 1e-6
GATE_TAU = 16.0
CONV_W = 3
GLA_SUB = 16
HALO = 8
V7X_VMEM_LIMIT = 56 * 1024 * 1024
LANES = 128
ROW_TILE = 2048
ROW_TILE_RESIDENT = 512
ROW_TILE_MERGE = 1024
ROW_TILE_WIDE_K = 256
GLA_CHUNK = 128
CONV_COLS = 256
FFN_COLS = 512
SHORT_SEQS_PER_STEP = 8
SUB_ROWS = 256
CAST_ROWS = 64
F32 = jnp.float32
BF16 = jnp.bfloat16


def _cparams(*sem):
    return pltpu.CompilerParams(dimension_semantics=sem, vmem_limit_bytes=V7X_VMEM_LIMIT)


def _row_tile(rows, target):
    best = None
    for t in range(16, min(rows, target) + 1, 16):
        if rows % t == 0:
            best = t
    return best or rows


def _rms(x, g):
    return x * lax.rsqrt(jnp.mean(x * x, axis=-1, keepdims=True) + EPS) * g


def _sigmoid(x):
    return 0.5 * jnp.tanh(0.5 * x) + 0.5


def _silu(x):
    h = 0.5 * x
    return h * jnp.tanh(h) + h


def _dot(a, b):
    return jnp.dot(a, b, preferred_element_type=F32)


def _dot_nt(a, b):
    return lax.dot_general(a, b, (((1,), (1,)), ((), ())), preferred_element_type=F32)


def _sliced(tm, matmul, epilogue):
    n = max(1, tm // SUB_ROWS)
    sizes = [SUB_ROWS] * (n - 1) + [tm - SUB_ROWS * (n - 1)]
    r0 = 0
    for rs in sizes:
        epilogue(r0, rs, matmul(r0, rs))
        r0 += rs


def _cast_into(dst_ref, row0, col0, src_ref):
    rows, cols = src_ref.shape

    def body(r, carry):
        off = pl.multiple_of(r * CAST_ROWS, CAST_ROWS)
        dst_ref[pl.ds(row0 + off, CAST_ROWS), col0:col0 + cols] = src_ref[pl.ds(off, CAST_ROWS), :].astype(BF16)
        return carry

    lax.fori_loop(0, rows // CAST_ROWS, body, 0)


def _rmsnorm_gate_kernel(x_ref, g_ref, wa_ref, wup_ref, b_ref, n_ref, lg_ref):
    n = _rms(x_ref[...], g_ref[...]).astype(BF16)
    n_ref[...] = n
    a = _dot_nt(n, wa_ref[...]).astype(BF16)
    z = _dot(a, wup_ref[...]) + b_ref[...]
    lg_ref[...] = (jnp.minimum(z, 0.0) - jnp.log(1.0 + jnp.exp(-jnp.abs(z)))) * (1.0 / GATE_TAU)


def _rmsnorm_gate(x, g, wa, wup, b, tm):
    rows, d = x.shape
    rp, dk = wup.shape
    const = lambda i: (0, 0)
    return pl.pallas_call(
        _rmsnorm_gate_kernel, grid=(rows // tm,),
        in_specs=[pl.BlockSpec((tm, d), lambda i: (i, 0)), pl.BlockSpec((1, d), const),
                  pl.BlockSpec((rp, d), const), pl.BlockSpec((rp, dk), const), pl.BlockSpec((1, dk), const)],
        out_specs=[pl.BlockSpec((tm, d), lambda i: (i, 0)), pl.BlockSpec((tm, dk), lambda i: (i, 0))],
        out_shape=[jax.ShapeDtypeStruct((rows, d), BF16), jax.ShapeDtypeStruct((rows, dk), F32)],
        compiler_params=_cparams("arbitrary"), name="rmsnorm_gate")(x, g, wa, wup, b)


class SeqLayout:
    def __init__(self, mode, seq_rows=None, n_short=None, short_len=None):
        self.mode, self.seq_rows, self.n_short, self.short_len = mode, seq_rows, n_short, short_len


def _load_halo(halo_ref, init_ref, carry_ref, i, tiles_per_seq):
    first = (i % tiles_per_seq) == 0

    @pl.when(first)
    def _():
        halo_ref[...] = init_ref[...]

    @pl.when(jnp.logical_not(first))
    def _():
        halo_ref[...] = carry_ref[...]


def _store_tail(tail, carry_ref, st_ref, i, j, tiles_per_seq):
    carry_ref[...] = tail

    @pl.when((i % tiles_per_seq) == tiles_per_seq - 1)
    def _():
        st_ref[i // tiles_per_seq, j] = tail[HALO - (CONV_W - 1):, :]


def _shift_rows(x, prev_row):
    r = pltpu.roll(x, 1, 0)
    first = lax.broadcasted_iota(jnp.int32, (HALO, 1), 0) == 0
    head = jnp.where(first, prev_row, r[:HALO])
    return jnp.concatenate([head, r[HALO:]], axis=0) if x.shape[0] > HALO else head


class _CausalConv:
    def __init__(self, w_ref, lay, halo, prev2_ref=None, prev1_ref=None):
        w = w_ref[...]
        self.w0, self.w1, self.w2 = w[0:1, :], w[1:2, :], w[2:3, :]
        self.lay, self.tail, self.prev2_ref, self.prev1_ref = lay, halo, prev2_ref, prev1_ref

    def __call__(self, u, r0):
        rs, c = u.shape
        lay, w0, w1 = self.lay, self.w0, self.w1
        um2, um1 = self.tail[HALO - 2:HALO - 1, :], self.tail[HALO - 1:HALO, :]
        s0 = _shift_rows(w0 * u, w0 * um1)
        s1_first = w1 * um1 + w0 * um2
        if lay.mode == "table":
            take = max(0, min(lay.n_short, r0 + rs) - r0)

            def table_rows(ref):
                parts = ([_lane_chunks_load(ref, r0, take)] if take else []) + (
                    [jnp.zeros((rs - take, c), F32)] if take < rs else [])
                return parts[0] if len(parts) == 1 else jnp.concatenate(parts, axis=0)

            p2, p1 = table_rows(self.prev2_ref), table_rows(self.prev1_ref)
            t = r0 + lax.broadcasted_iota(jnp.int32, (rs, 1), 0)
            starts = jnp.where(t < lay.n_short, t % lay.short_len, t - lay.n_short) == 0
            s0 = jnp.where(starts, w0 * p1, s0)
        a = w1 * u + s0
        s1 = _shift_rows(a, s1_first)
        if lay.mode == "table":
            s1 = jnp.where(starts, w1 * p1 + w0 * p2, s1)
        self.tail = u[rs - HALO:, :]
        return self.w2 * u + s1


def _conv_specs(lay, rows, tm, c, nj, init, prev2, prev1):
    col = lambda j, i: (0, j)
    if lay.mode == "carry":
        nseq = rows // lay.seq_rows
        return ((init,), [pl.BlockSpec((HALO, c), col)],
                [pl.BlockSpec((nseq, nj, CONV_W - 1, c), lambda j, i: (0, 0, 0, 0))],
                [jax.ShapeDtypeStruct((nseq, nj, CONV_W - 1, c), F32)],
                [pltpu.VMEM((HALO, c), F32)] * 2)
    assert tm == rows and lay.short_len >= CONV_W - 1
    nseq = lay.n_short // lay.short_len
    return ((prev2, prev1), [pl.BlockSpec((nseq, c), col)] * 2,
            [pl.BlockSpec((nseq, c), col)] * 2 + [pl.BlockSpec((HALO, c), col)],
            [jax.ShapeDtypeStruct((nseq, nj * c), F32)] * 2 + [jax.ShapeDtypeStruct((HALO, nj * c), F32)],
            [pltpu.VMEM((c // LANES, lay.n_short, LANES), F32)] * 2 + [pltpu.VMEM((c // LANES, tm, LANES), F32)])


def _lane_chunks_load(ref, r0, rows):
    return jnp.concatenate([ref[k, r0:r0 + rows, :] for k in range(ref.shape[0])], axis=1)


def _lane_chunks_store(ref, r0, x):
    for k in range(ref.shape[0]):
        ref[k, r0:r0 + x.shape[0], :] = x[:, k * LANES:(k + 1) * LANES]


def _fill_tables(lay, tab2_ref, tab1_ref, prev2_ref, prev1_ref):
    nseq = lay.n_short // lay.short_len
    first_rows = pl.ds(0, nseq, stride=lay.short_len)
    for tab_ref, prev_ref in ((tab2_ref, prev2_ref), (tab1_ref, prev1_ref)):
        tab_ref[...] = jnp.zeros(tab_ref.shape, F32)
        for k in range(tab_ref.shape[0]):
            tab_ref[k, first_rows, :] = prev_ref[:, k * LANES:(k + 1) * LANES]


def _store_short_states(lay, rows_ref, tail, last2_ref, last1_ref, tail_ref):
    nseq = lay.n_short // lay.short_len
    for k in range(rows_ref.shape[0]):
        cols = slice(k * LANES, (k + 1) * LANES)
        last2_ref[:, cols] = rows_ref[k, pl.ds(lay.short_len - 2, nseq, stride=lay.short_len), :]
        last1_ref[:, cols] = rows_ref[k, pl.ds(lay.short_len - 1, nseq, stride=lay.short_len), :]
    tail_ref[...] = tail


def _inproj_conv_kernel(lay, tm, cw, tiles_per_seq, *refs):
    n_ref, wb_ref, wc_ref, wh_ref, cwt_ref = refs[:5]
    if lay.mode == "carry":
        init_ref, ca_ref, st_ref, wbf_ref, halo_ref, carry_ref = refs[5:]
    else:
        prev2_ref, prev1_ref, ca_ref, last2_ref, last1_ref, tail_ref, wbf_ref, tab2_ref, tab1_ref, u_ref = refs[5:]
    j, i = pl.program_id(0), pl.program_id(1)

    @pl.when(i == 0)
    def _():
        for part, w_ref in enumerate((wb_ref, wc_ref, wh_ref)):
            _cast_into(wbf_ref, part * cw, 0, w_ref)

    if lay.mode == "carry":
        _load_halo(halo_ref, init_ref, carry_ref, i, tiles_per_seq)
        conv = _CausalConv(cwt_ref, lay, halo_ref[...])
    else:
        _fill_tables(lay, tab2_ref, tab1_ref, prev2_ref, prev1_ref)
        conv = _CausalConv(cwt_ref, lay, jnp.zeros((HALO, cw), F32), tab2_ref, tab1_ref)

    def matmul(r0, rs):
        return _dot_nt(n_ref[r0:r0 + rs, :], wbf_ref[...])

    def epilogue(r0, rs, p):
        u = p[:, cw:2 * cw] * p[:, 2 * cw:]
        if lay.mode == "table":
            _lane_chunks_store(u_ref, r0, u)
        ca_ref[r0:r0 + rs, :] = (p[:, :cw] * conv(u, r0)).astype(BF16)

    _sliced(tm, matmul, epilogue)
    if lay.mode == "carry":
        _store_tail(conv.tail, carry_ref, st_ref, i, j, tiles_per_seq)
    else:
        _store_short_states(lay, u_ref, conv.tail, last2_ref, last1_ref, tail_ref)


def _inproj_conv(n, w_in_t, conv_w, lay, tm, cw, init=None, prev2=None, prev1=None):
    rows, d = n.shape
    dc = conv_w.shape[1]
    nj = dc // cw
    tiles_per_seq = (lay.seq_rows // tm) if lay.mode == "carry" else 1
    xargs, xspecs, xout_specs, xout_shapes, xscratch = _conv_specs(lay, rows, tm, cw, nj, init, prev2, prev1)
    wspec = lambda part: pl.BlockSpec((None, cw, d), lambda j, i: (0, part * nj + j, 0))
    return pl.pallas_call(
        functools.partial(_inproj_conv_kernel, lay, tm, cw, tiles_per_seq),
        grid=(nj, rows // tm),
        in_specs=[pl.BlockSpec((tm, d), lambda j, i: (i, 0)), wspec(0), wspec(1), wspec(2),
                  pl.BlockSpec((CONV_W, cw), lambda j, i: (0, j))] + xspecs,
        out_specs=[pl.BlockSpec((tm, cw), lambda j, i: (i, j))] + xout_specs,
        out_shape=[jax.ShapeDtypeStruct((rows, dc), BF16)] + xout_shapes,
        scratch_shapes=[pltpu.VMEM((3 * cw, d), BF16)] + xscratch,
        compiler_params=_cparams("arbitrary", "arbitrary"),
        name="inproj_conv_" + lay.mode)(n, w_in_t, w_in_t, w_in_t, conv_w, *xargs)


def _inproj_act_kernel(tm, n_lin, n_silu, q_scale, n_ref, w_ref, o_ref, wbf_ref):
    j, i = pl.program_id(0), pl.program_id(1)

    @pl.when(i == 0)
    def _():
        _cast_into(wbf_ref, 0, 0, w_ref.at[0] if len(w_ref.shape) == 3 else w_ref)

    def matmul(r0, rs):
        return _dot_nt(n_ref[r0:r0 + rs, :], wbf_ref[...])

    def epilogue(r0, rs, p):
        if n_lin:
            act = p * jnp.where(j == 0, q_scale, 1.0)
            if n_silu:
                act = jnp.where(j >= n_lin, _silu(p), act)
        else:
            act = jnp.where(j < n_silu, _silu(p), _sigmoid(p)) if n_silu else _sigmoid(p)
        o_ref[r0:r0 + rs, :] = act.astype(BF16)

    _sliced(tm, matmul, epilogue)


def _inproj_act(n, w_t, w_spec, tiles, tm, tn, q_scale):
    rows, d = n.shape
    n_lin, n_silu, n_sig = tiles
    assert not (n_lin and n_sig)
    n_col = n_lin + n_silu + n_sig
    return pl.pallas_call(
        functools.partial(_inproj_act_kernel, tm, n_lin, n_silu, q_scale),
        grid=(n_col, rows // tm),
        in_specs=[pl.BlockSpec((tm, d), lambda j, i: (i, 0)), w_spec],
        out_specs=pl.BlockSpec((tm, tn), lambda j, i: (i, j)),
        out_shape=jax.ShapeDtypeStruct((rows, n_col * tn), BF16),
        scratch_shapes=[pltpu.VMEM((tn, d), BF16)],
        compiler_params=_cparams("arbitrary", "arbitrary"), name="inproj_act")(n, w_t)


def _cumsum_groups(x, sub):
    rows = x.shape[0]
    pos = lax.broadcasted_iota(jnp.int32, (rows, 1), 0) % sub
    s = 1
    while s < sub:
        x = x + jnp.where(pos >= s, pltpu.roll(x, s, 0), 0.0)
        s *= 2
    return x


def _gla_chunk(q, k, v, lg, s, sub):
    c, dk = q.shape
    ngrp = c // sub
    gpu = 2 if ngrp % 2 == 0 else 1
    unit = gpu * sub
    bt = _cumsum_groups(lg, sub)
    r = jnp.zeros((1, dk), F32)
    r_grp, b_rows = [], []
    for g in range(ngrp):
        r_grp.append(r)
        b_rows.append(bt[g * sub:(g + 1) * sub, :] + r)
        r = r + bt[(g + 1) * sub - 1:(g + 1) * sub, :]
    b = jnp.concatenate(b_rows, axis=0) if ngrp > 1 else b_rows[0]
    b_last = r
    qe = (q * jnp.exp(b)).astype(BF16)
    kd = (k * jnp.exp(b_last - b)).astype(BF16)
    att_rows = []
    for i in range(c // unit):
        ref = r_grp[i * gpu + gpu - 1]
        rows = slice(i * unit, (i + 1) * unit)
        seen = (i + 1) * unit
        qt = (q[rows] * jnp.exp(b[rows] - ref)).astype(BF16)
        ke = (k[:seen] * jnp.exp(ref - b[:seen])).astype(BF16)
        if seen < c:
            ke = jnp.concatenate([ke, jnp.zeros((c - seen, dk), BF16)], axis=0)
        a = _dot_nt(qt, ke)
        col = lax.broadcasted_iota(jnp.int32, (unit, c), 1)
        rloc = lax.broadcasted_iota(jnp.int32, (unit, c), 0)
        att_rows.append(jnp.where(col <= rloc + i * unit, a, 0.0))
    att = (jnp.concatenate(att_rows, axis=0) if len(att_rows) > 1 else att_rows[0]).astype(BF16)
    o = _dot(att, v) + _dot(qe, s.astype(BF16))
    upd = lax.dot_general(kd, v, (((0,), (0,)), ((), ())), preferred_element_type=F32)
    dl = jnp.exp(b_last)
    dl_col = jnp.transpose(jnp.broadcast_to(dl, (LANES, dk)))
    dv = s.shape[1]
    s_dec = jnp.concatenate([s[:, n * LANES:(n + 1) * LANES] * dl_col for n in range(dv // LANES)], axis=1)
    return o, s_dec + upd


def _gla_out(o, gn, gs):
    return (_rms(o, gn) * gs.astype(F32)).astype(BF16)


def _gla_seq_kernel(sub, heads, dk, dv, *refs):
    _gla_seq_body(sub, heads, dk, dv, pl.program_id(1) == 0, *refs)


def _gla_seq_body(sub, heads, dk, dv, first_chunk, q_ref, k_ref, v_ref, lg_ref, gs_ref, gn_ref, s0_ref, ob_ref, s_ref):
    @pl.when(first_chunk)
    def _():
        def copy_head(h, carry):
            s_ref[0, h] = s0_ref[0, h]
            return carry

        lax.fori_loop(0, heads, copy_head, 0)

    for h in range(heads):
        ck, cv = slice(h * dk, (h + 1) * dk), slice(h * dv, (h + 1) * dv)
        o, s_new = _gla_chunk(q_ref[:, ck].astype(F32), k_ref[:, ck].astype(F32), v_ref[:, cv], lg_ref[:, ck],
                              s_ref[0, h], sub)
        s_ref[0, h] = s_new
        ob_ref[:, cv] = _gla_out(o, gn_ref[...], gs_ref[:, cv])


def _gla_seq(lin, logg, gn, s0, heads, dk, dv, nseq, seq_rows, chunk, sub, row0, shared_init):
    nchunk = seq_rows // chunk
    blk0 = row0 // chunk
    rb = lambda b, c: blk0 + b * nchunk + c
    wk, wv = heads * dk, heads * dv
    assert (2 * wk) % wv == 0
    return pl.pallas_call(
        functools.partial(_gla_seq_kernel, sub, heads, dk, dv),
        grid=(nseq, nchunk),
        in_specs=[pl.BlockSpec((chunk, wk), lambda b, c: (rb(b, c), 0)),
                  pl.BlockSpec((chunk, wk), lambda b, c: (rb(b, c), 1)),
                  pl.BlockSpec((chunk, wv), lambda b, c: (rb(b, c), 2 * wk // wv)),
                  pl.BlockSpec((chunk, wk), lambda b, c: (rb(b, c), 0)),
                  pl.BlockSpec((chunk, wv), lambda b, c: (rb(b, c), 2 * wk // wv + 1)),
                  pl.BlockSpec((1, dv), lambda b, c: (0, 0)),
                  pl.BlockSpec((1, heads, dk, dv), lambda b, c: (0 if shared_init else b, 0, 0, 0))],
        out_specs=[pl.BlockSpec((chunk, wv), lambda b, c: (b * nchunk + c, 0)),
                   pl.BlockSpec((1, heads, dk, dv), lambda b, c: (b, 0, 0, 0))],
        out_shape=[jax.ShapeDtypeStruct((nseq * seq_rows, wv), BF16),
                   jax.ShapeDtypeStruct((nseq, heads, dk, dv), F32)],
        compiler_params=_cparams("arbitrary", "arbitrary"),
        name="gla_seq")(lin, lin, lin, logg, lin, gn, s0)


def _gla_short_kernel(nb, sl, q_ref, k_ref, v_ref, lg_ref, gs_ref, gn_ref, s0_ref, ob_ref, s_ref):
    q = q_ref[...].astype(F32)
    k = k_ref[...].astype(F32)
    v = v_ref[...].astype(F32)
    lg = lg_ref[...]
    outs = []
    for n in range(nb):
        rs = slice(n * sl, (n + 1) * sl)
        o, s_new = _gla_chunk(q[rs], k[rs], v[rs].astype(BF16), lg[rs], s0_ref[n, 0], sl)
        s_ref[n, 0] = s_new
        outs.append(o)
    o = jnp.concatenate(outs, axis=0)
    ob_ref[...] = _gla_out(o, gn_ref[...], gs_ref[...])


def _gla_both_kernel(sub, heads, dk, dv, nchunk, nb, sl, n_long, n_short, *refs):
    (ql, kl, vl, lgl, gsl, gn, s0l, qs, ks, vs, lgs, gss, s0s, obl, sl_out, obs, ss_out) = refs
    step = pl.program_id(0)

    def long_part():
        _gla_seq_body(sub, heads, dk, dv, (step % nchunk) == 0, ql, kl, vl, lgl, gsl, gn, s0l, obl, sl_out)

    def short_part():
        _gla_short_kernel(nb, sl, qs, ks, vs, lgs, gss, gn, s0s, obs, ss_out)

    if n_long == n_short:
        long_part()
        short_part()
    else:
        pl.when(step < n_long)(long_part)
        pl.when(step < n_short)(short_part)


def _gla_both(lin_l, logg_l, s0_l, nseq_l, seq_rows, chunk, sub, lin_s, logg_s, s0_s, nseq_s, sl, nb, gn, heads, dk, dv):
    nchunk = seq_rows // chunk
    n_long, n_short = nseq_l * nchunk, (nseq_s // nb) * heads
    steps = max(n_long, n_short)
    wk, wv = heads * dk, heads * dv
    assert (2 * wk) % wv == 0
    rows = nb * sl
    kv = (2 * heads * dk) // dv
    lo = (lambda s: s) if n_long == steps else (lambda s: jnp.minimum(s, n_long - 1))
    sh = (lambda s: s) if n_short == steps else (lambda s: jnp.minimum(s, n_short - 1))
    long_row = lambda col: (lambda s: (lo(s), col))
    short_blk = lambda off: (lambda s: (sh(s) // heads, off + sh(s) % heads))
    return pl.pallas_call(
        functools.partial(_gla_both_kernel, sub, heads, dk, dv, nchunk, nb, sl, n_long, n_short),
        grid=(steps,),
        in_specs=[pl.BlockSpec((chunk, wk), long_row(0)), pl.BlockSpec((chunk, wk), long_row(1)),
                  pl.BlockSpec((chunk, wv), long_row(2 * wk // wv)), pl.BlockSpec((chunk, wk), long_row(0)),
                  pl.BlockSpec((chunk, wv), long_row(2 * wk // wv + 1)),
                  pl.BlockSpec((1, dv), lambda s: (0, 0)),
                  pl.BlockSpec((1, heads, dk, dv), lambda s: (0, 0, 0, 0)),
                  pl.BlockSpec((rows, dk), short_blk(0)), pl.BlockSpec((rows, dk), short_blk(heads)),
                  pl.BlockSpec((rows, dv), short_blk(kv)), pl.BlockSpec((rows, dk), short_blk(0)),
                  pl.BlockSpec((rows, dv), short_blk(kv + heads)),
                  pl.BlockSpec((nb, 1, dk, dv), lambda s: (sh(s) // heads, sh(s) % heads, 0, 0))],
        out_specs=[pl.BlockSpec((chunk, wv), long_row(0)),
                   pl.BlockSpec((1, heads, dk, dv), lambda s: (lo(s) // nchunk, 0, 0, 0)),
                   pl.BlockSpec((rows, dv), short_blk(0)),
                   pl.BlockSpec((nb, 1, dk, dv), lambda s: (sh(s) // heads, sh(s) % heads, 0, 0))],
        out_shape=[jax.ShapeDtypeStruct((nseq_l * seq_rows, wv), BF16),
                   jax.ShapeDtypeStruct((nseq_l, heads, dk, dv), F32),
                   jax.ShapeDtypeStruct((nseq_s * sl, wv), BF16),
                   jax.ShapeDtypeStruct((nseq_s, heads, dk, dv), F32)],
        compiler_params=_cparams("arbitrary"),
        name="gla_both")(lin_l, lin_l, lin_l, logg_l, lin_l, gn, s0_l, lin_s, lin_s, lin_s, logg_s, lin_s, s0_s)


def _merge_kernel(ca_ref, ob_ref, ga_ref, gb_ref, wc_ref, wg_ref, o_ref):
    ya = _dot(ca_ref[...], wc_ref[...])
    yb = _dot(ob_ref[...], wg_ref[...])
    o_ref[...] = (ga_ref[...].astype(F32) * ya + gb_ref[...].astype(F32) * yb).astype(BF16)


def _merge(ca, ob, sig, wc, wg, tm, tn):
    rows, dc = ca.shape
    dg, d = wg.shape
    nn = d // tn
    return pl.pallas_call(
        _merge_kernel, grid=(rows // tm, nn),
        in_specs=[pl.BlockSpec((tm, dc), lambda i, j: (i, 0)), pl.BlockSpec((tm, dg), lambda i, j: (i, 0)),
                  pl.BlockSpec((tm, tn), lambda i, j: (i, j)), pl.BlockSpec((tm, tn), lambda i, j: (i, nn + j)),
                  pl.BlockSpec((dc, tn), lambda i, j: (0, j)), pl.BlockSpec((dg, tn), lambda i, j: (0, j))],
        out_specs=pl.BlockSpec((tm, tn), lambda i, j: (i, j)),
        out_shape=jax.ShapeDtypeStruct((rows, d), BF16),
        compiler_params=_cparams("arbitrary", "arbitrary"), name="merge")(ca, ob, sig, sig, wc, wg)


def _oproj_kernel(m_ref, x_ref, w_ref, g_ref, h_ref, n2_ref):
    h = x_ref[...] + _dot(m_ref[...], w_ref[...])
    h_ref[...] = h
    n2_ref[...] = _rms(h, g_ref[...]).astype(BF16)


def _oproj(m, x, w, g, tm):
    rows, d = x.shape
    row = lambda i: (i, 0)
    return pl.pallas_call(
        _oproj_kernel, grid=(rows // tm,),
        in_specs=[pl.BlockSpec((tm, d), row), pl.BlockSpec((tm, d), row),
                  pl.BlockSpec((d, d), lambda i: (0, 0)), pl.BlockSpec((1, d), lambda i: (0, 0))],
        out_specs=[pl.BlockSpec((tm, d), row), pl.BlockSpec((tm, d), row)],
        out_shape=[jax.ShapeDtypeStruct((rows, d), F32), jax.ShapeDtypeStruct((rows, d), BF16)],
        compiler_params=_cparams("arbitrary"), name="oproj")(m, x, w, g)


def _ffn_up_kernel(lay, tm, fw, tiles_per_seq, *refs):
    n_ref, wa_ref, wg_ref, cwt_ref, b_ref = refs[:5]
    if lay.mode == "carry":
        init_ref, act_ref, st_ref, wbf_ref, halo_ref, carry_ref = refs[5:]
    else:
        prev2_ref, prev1_ref, act_ref, last2_ref, last1_ref, tail_ref, wbf_ref, tab2_ref, tab1_ref, gt_ref = refs[5:]
    j, i = pl.program_id(0), pl.program_id(1)

    @pl.when(i == 0)
    def _():
        _cast_into(wbf_ref, 0, 0, wa_ref)
        _cast_into(wbf_ref, 0, fw, wg_ref)

    if lay.mode == "carry":
        _load_halo(halo_ref, init_ref, carry_ref, i, tiles_per_seq)
        conv = _CausalConv(cwt_ref, lay, halo_ref[...])
    else:
        _fill_tables(lay, tab2_ref, tab1_ref, prev2_ref, prev1_ref)
        conv = _CausalConv(cwt_ref, lay, jnp.zeros((HALO, fw), F32), tab2_ref, tab1_ref)

    def matmul(r0, rs):
        return _dot(n_ref[r0:r0 + rs, :], wbf_ref[...])

    def epilogue(r0, rs, p):
        gt = p[:, fw:]
        if lay.mode == "table":
            _lane_chunks_store(gt_ref, r0, gt)
        z = conv(gt, r0) + b_ref[...]
        act_ref[r0:r0 + rs, :] = (_silu(z) * p[:, :fw]).astype(BF16)

    _sliced(tm, matmul, epilogue)
    if lay.mode == "carry":
        _store_tail(conv.tail, carry_ref, st_ref, i, j, tiles_per_seq)
    else:
        _store_short_states(lay, gt_ref, conv.tail, last2_ref, last1_ref, tail_ref)


def _ffn_up(n2, w_up, conv_w, bias, lay, tm, fw, init=None, prev2=None, prev1=None):
    rows, d = n2.shape
    dff = conv_w.shape[1]
    nj = dff // fw
    tiles_per_seq = (lay.seq_rows // tm) if lay.mode == "carry" else 1
    xargs, xspecs, xout_specs, xout_shapes, xscratch = _conv_specs(lay, rows, tm, fw, nj, init, prev2, prev1)
    wspec = lambda part: pl.BlockSpec((None, d, fw), lambda j, i: (0, 0, part * nj + j))
    return pl.pallas_call(
        functools.partial(_ffn_up_kernel, lay, tm, fw, tiles_per_seq),
        grid=(nj, rows // tm),
        in_specs=[pl.BlockSpec((tm, d), lambda j, i: (i, 0)), wspec(0), wspec(1),
                  pl.BlockSpec((CONV_W, fw), lambda j, i: (0, j)), pl.BlockSpec((1, fw), lambda j, i: (0, j))]
                 + xspecs,
        out_specs=[pl.BlockSpec((tm, fw), lambda j, i: (i, j))] + xout_specs,
        out_shape=[jax.ShapeDtypeStruct((rows, dff), BF16)] + xout_shapes,
        scratch_shapes=[pltpu.VMEM((d, 2 * fw), BF16)] + xscratch,
        compiler_params=_cparams("arbitrary", "arbitrary"),
        name="ffn_up_" + lay.mode)(n2, w_up, w_up, conv_w, bias, *xargs)


def _ffn_down_kernel(act_ref, w_ref, h_ref, g_ref, y_ref):
    y_ref[...] = _rms(h_ref[...] + _dot(act_ref[...], w_ref[...]), g_ref[...])


def _ffn_down(act, w, h, g, rows, tm):
    dff = act.shape[1]
    d = w.shape[1]
    row = lambda i: (i, 0)
    return pl.pallas_call(
        _ffn_down_kernel, grid=(rows // tm,),
        in_specs=[pl.BlockSpec((tm, dff), row),
                  pl.BlockSpec((dff, d), lambda i: (0, 0), pipeline_mode=pl.Buffered(1)),
                  pl.BlockSpec((tm, d), row), pl.BlockSpec((1, d), lambda i: (0, 0))],
        out_specs=pl.BlockSpec((tm, d), row),
        out_shape=jax.ShapeDtypeStruct((rows, d), F32),
        compiler_params=_cparams("arbitrary"), name="ffn_down")(act, w, h, g)


def kernel(x_prompt, x_sample, state_conv, state_gla, state_ffn_conv, meta_tokens, norm_mix_g, w_in, conv_mix_w, w_conv_out, w_gate_up, b_gate, gla_norm_g, w_gla_out, w_o, norm_ffn_g, w_ffn_up, ffn_conv_w, ffn_conv_b, w_ffn_down, final_norm_g):
    bp, seq, d = x_prompt.shape
    bs, sl, _ = x_sample.shape
    assert w_in.shape[0] == 1, "single-layer step"
    n_meta = meta_tokens.shape[0]
    dc = state_conv.shape[-1]
    _, _, heads, dk, dv = state_gla.shape
    dff = state_ffn_conv.shape[-1]
    rank = w_gate_up.shape[1]
    assert n_meta % GLA_SUB == 0 and seq % GLA_SUB == 0 and GLA_SUB % sl == 0 and sl >= CONV_W - 1

    o_q = 3 * dc
    o_a = o_q + 2 * heads * dk + 2 * heads * dv
    o_ga = o_a + rank
    tn = heads * dk
    assert o_q % tn == 0 and (heads * dv) % tn == 0 and d % tn == 0
    cw, fw = CONV_COLS, FFN_COLS
    tiles_qkvg = ((2 * heads * dk + heads * dv) // tn, heads * dv // tn, 0)
    tiles_gates = (0, 0, 2 * d // tn)
    spec_qkvg = pl.BlockSpec((None, tn, d), lambda j, i: (0, o_q // tn + j, 0))
    spec_gates = pl.BlockSpec((pl.Element(1), pl.Element(tn), pl.Element(d)), lambda j, i: (0, (o_ga // 8 + j * (tn // 8)) * 8, 0))
    assert o_ga % 8 == 0
    q_scale = float(dk) ** -0.5

    w_in_t = jnp.swapaxes(w_in, 1, 2)
    wa = jnp.pad(w_in_t[0, o_a:o_ga], ((0, LANES - rank), (0, 0))).astype(BF16)
    wup = jnp.pad(w_gate_up[0], ((0, LANES - rank), (0, 0))).astype(BF16)
    wc = w_conv_out[0].astype(BF16)
    wg = w_gla_out[0].astype(BF16)
    wo = w_o[0].astype(BF16)
    wd = w_ffn_down[0].astype(BF16)
    g1, g2, gf, gn = norm_mix_g[0][None], norm_ffn_g[0][None], final_norm_g[None], gla_norm_g[0][None]
    bg, fb, cmw, fcw = b_gate[0][None], ffn_conv_b[0][None], conv_mix_w[0], ffn_conv_w[0]

    def in_projections(n, tm, lay, **conv_kw):
        conv_outs = _inproj_conv(n, w_in_t, cmw, lay, tm, cw, **conv_kw)
        lin = _inproj_act(n, w_in_t, spec_qkvg, tiles_qkvg, tm, tn, q_scale)
        sig = _inproj_act(n, w_in_t, spec_gates, tiles_gates, tm, tn, 1.0)
        return conv_outs, lin, sig

    n_short = bs * sl
    rows_s = n_short + n_meta
    xs = jnp.concatenate([x_sample.reshape(n_short, d), meta_tokens.astype(x_sample.dtype)], axis=0)
    lay_s = SeqLayout("table", n_short=n_short, short_len=sl)
    tm_s = rows_s
    n_s, logg_s = _rmsnorm_gate(xs, g1, wa, wup, bg, _row_tile(rows_s, ROW_TILE_RESIDENT))
    (ca_s, conv_s2, conv_s1, conv_tail), lin_s, sig_s = in_projections(
        n_s, tm_s, lay_s, prev2=state_conv[0, :, 0], prev1=state_conv[0, :, 1])
    s_zero = jnp.zeros((1, heads, dk, dv), F32)
    ob_m, s_meta = _gla_seq(lin_s, logg_s, gn, s_zero, heads, dk, dv, 1, n_meta, n_meta, GLA_SUB,
                            n_short, True)

    rows_p = bp * seq
    xp = x_prompt.reshape(rows_p, d)
    lay_p = SeqLayout("carry", seq_rows=seq)
    tm_p = _row_tile(seq, ROW_TILE)
    tm_o = _row_tile(seq, ROW_TILE_RESIDENT)
    n_p, logg_p = _rmsnorm_gate(xp, g1, wa, wup, bg, tm_o)
    (ca_p, conv_p), lin_p, sig_p = in_projections(n_p, tm_p, lay_p, init=conv_tail)
    ob_p, s_p, ob_smp, s_smp = _gla_both(lin_p, logg_p, s_meta, bp, seq, _row_tile(seq, GLA_CHUNK), GLA_SUB,
                                         lin_s, logg_s, state_gla[0], bs, sl, SHORT_SEQS_PER_STEP,
                                         gn, heads, dk, dv)

    ob_s = jnp.concatenate([ob_smp, ob_m], axis=0)
    m_s = _merge(ca_s, ob_s, sig_s, wc, wg, tm_s, tn)
    h_s, n2_s = _oproj(m_s, xs, wo, g2, _row_tile(rows_s, ROW_TILE_RESIDENT))
    act_s, ffn_s2, ffn_s1, ffn_tail = _ffn_up(n2_s, w_ffn_up, fcw, fb, lay_s, tm_s, fw,
                                              prev2=state_ffn_conv[0, :, 0], prev1=state_ffn_conv[0, :, 1])
    y_s = _ffn_down(act_s, wd, h_s, gf, n_short, _row_tile(n_short, ROW_TILE_WIDE_K))

    m_p = _merge(ca_p, ob_p, sig_p, wc, wg, _row_tile(seq, ROW_TILE_MERGE), tn)
    h_p, n2_p = _oproj(m_p, xp, wo, g2, tm_o)
    act_p, ffn_p = _ffn_up(n2_p, w_ffn_up, fcw, fb, lay_p, tm_p, fw, init=ffn_tail)
    y_p = _ffn_down(act_p, wd, h_p, gf, rows_p, _row_tile(seq, ROW_TILE_WIDE_K))

    untile = lambda st: jnp.swapaxes(st, 1, 2).reshape(st.shape[0], CONV_W - 1, -1)
    return (y_p.reshape(bp, seq, d), y_s.reshape(bs, sl, d),
            untile(conv_p)[None], s_p[None], untile(ffn_p)[None],
            jnp.stack([conv_s2, conv_s1], axis=1)[None], s_smp[None], jnp.stack([ffn_s2, ffn_s1], axis=1)[None])
```

```python
import functools

import jax
import jax.numpy as jnp
from jax import lax
from jax.experimental import pallas as pl
from jax.experimental.pallas import tpu as pltpu

EPS = 1e-6
GATE_TAU = 16.0
CONV_W = 3
GLA_SUB = 16
HALO = 8
V7X_VMEM_LIMIT = 56 * 1024 * 1024
LANES = 128
ROW_TILE = 2048
ROW_TILE_RESIDENT = 512
ROW_TILE_MERGE = 1024
ROW_TILE_WIDE_K = 256
GLA_CHUNK = 128
CONV_COLS = 256
FFN_COLS = 512
SHORT_SEQS_PER_STEP = 8
SUB_ROWS = 256
CAST_ROWS = 64
F32 = jnp.float32
BF16 = jnp.bfloat16


def _cparams(*sem):
    return pltpu.CompilerParams(dimension_semantics=sem, vmem_limit_bytes=V7X_VMEM_LIMIT)


def _row_tile(rows, target):
    best = None
    for t in range(16, min(rows, target) + 1, 16):
        if rows % t == 0:
            best = t
    return best or rows


def _rms(x, g):
    return x * lax.rsqrt(jnp.mean(x * x, axis=-1, keepdims=True) + EPS) * g


def _sigmoid(x):
    return 0.5 * jnp.tanh(0.5 * x) + 0.5


def _silu(x):
    h = 0.5 * x
    return h * jnp.tanh(h) + h


def _dot(a, b):
    return jnp.dot(a, b, preferred_element_type=F32)


def _dot_nt(a, b):
    return lax.dot_general(a, b, (((1,), (1,)), ((), ())), preferred_element_type=F32)


def _sliced(tm, matmul, epilogue):
    n = max(1, tm // SUB_ROWS)
    sizes = [SUB_ROWS] * (n - 1) + [tm - SUB_ROWS * (n - 1)]
    r0 = 0
    for rs in sizes:
        epilogue(r0, rs, matmul(r0, rs))
        r0 += rs


def _cast_into(dst_ref, row0, col0, src_ref):
    rows, cols = src_ref.shape

    def body(r, carry):
        off = pl.multiple_of(r * CAST_ROWS, CAST_ROWS)
        dst_ref[pl.ds(row0 + off, CAST_ROWS), col0:col0 + cols] = src_ref[pl.ds(off, CAST_ROWS), :].astype(BF16)
        return carry

    lax.fori_loop(0, rows // CAST_ROWS, body, 0)


def _rmsnorm_gate_kernel(x_ref, g_ref, wa_ref, wup_ref, b_ref, n_ref, lg_ref):
    n = _rms(x_ref[...], g_ref[...]).astype(BF16)
    n_ref[...] = n
    a = _dot_nt(n, wa_ref[...]).astype(BF16)
    z = _dot(a, wup_ref[...]) + b_ref[...]
    lg_ref[...] = (jnp.minimum(z, 0.0) - jnp.log(1.0 + jnp.exp(-jnp.abs(z)))) * (1.0 / GATE_TAU)


def _rmsnorm_gate(x, g, wa, wup, b, tm):
    rows, d = x.shape
    rp, dk = wup.shape
    const = lambda i: (0, 0)
    return pl.pallas_call(
        _rmsnorm_gate_kernel, grid=(rows // tm,),
        in_specs=[pl.BlockSpec((tm, d), lambda i: (i, 0)), pl.BlockSpec((1, d), const),
                  pl.BlockSpec((rp, d), const), pl.BlockSpec((rp, dk), const), pl.BlockSpec((1, dk), const)],
        out_specs=[pl.BlockSpec((tm, d), lambda i: (i, 0)), pl.BlockSpec((tm, dk), lambda i: (i, 0))],
        out_shape=[jax.ShapeDtypeStruct((rows, d), BF16), jax.ShapeDtypeStruct((rows, dk), F32)],
        compiler_params=_cparams("arbitrary"), name="rmsnorm_gate")(x, g, wa, wup, b)


class SeqLayout:
    def __init__(self, mode, seq_rows=None, n_short=None, short_len=None):
        self.mode, self.seq_rows, self.n_short, self.short_len = mode, seq_rows, n_short, short_len


def _load_halo(halo_ref, init_ref, carry_ref, i, tiles_per_seq):
    first = (i % tiles_per_seq) == 0

    @pl.when(first)
    def _():
        halo_ref[...] = init_ref[...]

    @pl.when(jnp.logical_not(first))
    def _():
        halo_ref[...] = carry_ref[...]


def _store_tail(tail, carry_ref, st_ref, i, j, tiles_per_seq):
    carry_ref[...] = tail

    @pl.when((i % tiles_per_seq) == tiles_per_seq - 1)
    def _():
        st_ref[i // tiles_per_seq, j] = tail[HALO - (CONV_W - 1):, :]


def _shift_rows(x, prev_row):
    rows, c = x.shape
    prev8 = jnp.broadcast_to(prev_row, (HALO, c))
    above = jnp.concatenate([prev8, x[:rows - HALO]], axis=0) if rows > HALO else prev8
    last = lax.broadcasted_iota(jnp.int32, (rows, 1), 0) % HALO == HALO - 1
    mixed = jnp.where(last, above, x).reshape(rows // HALO, HALO, c)
    return pltpu.roll(mixed, 1, 1).reshape(rows, c)


class _CausalConv:
    def __init__(self, w_ref, lay, halo, prev2_ref=None, prev1_ref=None):
        w = w_ref[...]
        self.w0, self.w1, self.w2 = w[0:1, :], w[1:2, :], w[2:3, :]
        self.lay, self.tail, self.prev2_ref, self.prev1_ref = lay, halo, prev2_ref, prev1_ref

    def __call__(self, u, r0):
        rs, c = u.shape
        lay, w0, w1 = self.lay, self.w0, self.w1
        um2, um1 = self.tail[HALO - 2:HALO - 1, :], self.tail[HALO - 1:HALO, :]
        s0 = _shift_rows(w0 * u, w0 * um1)
        s1_first = w1 * um1 + w0 * um2
        if lay.mode == "table":
            take = max(0, min(lay.n_short, r0 + rs) - r0)

            def table_rows(ref):
                parts = ([_lane_chunks_load(ref, r0, take)] if take else []) + (
                    [jnp.zeros((rs - take, c), F32)] if take < rs else [])
                return parts[0] if len(parts) == 1 else jnp.concatenate(parts, axis=0)

            p2, p1 = table_rows(self.prev2_ref), table_rows(self.prev1_ref)
            t = r0 + lax.broadcasted_iota(jnp.int32, (rs, 1), 0)
            starts = jnp.where(t < lay.n_short, t % lay.short_len, t - lay.n_short) == 0
            s0 = jnp.where(starts, w0 * p1, s0)
        a = w1 * u + s0
        s1 = _shift_rows(a, s1_first)
        if lay.mode == "table":
            s1 = jnp.where(starts, w1 * p1 + w0 * p2, s1)
        self.tail = u[rs - HALO:, :]
        return self.w2 * u + s1


def _conv_specs(lay, rows, tm, c, nj, init, prev2, prev1):
    col = lambda j, i: (0, j)
    if lay.mode == "carry":
        nseq = rows // lay.seq_rows
        return ((init,), [pl.BlockSpec((HALO, c), col)],
                [pl.BlockSpec((nseq, nj, CONV_W - 1, c), lambda j, i: (0, 0, 0, 0))],
                [jax.ShapeDtypeStruct((nseq, nj, CONV_W - 1, c), F32)],
                [pltpu.VMEM((HALO, c), F32)] * 2)
    assert tm == rows and lay.short_len >= CONV_W - 1
    nseq = lay.n_short // lay.short_len
    return ((prev2, prev1), [pl.BlockSpec((nseq, c), col)] * 2,
            [pl.BlockSpec((nseq, c), col)] * 2 + [pl.BlockSpec((HALO, c), col)],
            [jax.ShapeDtypeStruct((nseq, nj * c), F32)] * 2 + [jax.ShapeDtypeStruct((HALO, nj * c), F32)],
            [pltpu.VMEM((c // LANES, lay.n_short, LANES), F32)] * 2 + [pltpu.VMEM((c // LANES, tm, LANES), F32)])


def _lane_chunks_load(ref, r0, rows):
    return jnp.concatenate([ref[k, r0:r0 + rows, :] for k in range(ref.shape[0])], axis=1)


def _lane_chunks_store(ref, r0, x):
    for k in range(ref.shape[0]):
        ref[k, r0:r0 + x.shape[0], :] = x[:, k * LANES:(k + 1) * LANES]


def _fill_tables(lay, tab2_ref, tab1_ref, prev2_ref, prev1_ref):
    nseq = lay.n_short // lay.short_len
    first_rows = pl.ds(0, nseq, stride=lay.short_len)
    for tab_ref, prev_ref in ((tab2_ref, prev2_ref), (tab1_ref, prev1_ref)):
        tab_ref[...] = jnp.zeros(tab_ref.shape, F32)
        for k in range(tab_ref.shape[0]):
            tab_ref[k, first_rows, :] = prev_ref[:, k * LANES:(k + 1) * LANES]


def _store_short_states(lay, rows_ref, tail, last2_ref, last1_ref, tail_ref):
    nseq = lay.n_short // lay.short_len
    for k in range(rows_ref.shape[0]):
        cols = slice(k * LANES, (k + 1) * LANES)
        last2_ref[:, cols] = rows_ref[k, pl.ds(lay.short_len - 2, nseq, stride=lay.short_len), :]
        last1_ref[:, cols] = rows_ref[k, pl.ds(lay.short_len - 1, nseq, stride=lay.short_len), :]
    tail_ref[...] = tail


def _inproj_conv_kernel(lay, tm, cw, tiles_per_seq, *refs):
    n_ref, wb_ref, wc_ref, wh_ref, cwt_ref = refs[:5]
    if lay.mode == "carry":
        init_ref, ca_ref, st_ref, wbf_ref, halo_ref, carry_ref = refs[5:]
    else:
        prev2_ref, prev1_ref, ca_ref, last2_ref, last1_ref, tail_ref, wbf_ref, tab2_ref, tab1_ref, u_ref = refs[5:]
    j, i = pl.program_id(0), pl.program_id(1)

    @pl.when(i == 0)
    def _():
        for part, w_ref in enumerate((wb_ref, wc_ref, wh_ref)):
            _cast_into(wbf_ref, part * cw, 0, w_ref)

    if lay.mode == "carry":
        _load_halo(halo_ref, init_ref, carry_ref, i, tiles_per_seq)
        conv = _CausalConv(cwt_ref, lay, halo_ref[...])
    else:
        _fill_tables(lay, tab2_ref, tab1_ref, prev2_ref, prev1_ref)
        conv = _CausalConv(cwt_ref, lay, jnp.zeros((HALO, cw), F32), tab2_ref, tab1_ref)

    def matmul(r0, rs):
        return _dot_nt(n_ref[r0:r0 + rs, :], wbf_ref[...])

    def epilogue(r0, rs, p):
        u = p[:, cw:2 * cw] * p[:, 2 * cw:]
        if lay.mode == "table":
            _lane_chunks_store(u_ref, r0, u)
        ca_ref[r0:r0 + rs, :] = (p[:, :cw] * conv(u, r0)).astype(BF16)

    _sliced(tm, matmul, epilogue)
    if lay.mode == "carry":
        _store_tail(conv.tail, carry_ref, st_ref, i, j, tiles_per_seq)
    else:
        _store_short_states(lay, u_ref, conv.tail, last2_ref, last1_ref, tail_ref)


def _inproj_conv(n, w_in_t, conv_w, lay, tm, cw, init=None, prev2=None, prev1=None):
    rows, d = n.shape
    dc = conv_w.shape[1]
    nj = dc // cw
    tiles_per_seq = (lay.seq_rows // tm) if lay.mode == "carry" else 1
    xargs, xspecs, xout_specs, xout_shapes, xscratch = _conv_specs(lay, rows, tm, cw, nj, init, prev2, prev1)
    wspec = lambda part: pl.BlockSpec((None, cw, d), lambda j, i: (0, part * nj + j, 0))
    return pl.pallas_call(
        functools.partial(_inproj_conv_kernel, lay, tm, cw, tiles_per_seq),
        grid=(nj, rows // tm),
        in_specs=[pl.BlockSpec((tm, d), lambda j, i: (i, 0)), wspec(0), wspec(1), wspec(2),
                  pl.BlockSpec((CONV_W, cw), lambda j, i: (0, j))] + xspecs,
        out_specs=[pl.BlockSpec((tm, cw), lambda j, i: (i, j))] + xout_specs,
        out_shape=[jax.ShapeDtypeStruct((rows, dc), BF16)] + xout_shapes,
        scratch_shapes=[pltpu.VMEM((3 * cw, d), BF16)] + xscratch,
        compiler_params=_cparams("arbitrary", "arbitrary"),
        name="inproj_conv_" + lay.mode)(n, w_in_t, w_in_t, w_in_t, conv_w, *xargs)


def _inproj_act_kernel(tm, n_lin, n_silu, q_scale, n_ref, w_ref, o_ref, wbf_ref):
    j, i = pl.program_id(0), pl.program_id(1)

    @pl.when(i == 0)
    def _():
        _cast_into(wbf_ref, 0, 0, w_ref.at[0] if len(w_ref.shape) == 3 else w_ref)

    def matmul(r0, rs):
        return _dot_nt(n_ref[r0:r0 + rs, :], wbf_ref[...])

    def epilogue(r0, rs, p):
        if n_lin:
            act = p * jnp.where(j == 0, q_scale, 1.0)
            if n_silu:
                act = jnp.where(j >= n_lin, _silu(p), act)
        else:
            act = jnp.where(j < n_silu, _silu(p), _sigmoid(p)) if n_silu else _sigmoid(p)
        o_ref[r0:r0 + rs, :] = act.astype(BF16)

    _sliced(tm, matmul, epilogue)


def _inproj_act(n, w_t, w_spec, tiles, tm, tn, q_scale):
    rows, d = n.shape
    n_lin, n_silu, n_sig = tiles
    assert not (n_lin and n_sig)
    n_col = n_lin + n_silu + n_sig
    return pl.pallas_call(
        functools.partial(_inproj_act_kernel, tm, n_lin, n_silu, q_scale),
        grid=(n_col, rows // tm),
        in_specs=[pl.BlockSpec((tm, d), lambda j, i: (i, 0)), w_spec],
        out_specs=pl.BlockSpec((tm, tn), lambda j, i: (i, j)),
        out_shape=jax.ShapeDtypeStruct((rows, n_col * tn), BF16),
        scratch_shapes=[pltpu.VMEM((tn, d), BF16)],
        compiler_params=_cparams("arbitrary", "arbitrary"), name="inproj_act")(n, w_t)


def _cumsum_groups(x, sub):
    rows = x.shape[0]
    pos = lax.broadcasted_iota(jnp.int32, (rows, 1), 0) % sub
    s = 1
    while s < sub:
        x = x + jnp.where(pos >= s, pltpu.roll(x, s, 0), 0.0)
        s *= 2
    return x


def _gla_chunk(q, k, v, lg, s, sub):
    c, dk = q.shape
    ngrp = c // sub
    gpu = 2 if ngrp % 2 == 0 else 1
    unit = gpu * sub
    bt = _cumsum_groups(lg, sub)
    r = jnp.zeros((1, dk), F32)
    r_grp, b_rows = [], []
    for g in range(ngrp):
        r_grp.append(r)
        b_rows.append(bt[g * sub:(g + 1) * sub, :] + r)
        r = r + bt[(g + 1) * sub - 1:(g + 1) * sub, :]
    b = jnp.concatenate(b_rows, axis=0) if ngrp > 1 else b_rows[0]
    b_last = r
    qe = (q * jnp.exp(b)).astype(BF16)
    kd = (k * jnp.exp(b_last - b)).astype(BF16)
    att_rows = []
    for i in range(c // unit):
        ref = r_grp[i * gpu + gpu - 1]
        rows = slice(i * unit, (i + 1) * unit)
        seen = (i + 1) * unit
        qt = (q[rows] * jnp.exp(b[rows] - ref)).astype(BF16)
        ke = (k[:seen] * jnp.exp(ref - b[:seen])).astype(BF16)
        if seen < c:
            ke = jnp.concatenate([ke, jnp.zeros((c - seen, dk), BF16)], axis=0)
        a = _dot_nt(qt, ke)
        col = lax.broadcasted_iota(jnp.int32, (unit, c), 1)
        rloc = lax.broadcasted_iota(jnp.int32, (unit, c), 0)
        att_rows.append(jnp.where(col <= rloc + i * unit, a, 0.0))
    att = (jnp.concatenate(att_rows, axis=0) if len(att_rows) > 1 else att_rows[0]).astype(BF16)
    o = _dot(att, v) + _dot(qe, s.astype(BF16))
    upd = lax.dot_general(kd, v, (((0,), (0,)), ((), ())), preferred_element_type=F32)
    dl = jnp.exp(b_last)
    dl_col = jnp.transpose(jnp.broadcast_to(dl, (LANES, dk)))
    dv = s.shape[1]
    s_dec = jnp.concatenate([s[:, n * LANES:(n + 1) * LANES] * dl_col for n in range(dv // LANES)], axis=1)
    return o, s_dec + upd


def _gla_out(o, gn, gs):
    return (_rms(o, gn) * gs.astype(F32)).astype(BF16)


def _gla_seq_kernel(sub, heads, dk, dv, *refs):
    _gla_seq_body(sub, heads, dk, dv, pl.program_id(1) == 0, *refs)


def _gla_seq_body(sub, heads, dk, dv, first_chunk, q_ref, k_ref, v_ref, lg_ref, gs_ref, gn_ref, s0_ref, ob_ref, s_ref):
    @pl.when(first_chunk)
    def _():
        def copy_head(h, carry):
            s_ref[0, h] = s0_ref[0, h]
            return carry

        lax.fori_loop(0, heads, copy_head, 0)

    for h in range(heads):
        ck, cv = slice(h * dk, (h + 1) * dk), slice(h * dv, (h + 1) * dv)
        o, s_new = _gla_chunk(q_ref[:, ck].astype(F32), k_ref[:, ck].astype(F32), v_ref[:, cv], lg_ref[:, ck],
                              s_ref[0, h], sub)
        s_ref[0, h] = s_new
        ob_ref[:, cv] = _gla_out(o, gn_ref[...], gs_ref[:, cv])


def _gla_seq(lin, logg, gn, s0, heads, dk, dv, nseq, seq_rows, chunk, sub, row0, shared_init):
    nchunk = seq_rows // chunk
    blk0 = row0 // chunk
    rb = lambda b, c: blk0 + b * nchunk + c
    wk, wv = heads * dk, heads * dv
    assert (2 * wk) % wv == 0
    return pl.pallas_call(
        functools.partial(_gla_seq_kernel, sub, heads, dk, dv),
        grid=(nseq, nchunk),
        in_specs=[pl.BlockSpec((chunk, wk), lambda b, c: (rb(b, c), 0)),
                  pl.BlockSpec((chunk, wk), lambda b, c: (rb(b, c), 1)),
                  pl.BlockSpec((chunk, wv), lambda b, c: (rb(b, c), 2 * wk // wv)),
                  pl.BlockSpec((chunk, wk), lambda b, c: (rb(b, c), 0)),
                  pl.BlockSpec((chunk, wv), lambda b, c: (rb(b, c), 2 * wk // wv + 1)),
                  pl.BlockSpec((1, dv), lambda b, c: (0, 0)),
                  pl.BlockSpec((1, heads, dk, dv), lambda b, c: (0 if shared_init else b, 0, 0, 0))],
        out_specs=[pl.BlockSpec((chunk, wv), lambda b, c: (b * nchunk + c, 0)),
                   pl.BlockSpec((1, heads, dk, dv), lambda b, c: (b, 0, 0, 0))],
        out_shape=[jax.ShapeDtypeStruct((nseq * seq_rows, wv), BF16),
                   jax.ShapeDtypeStruct((nseq, heads, dk, dv), F32)],
        compiler_params=_cparams("arbitrary", "arbitrary"),
        name="gla_seq")(lin, lin, lin, logg, lin, gn, s0)


def _gla_short_kernel(nb, sl, q_ref, k_ref, v_ref, lg_ref, gs_ref, gn_ref, s0_ref, ob_ref, s_ref):
    q = q_ref[...].astype(F32)
    k = k_ref[...].astype(F32)
    v = v_ref[...].astype(F32)
    lg = lg_ref[...]
    outs = []
    for n in range(nb):
        rs = slice(n * sl, (n + 1) * sl)
        o, s_new = _gla_chunk(q[rs], k[rs], v[rs].astype(BF16), lg[rs], s0_ref[n, 0], sl)
        s_ref[n, 0] = s_new
        outs.append(o)
    o = jnp.concatenate(outs, axis=0)
    ob_ref[...] = _gla_out(o, gn_ref[...], gs_ref[...])


def _gla_both_kernel(sub, heads, dk, dv, nchunk, nb, sl, n_long, n_short, *refs):
    (ql, kl, vl, lgl, gsl, gn, s0l, qs, ks, vs, lgs, gss, s0s, obl, sl_out, obs, ss_out) = refs
    step = pl.program_id(0)

    def long_part():
        _gla_seq_body(sub, heads, dk, dv, (step % nchunk) == 0, ql, kl, vl, lgl, gsl, gn, s0l, obl, sl_out)

    def short_part():
        _gla_short_kernel(nb, sl, qs, ks, vs, lgs, gss, gn, s0s, obs, ss_out)

    if n_long == n_short:
        long_part()
        short_part()
    else:
        pl.when(step < n_long)(long_part)
        pl.when(step < n_short)(short_part)


def _gla_both(lin_l, logg_l, s0_l, nseq_l, seq_rows, chunk, sub, lin_s, logg_s, s0_s, nseq_s, sl, nb, gn, heads, dk, dv):
    nchunk = seq_rows // chunk
    n_long, n_short = nseq_l * nchunk, (nseq_s // nb) * heads
    steps = max(n_long, n_short)
    wk, wv = heads * dk, heads * dv
    assert (2 * wk) % wv == 0
    rows = nb * sl
    kv = (2 * heads * dk) // dv
    lo = (lambda s: s) if n_long == steps else (lambda s: jnp.minimum(s, n_long - 1))
    sh = (lambda s: s) if n_short == steps else (lambda s: jnp.minimum(s, n_short - 1))
    long_row = lambda col: (lambda s: (lo(s), col))
    short_blk = lambda off: (lambda s: (sh(s) // heads, off + sh(s) % heads))
    return pl.pallas_call(
        functools.partial(_gla_both_kernel, sub, heads, dk, dv, nchunk, nb, sl, n_long, n_short),
        grid=(steps,),
        in_specs=[pl.BlockSpec((chunk, wk), long_row(0)), pl.BlockSpec((chunk, wk), long_row(1)),
                  pl.BlockSpec((chunk, wv), long_row(2 * wk // wv)), pl.BlockSpec((chunk, wk), long_row(0)),
                  pl.BlockSpec((chunk, wv), long_row(2 * wk // wv + 1)),
                  pl.BlockSpec((1, dv), lambda s: (0, 0)),
                  pl.BlockSpec((1, heads, dk, dv), lambda s: (0, 0, 0, 0)),
                  pl.BlockSpec((rows, dk), short_blk(0)), pl.BlockSpec((rows, dk), short_blk(heads)),
                  pl.BlockSpec((rows, dv), short_blk(kv)), pl.BlockSpec((rows, dk), short_blk(0)),
                  pl.BlockSpec((rows, dv), short_blk(kv + heads)),
                  pl.BlockSpec((nb, 1, dk, dv), lambda s: (sh(s) // heads, sh(s) % heads, 0, 0))],
        out_specs=[pl.BlockSpec((chunk, wv), long_row(0)),
                   pl.BlockSpec((1, heads, dk, dv), lambda s: (lo(s) // nchunk, 0, 0, 0)),
                   pl.BlockSpec((rows, dv), short_blk(0)),
                   pl.BlockSpec((nb, 1, dk, dv), lambda s: (sh(s) // heads, sh(s) % heads, 0, 0))],
        out_shape=[jax.ShapeDtypeStruct((nseq_l * seq_rows, wv), BF16),
                   jax.ShapeDtypeStruct((nseq_l, heads, dk, dv), F32),
                   jax.ShapeDtypeStruct((nseq_s * sl, wv), BF16),
                   jax.ShapeDtypeStruct((nseq_s, heads, dk, dv), F32)],
        compiler_params=_cparams("arbitrary"),
        name="gla_both")(lin_l, lin_l, lin_l, logg_l, lin_l, gn, s0_l, lin_s, lin_s, lin_s, logg_s, lin_s, s0_s)


def _merge_kernel(ca_ref, ob_ref, ga_ref, gb_ref, wc_ref, wg_ref, o_ref):
    ya = _dot(ca_ref[...], wc_ref[...])
    yb = _dot(ob_ref[...], wg_ref[...])
    o_ref[...] = (ga_ref[...].astype(F32) * ya + gb_ref[...].astype(F32) * yb).astype(BF16)


def _merge(ca, ob, sig, wc, wg, tm, tn):
    rows, dc = ca.shape
    dg, d = wg.shape
    nn = d // tn
    return pl.pallas_call(
        _merge_kernel, grid=(rows // tm, nn),
        in_specs=[pl.BlockSpec((tm, dc), lambda i, j: (i, 0)), pl.BlockSpec((tm, dg), lambda i, j: (i, 0)),
                  pl.BlockSpec((tm, tn), lambda i, j: (i, j)), pl.BlockSpec((tm, tn), lambda i, j: (i, nn + j)),
                  pl.BlockSpec((dc, tn), lambda i, j: (0, j)), pl.BlockSpec((dg, tn), lambda i, j: (0, j))],
        out_specs=pl.BlockSpec((tm, tn), lambda i, j: (i, j)),
        out_shape=jax.ShapeDtypeStruct((rows, d), BF16),
        compiler_params=_cparams("arbitrary", "arbitrary"), name="merge")(ca, ob, sig, sig, wc, wg)


def _oproj_kernel(m_ref, x_ref, w_ref, g_ref, h_ref, n2_ref):
    h = x_ref[...] + _dot(m_ref[...], w_ref[...])
    h_ref[...] = h
    n2_ref[...] = _rms(h, g_ref[...]).astype(BF16)


def _oproj(m, x, w, g, tm):
    rows, d = x.shape
    row = lambda i: (i, 0)
    return pl.pallas_call(
        _oproj_kernel, grid=(rows // tm,),
        in_specs=[pl.BlockSpec((tm, d), row), pl.BlockSpec((tm, d), row),
                  pl.BlockSpec((d, d), lambda i: (0, 0)), pl.BlockSpec((1, d), lambda i: (0, 0))],
        out_specs=[pl.BlockSpec((tm, d), row), pl.BlockSpec((tm, d), row)],
        out_shape=[jax.ShapeDtypeStruct((rows, d), F32), jax.ShapeDtypeStruct((rows, d), BF16)],
        compiler_params=_cparams("arbitrary"), name="oproj")(m, x, w, g)


def _ffn_up_kernel(lay, tm, fw, tiles_per_seq, *refs):
    n_ref, wa_ref, wg_ref, cwt_ref, b_ref = refs[:5]
    if lay.mode == "carry":
        init_ref, act_ref, st_ref, wbf_ref, halo_ref, carry_ref = refs[5:]
    else:
        prev2_ref, prev1_ref, act_ref, last2_ref, last1_ref, tail_ref, wbf_ref, tab2_ref, tab1_ref, gt_ref = refs[5:]
    j, i = pl.program_id(0), pl.program_id(1)

    @pl.when(i == 0)
    def _():
        _cast_into(wbf_ref, 0, 0, wa_ref)
        _cast_into(wbf_ref, 0, fw, wg_ref)

    if lay.mode == "carry":
        _load_halo(halo_ref, init_ref, carry_ref, i, tiles_per_seq)
        conv = _CausalConv(cwt_ref, lay, halo_ref[...])
    else:
        _fill_tables(lay, tab2_ref, tab1_ref, prev2_ref, prev1_ref)
        conv = _CausalConv(cwt_ref, lay, jnp.zeros((HALO, fw), F32), tab2_ref, tab1_ref)

    def matmul(r0, rs):
        return _dot(n_ref[r0:r0 + rs, :], wbf_ref[...])

    def epilogue(r0, rs, p):
        gt = p[:, fw:]
        if lay.mode == "table":
            _lane_chunks_store(gt_ref, r0, gt)
        z = conv(gt, r0) + b_ref[...]
        act_ref[r0:r0 + rs, :] = (_silu(z) * p[:, :fw]).astype(BF16)

    _sliced(tm, matmul, epilogue)
    if lay.mode == "carry":
        _store_tail(conv.tail, carry_ref, st_ref, i, j, tiles_per_seq)
    else:
        _store_short_states(lay, gt_ref, conv.tail, last2_ref, last1_ref, tail_ref)


def _ffn_up(n2, w_up, conv_w, bias, lay, tm, fw, init=None, prev2=None, prev1=None):
    rows, d = n2.shape
    dff = conv_w.shape[1]
    nj = dff // fw
    tiles_per_seq = (lay.seq_rows // tm) if lay.mode == "carry" else 1
    xargs, xspecs, xout_specs, xout_shapes, xscratch = _conv_specs(lay, rows, tm, fw, nj, init, prev2, prev1)
    wspec = lambda part: pl.BlockSpec((None, d, fw), lambda j, i: (0, 0, part * nj + j))
    return pl.pallas_call(
        functools.partial(_ffn_up_kernel, lay, tm, fw, tiles_per_seq),
        grid=(nj, rows // tm),
        in_specs=[pl.BlockSpec((tm, d), lambda j, i: (i, 0)), wspec(0), wspec(1),
                  pl.BlockSpec((CONV_W, fw), lambda j, i: (0, j)), pl.BlockSpec((1, fw), lambda j, i: (0, j))]
                 + xspecs,
        out_specs=[pl.BlockSpec((tm, fw), lambda j, i: (i, j))] + xout_specs,
        out_shape=[jax.ShapeDtypeStruct((rows, dff), BF16)] + xout_shapes,
        scratch_shapes=[pltpu.VMEM((d, 2 * fw), BF16)] + xscratch,
        compiler_params=_cparams("arbitrary", "arbitrary"),
        name="ffn_up_" + lay.mode)(n2, w_up, w_up, conv_w, bias, *xargs)


def _ffn_down_kernel(act_ref, w_ref, h_ref, g_ref, y_ref):
    y_ref[...] = _rms(h_ref[...] + _dot(act_ref[...], w_ref[...]), g_ref[...])


def _ffn_down(act, w, h, g, rows, tm):
    dff = act.shape[1]
    d = w.shape[1]
    row = lambda i: (i, 0)
    return pl.pallas_call(
        _ffn_down_kernel, grid=(rows // tm,),
        in_specs=[pl.BlockSpec((tm, dff), row),
                  pl.BlockSpec((dff, d), lambda i: (0, 0), pipeline_mode=pl.Buffered(1)),
                  pl.BlockSpec((tm, d), row), pl.BlockSpec((1, d), lambda i: (0, 0))],
        out_specs=pl.BlockSpec((tm, d), row),
        out_shape=jax.ShapeDtypeStruct((rows, d), F32),
        compiler_params=_cparams("arbitrary"), name="ffn_down")(act, w, h, g)


def kernel(x_prompt, x_sample, state_conv, state_gla, state_ffn_conv, meta_tokens, norm_mix_g, w_in, conv_mix_w, w_conv_out, w_gate_up, b_gate, gla_norm_g, w_gla_out, w_o, norm_ffn_g, w_ffn_up, ffn_conv_w, ffn_conv_b, w_ffn_down, final_norm_g):
    bp, seq, d = x_prompt.shape
    bs, sl, _ = x_sample.shape
    assert w_in.shape[0] == 1, "single-layer step"
    n_meta = meta_tokens.shape[0]
    dc = state_conv.shape[-1]
    _, _, heads, dk, dv = state_gla.shape
    dff = state_ffn_conv.shape[-1]
    rank = w_gate_up.shape[1]
    assert n_meta % GLA_SUB == 0 and seq % GLA_SUB == 0 and GLA_SUB % sl == 0 and sl >= CONV_W - 1

    o_q = 3 * dc
    o_a = o_q + 2 * heads * dk + 2 * heads * dv
    o_ga = o_a + rank
    tn = heads * dk
    assert o_q % tn == 0 and (heads * dv) % tn == 0 and d % tn == 0
    cw, fw = CONV_COLS, FFN_COLS
    tiles_qkvg = ((2 * heads * dk + heads * dv) // tn, heads * dv // tn, 0)
    tiles_gates = (0, 0, 2 * d // tn)
    spec_qkvg = pl.BlockSpec((None, tn, d), lambda j, i: (0, o_q // tn + j, 0))
    spec_gates = pl.BlockSpec((pl.Element(1), pl.Element(tn), pl.Element(d)), lambda j, i: (0, (o_ga // 8 + j * (tn // 8)) * 8, 0))
    assert o_ga % 8 == 0
    q_scale = float(dk) ** -0.5

    w_in_t = jnp.swapaxes(w_in, 1, 2)
    wa = jnp.pad(w_in_t[0, o_a:o_ga], ((0, LANES - rank), (0, 0))).astype(BF16)
    wup = jnp.pad(w_gate_up[0], ((0, LANES - rank), (0, 0))).astype(BF16)
    wc = w_conv_out[0].astype(BF16)
    wg = w_gla_out[0].astype(BF16)
    wo = w_o[0].astype(BF16)
    wd = w_ffn_down[0].astype(BF16)
    g1, g2, gf, gn = norm_mix_g[0][None], norm_ffn_g[0][None], final_norm_g[None], gla_norm_g[0][None]
    bg, fb, cmw, fcw = b_gate[0][None], ffn_conv_b[0][None], conv_mix_w[0], ffn_conv_w[0]

    def in_projections(n, tm, lay, **conv_kw):
        conv_outs = _inproj_conv(n, w_in_t, cmw, lay, tm, cw, **conv_kw)
        lin = _inproj_act(n, w_in_t, spec_qkvg, tiles_qkvg, tm, tn, q_scale)
        sig = _inproj_act(n, w_in_t, spec_gates, tiles_gates, tm, tn, 1.0)
        return conv_outs, lin, sig

    n_short = bs * sl
    rows_s = n_short + n_meta
    xs = jnp.concatenate([x_sample.reshape(n_short, d), meta_tokens.astype(x_sample.dtype)], axis=0)
    lay_s = SeqLayout("table", n_short=n_short, short_len=sl)
    tm_s = rows_s
    n_s, logg_s = _rmsnorm_gate(xs, g1, wa, wup, bg, _row_tile(rows_s, ROW_TILE_RESIDENT))
    (ca_s, conv_s2, conv_s1, conv_tail), lin_s, sig_s = in_projections(
        n_s, tm_s, lay_s, prev2=state_conv[0, :, 0], prev1=state_conv[0, :, 1])
    s_zero = jnp.zeros((1, heads, dk, dv), F32)
    ob_m, s_meta = _gla_seq(lin_s, logg_s, gn, s_zero, heads, dk, dv, 1, n_meta, n_meta, GLA_SUB,
                            n_short, True)

    rows_p = bp * seq
    xp = x_prompt.reshape(rows_p, d)
    lay_p = SeqLayout("carry", seq_rows=seq)
    tm_p = _row_tile(seq, ROW_TILE)
    tm_o = _row_tile(seq, ROW_TILE_RESIDENT)
    n_p, logg_p = _rmsnorm_gate(xp, g1, wa, wup, bg, tm_o)
    (ca_p, conv_p), lin_p, sig_p = in_projections(n_p, tm_p, lay_p, init=conv_tail)
    ob_p, s_p, ob_smp, s_smp = _gla_both(lin_p, logg_p, s_meta, bp, seq, _row_tile(seq, GLA_CHUNK), GLA_SUB,
                                         lin_s, logg_s, state_gla[0], bs, sl, SHORT_SEQS_PER_STEP,
                                         gn, heads, dk, dv)

    ob_s = jnp.concatenate([ob_smp, ob_m], axis=0)
    m_s = _merge(ca_s, ob_s, sig_s, wc, wg, tm_s, tn)
    h_s, n2_s = _oproj(m_s, xs, wo, g2, _row_tile(rows_s, ROW_TILE_RESIDENT))
    act_s, ffn_s2, ffn_s1, ffn_tail = _ffn_up(n2_s, w_ffn_up, fcw, fb, lay_s, tm_s, fw,
                                              prev2=state_ffn_conv[0, :, 0], prev1=state_ffn_conv[0, :, 1])
    y_s = _ffn_down(act_s, wd, h_s, gf, n_short, _row_tile(n_short, ROW_TILE_WIDE_K))

    m_p = _merge(ca_p, ob_p, sig_p, wc, wg, _row_tile(seq, ROW_TILE_MERGE), tn)
    h_p, n2_p = _oproj(m_p, xp, wo, g2, tm_o)
    act_p, ffn_p = _ffn_up(n2_p, w_ffn_up, fcw, fb, lay_p, tm_p, fw, init=ffn_tail)
    y_p = _ffn_down(act_p, wd, h_p, gf, rows_p, _row_tile(seq, ROW_TILE_WIDE_K))

    untile = lambda st: jnp.swapaxes(st, 1, 2).reshape(st.shape[0], CONV_W - 1, -1)
    return (y_p.reshape(bp, seq, d), y_s.reshape(bs, sl, d),
            untile(conv_p)[None], s_p[None], untile(ffn_p)[None],
            jnp.stack([conv_s2, conv_s1], axis=1)[None], s_smp[None], jnp.stack([ffn_s2, ffn_s1], axis=1)[None])
```

```python
import functools

import jax
import jax.numpy as jnp
from jax import lax
from jax.experimental import pallas as pl
from jax.experimental.pallas import tpu as pltpu

EPS = 1e-6
GATE_TAU = 16.0
CONV_W = 3
GLA_SUB = 16
HALO = 8
V7X_VMEM_LIMIT = 56 * 1024 * 1024
LANES = 128
ROW_TILE = 2048
ROW_TILE_RESIDENT = 512
ROW_TILE_MERGE = 1024
ROW_TILE_NORM = 1024
ROW_TILE_WIDE_K = 256
GLA_CHUNK = 128
CONV_COLS = 256
FFN_COLS = 512
SHORT_SEQS_PER_STEP = 8
SUB_ROWS = 256
CAST_ROWS = 64
F32 = jnp.float32
BF16 = jnp.bfloat16


def _cparams(*sem):
    return pltpu.CompilerParams(dimension_semantics=sem, vmem_limit_bytes=V7X_VMEM_LIMIT)


def _row_tile(rows, target):
    best = None
    for t in range(16, min(rows, target) + 1, 16):
        if rows % t == 0:
            best = t
    return best or rows


def _rms(x, g):
    return x * lax.rsqrt(jnp.mean(x * x, axis=-1, keepdims=True) + EPS) * g


def _silu(x):
    h = 0.5 * x
    return h * jnp.tanh(h) + h


def _dot(a, b):
    return jnp.dot(a, b, preferred_element_type=F32)


def _dot_nt(a, b):
    return lax.dot_general(a, b, (((1,), (1,)), ((), ())), preferred_element_type=F32)


def _sliced(tm, matmul, epilogue):
    n = max(1, tm // SUB_ROWS)
    sizes = [SUB_ROWS] * (n - 1) + [tm - SUB_ROWS * (n - 1)]
    r0 = 0
    for rs in sizes:
        epilogue(r0, rs, matmul(r0, rs))
        r0 += rs


def _cast_into(dst_ref, row0, col0, src_ref):
    rows, cols = src_ref.shape

    def body(r, carry):
        off = pl.multiple_of(r * CAST_ROWS, CAST_ROWS)
        dst_ref[pl.ds(row0 + off, CAST_ROWS), col0:col0 + cols] = src_ref[pl.ds(off, CAST_ROWS), :].astype(BF16)
        return carry

    lax.fori_loop(0, rows // CAST_ROWS, body, 0)


def _rmsnorm_gate_kernel(x_ref, g_ref, wa_ref, wup_ref, b_ref, n_ref, lg_ref):
    n = _rms(x_ref[...], g_ref[...]).astype(BF16)
    n_ref[...] = n
    a = _dot_nt(n, wa_ref[...]).astype(BF16)
    z = _dot(a, wup_ref[...]) + b_ref[...]
    lg_ref[...] = (jnp.minimum(z, 0.0) - jnp.log(1.0 + jnp.exp(-jnp.abs(z)))) * (1.0 / GATE_TAU)


def _rmsnorm_gate(x, g, wa, wup, b, tm):
    rows, d = x.shape
    rp, dk = wup.shape
    const = lambda i: (0, 0)
    return pl.pallas_call(
        _rmsnorm_gate_kernel, grid=(rows // tm,),
        in_specs=[pl.BlockSpec((tm, d), lambda i: (i, 0)), pl.BlockSpec((1, d), const),
                  pl.BlockSpec((rp, d), const), pl.BlockSpec((rp, dk), const), pl.BlockSpec((1, dk), const)],
        out_specs=[pl.BlockSpec((tm, d), lambda i: (i, 0)), pl.BlockSpec((tm, dk), lambda i: (i, 0))],
        out_shape=[jax.ShapeDtypeStruct((rows, d), BF16), jax.ShapeDtypeStruct((rows, dk), F32)],
        compiler_params=_cparams("arbitrary"), name="rmsnorm_gate")(x, g, wa, wup, b)


class SeqLayout:
    def __init__(self, mode, seq_rows=None, n_short=None, short_len=None):
        self.mode, self.seq_rows, self.n_short, self.short_len = mode, seq_rows, n_short, short_len


def _load_halo(halo_ref, init_ref, carry_ref, i, tiles_per_seq):
    first = (i % tiles_per_seq) == 0

    @pl.when(first)
    def _():
        halo_ref[...] = init_ref[...]

    @pl.when(jnp.logical_not(first))
    def _():
        halo_ref[...] = carry_ref[...]


def _store_tail(tail, carry_ref, st_ref, i, j, tiles_per_seq):
    carry_ref[...] = tail

    @pl.when((i % tiles_per_seq) == tiles_per_seq - 1)
    def _():
        st_ref[i // tiles_per_seq, j] = tail[HALO - (CONV_W - 1):, :]


def _shift_rows(x, prev_row):
    rows, c = x.shape
    prev8 = jnp.broadcast_to(prev_row, (HALO, c))
    above = jnp.concatenate([prev8, x[:rows - HALO]], axis=0) if rows > HALO else prev8
    last = lax.broadcasted_iota(jnp.int32, (rows, 1), 0) % HALO == HALO - 1
    mixed = jnp.where(last, above, x).reshape(rows // HALO, HALO, c)
    return pltpu.roll(mixed, 1, 1).reshape(rows, c)


class _CausalConv:
    def __init__(self, w_ref, lay, halo, prev2_ref=None, prev1_ref=None):
        w = w_ref[...]
        self.w0, self.w1, self.w2 = w[0:1, :], w[1:2, :], w[2:3, :]
        self.lay, self.tail, self.prev2_ref, self.prev1_ref = lay, halo, prev2_ref, prev1_ref

    def __call__(self, u, r0):
        rs, c = u.shape
        lay, w0, w1 = self.lay, self.w0, self.w1
        um2, um1 = self.tail[HALO - 2:HALO - 1, :], self.tail[HALO - 1:HALO, :]
        s0 = _shift_rows(w0 * u, w0 * um1)
        s1_first = w1 * um1 + w0 * um2
        if lay.mode == "table":
            take = max(0, min(lay.n_short, r0 + rs) - r0)

            def table_rows(ref):
                parts = ([_lane_chunks_load(ref, r0, take)] if take else []) + (
                    [jnp.zeros((rs - take, c), F32)] if take < rs else [])
                return parts[0] if len(parts) == 1 else jnp.concatenate(parts, axis=0)

            p2, p1 = table_rows(self.prev2_ref), table_rows(self.prev1_ref)
            t = r0 + lax.broadcasted_iota(jnp.int32, (rs, 1), 0)
            starts = jnp.where(t < lay.n_short, t % lay.short_len, t - lay.n_short) == 0
            s0 = jnp.where(starts, w0 * p1, s0)
        a = w1 * u + s0
        s1 = _shift_rows(a, s1_first)
        if lay.mode == "table":
            s1 = jnp.where(starts, w1 * p1 + w0 * p2, s1)
        self.tail = u[rs - HALO:, :]
        return self.w2 * u + s1


def _conv_specs(lay, rows, tm, c, nj, init, prev):
    col = lambda j, i: (0, j)
    if lay.mode == "carry":
        nseq = rows // lay.seq_rows
        return ((init,), [pl.BlockSpec((HALO, c), col)],
                [pl.BlockSpec((nseq, nj, CONV_W - 1, c), lambda j, i: (0, 0, 0, 0))],
                [jax.ShapeDtypeStruct((nseq, nj, CONV_W - 1, c), F32)],
                [pltpu.VMEM((HALO, c), F32)] * 2)
    assert tm == rows and lay.short_len >= CONV_W - 1
    nseq = lay.n_short // lay.short_len
    return ((prev, prev), [pl.BlockSpec((nseq, c), col), pl.BlockSpec((nseq, c), lambda j, i: (0, nj + j))],
            [pl.BlockSpec((nseq, c), col)] * 2 + [pl.BlockSpec((HALO, c), col)],
            [jax.ShapeDtypeStruct((nseq, nj * c), F32)] * 2 + [jax.ShapeDtypeStruct((HALO, nj * c), F32)],
            [pltpu.VMEM((c // LANES, lay.n_short, LANES), F32)] * 2 + [pltpu.VMEM((c // LANES, tm, LANES), F32)])


def _lane_chunks_load(ref, r0, rows):
    return jnp.concatenate([ref[k, r0:r0 + rows, :] for k in range(ref.shape[0])], axis=1)


def _lane_chunks_store(ref, r0, x):
    for k in range(ref.shape[0]):
        ref[k, r0:r0 + x.shape[0], :] = x[:, k * LANES:(k + 1) * LANES]


def _fill_tables(lay, tab2_ref, tab1_ref, prev2_ref, prev1_ref):
    nseq = lay.n_short // lay.short_len
    first_rows = pl.ds(0, nseq, stride=lay.short_len)
    for tab_ref, prev_ref in ((tab2_ref, prev2_ref), (tab1_ref, prev1_ref)):
        tab_ref[...] = jnp.zeros(tab_ref.shape, F32)
        for k in range(tab_ref.shape[0]):
            tab_ref[k, first_rows, :] = prev_ref[:, k * LANES:(k + 1) * LANES]


def _store_short_states(lay, rows_ref, tail, last2_ref, last1_ref, tail_ref):
    nseq = lay.n_short // lay.short_len
    for k in range(rows_ref.shape[0]):
        cols = slice(k * LANES, (k + 1) * LANES)
        last2_ref[:, cols] = rows_ref[k, pl.ds(lay.short_len - 2, nseq, stride=lay.short_len), :]
        last1_ref[:, cols] = rows_ref[k, pl.ds(lay.short_len - 1, nseq, stride=lay.short_len), :]
    tail_ref[...] = tail


def _inproj_conv_kernel(lay, tm, cw, tiles_per_seq, *refs):
    n_ref, wb_ref, wc_ref, wh_ref, cwt_ref = refs[:5]
    if lay.mode == "carry":
        init_ref, ca_ref, st_ref, wbf_ref, halo_ref, carry_ref = refs[5:]
    else:
        prev2_ref, prev1_ref, ca_ref, last2_ref, last1_ref, tail_ref, wbf_ref, tab2_ref, tab1_ref, u_ref = refs[5:]
    j, i = pl.program_id(0), pl.program_id(1)

    @pl.when(i == 0)
    def _():
        for part, w_ref in enumerate((wb_ref, wc_ref, wh_ref)):
            _cast_into(wbf_ref, part * cw, 0, w_ref)

    if lay.mode == "carry":
        _load_halo(halo_ref, init_ref, carry_ref, i, tiles_per_seq)
        conv = _CausalConv(cwt_ref, lay, halo_ref[...])
    else:
        _fill_tables(lay, tab2_ref, tab1_ref, prev2_ref, prev1_ref)
        conv = _CausalConv(cwt_ref, lay, jnp.zeros((HALO, cw), F32), tab2_ref, tab1_ref)

    def matmul(r0, rs):
        return _dot_nt(n_ref[r0:r0 + rs, :], wbf_ref[...])

    def epilogue(r0, rs, p):
        u = p[:, cw:2 * cw] * p[:, 2 * cw:]
        if lay.mode == "table":
            _lane_chunks_store(u_ref, r0, u)
        ca_ref[r0:r0 + rs, :] = (p[:, :cw] * conv(u, r0)).astype(BF16)

    _sliced(tm, matmul, epilogue)
    if lay.mode == "carry":
        _store_tail(conv.tail, carry_ref, st_ref, i, j, tiles_per_seq)
    else:
        _store_short_states(lay, u_ref, conv.tail, last2_ref, last1_ref, tail_ref)


def _inproj_conv(n, w_in_t, conv_w, lay, tm, cw, init=None, prev=None):
    rows, d = n.shape
    dc = conv_w.shape[1]
    nj = dc // cw
    tiles_per_seq = (lay.seq_rows // tm) if lay.mode == "carry" else 1
    xargs, xspecs, xout_specs, xout_shapes, xscratch = _conv_specs(lay, rows, tm, cw, nj, init, prev)
    wspec = lambda part: pl.BlockSpec((None, cw, d), lambda j, i: (0, part * nj + j, 0))
    return pl.pallas_call(
        functools.partial(_inproj_conv_kernel, lay, tm, cw, tiles_per_seq),
        grid=(nj, rows // tm),
        in_specs=[pl.BlockSpec((tm, d), lambda j, i: (i, 0)), wspec(0), wspec(1), wspec(2),
                  pl.BlockSpec((CONV_W, cw), lambda j, i: (0, j))] + xspecs,
        out_specs=[pl.BlockSpec((tm, cw), lambda j, i: (i, j))] + xout_specs,
        out_shape=[jax.ShapeDtypeStruct((rows, dc), BF16)] + xout_shapes,
        scratch_shapes=[pltpu.VMEM((3 * cw, d), BF16)] + xscratch,
        compiler_params=_cparams("arbitrary", "arbitrary"),
        name="inproj_conv_" + lay.mode)(n, w_in_t, w_in_t, w_in_t, conv_w, *xargs)


def _inproj_act_kernel(tm, n_lin, n_silu, q_scale, n_ref, w_ref, o_ref, wbf_ref):
    j, i = pl.program_id(0), pl.program_id(1)

    @pl.when(i == 0)
    def _():
        _cast_into(wbf_ref, 0, 0, w_ref.at[0] if len(w_ref.shape) == 3 else w_ref)

    def matmul(r0, rs):
        return _dot_nt(n_ref[r0:r0 + rs, :], wbf_ref[...])

    def epilogue(r0, rs, p):
        if n_lin:
            act = p * jnp.where(j == 0, q_scale, 1.0)
        else:
            h = 0.5 * p
            factor = jnp.where(j < n_silu, h, 0.5) if n_silu else 0.5
            act = factor * (jnp.tanh(h) + 1.0)
        o_ref[r0:r0 + rs, :] = act.astype(BF16)

    _sliced(tm, matmul, epilogue)


def _inproj_act(n, w_t, w_spec, tiles, tm, tn, q_scale):
    rows, d = n.shape
    n_lin, n_silu, n_sig = tiles
    assert not (n_lin and (n_silu or n_sig))
    n_col = n_lin + n_silu + n_sig
    return pl.pallas_call(
        functools.partial(_inproj_act_kernel, tm, n_lin, n_silu, q_scale),
        grid=(n_col, rows // tm),
        in_specs=[pl.BlockSpec((tm, d), lambda j, i: (i, 0)), w_spec],
        out_specs=pl.BlockSpec((tm, tn), lambda j, i: (i, j)),
        out_shape=jax.ShapeDtypeStruct((rows, n_col * tn), BF16),
        scratch_shapes=[pltpu.VMEM((tn, d), BF16)],
        compiler_params=_cparams("arbitrary", "arbitrary"), name="inproj_act")(n, w_t)


def _cumsum_groups(x, sub):
    rows = x.shape[0]
    pos = lax.broadcasted_iota(jnp.int32, (rows, 1), 0) % sub
    s = 1
    while s < sub:
        x = x + jnp.where(pos >= s, pltpu.roll(x, s, 0), 0.0)
        s *= 2
    return x


def _gla_chunk(q, k, v, lg, s, sub):
    c, dk = q.shape
    ngrp = c // sub
    gpu = 2 if ngrp % 2 == 0 else 1
    unit = gpu * sub
    bt = _cumsum_groups(lg, sub)
    r = jnp.zeros((1, dk), F32)
    r_grp, b_rows = [], []
    for g in range(ngrp):
        r_grp.append(r)
        b_rows.append(bt[g * sub:(g + 1) * sub, :] + r)
        r = r + bt[(g + 1) * sub - 1:(g + 1) * sub, :]
    b = jnp.concatenate(b_rows, axis=0) if ngrp > 1 else b_rows[0]
    b_last = r
    qe = (q * jnp.exp(b)).astype(BF16)
    kd = (k * jnp.exp(b_last - b)).astype(BF16)
    att_rows = []
    for i in range(c // unit):
        ref = r_grp[i * gpu + gpu - 1]
        rows = slice(i * unit, (i + 1) * unit)
        seen = (i + 1) * unit
        qt = (q[rows] * jnp.exp(b[rows] - ref)).astype(BF16)
        ke = (k[:seen] * jnp.exp(ref - b[:seen])).astype(BF16)
        if seen < c:
            ke = jnp.concatenate([ke, jnp.zeros((c - seen, dk), BF16)], axis=0)
        a = _dot_nt(qt, ke)
        col = lax.broadcasted_iota(jnp.int32, (unit, c), 1)
        rloc = lax.broadcasted_iota(jnp.int32, (unit, c), 0)
        att_rows.append(jnp.where(col <= rloc + i * unit, a, 0.0))
    att = (jnp.concatenate(att_rows, axis=0) if len(att_rows) > 1 else att_rows[0]).astype(BF16)
    o = _dot(att, v) + _dot(qe, s.astype(BF16))
    upd = lax.dot_general(kd, v, (((0,), (0,)), ((), ())), preferred_element_type=F32)
    dl = jnp.exp(b_last)
    dl_col = jnp.transpose(jnp.broadcast_to(dl, (LANES, dk)))
    dv = s.shape[1]
    s_dec = jnp.concatenate([s[:, n * LANES:(n + 1) * LANES] * dl_col for n in range(dv // LANES)], axis=1)
    return o, s_dec + upd


def _gla_out(o, gn, gs):
    return (_rms(o, gn) * gs.astype(F32)).astype(BF16)


def _gla_seq_kernel(sub, heads, dk, dv, *refs):
    _gla_seq_body(sub, heads, dk, dv, pl.program_id(1) == 0, *refs)


def _gla_seq_body(sub, heads, dk, dv, first_chunk, q_ref, k_ref, v_ref, lg_ref, gs_ref, gn_ref, s0_ref, ob_ref, s_ref):
    @pl.when(first_chunk)
    def _():
        def copy_head(h, carry):
            s_ref[0, h] = s0_ref[0, h]
            return carry

        lax.fori_loop(0, heads, copy_head, 0)

    for h in range(heads):
        ck, cv = slice(h * dk, (h + 1) * dk), slice(h * dv, (h + 1) * dv)
        o, s_new = _gla_chunk(q_ref[:, ck].astype(F32), k_ref[:, ck].astype(F32), v_ref[:, cv], lg_ref[:, ck],
                              s_ref[0, h], sub)
        s_ref[0, h] = s_new
        ob_ref[:, cv] = _gla_out(o, gn_ref[...], gs_ref[:, cv])


def _gla_seq(lin, act, logg, gn, s0, heads, dk, dv, nseq, seq_rows, chunk, sub, row0, shared_init):
    nchunk = seq_rows // chunk
    blk0 = row0 // chunk
    rb = lambda b, c: blk0 + b * nchunk + c
    wk, wv = heads * dk, heads * dv
    assert (2 * wk) % wv == 0
    return pl.pallas_call(
        functools.partial(_gla_seq_kernel, sub, heads, dk, dv),
        grid=(nseq, nchunk),
        in_specs=[pl.BlockSpec((chunk, wk), lambda b, c: (rb(b, c), 0)),
                  pl.BlockSpec((chunk, wk), lambda b, c: (rb(b, c), 1)),
                  pl.BlockSpec((chunk, wv), lambda b, c: (rb(b, c), 2 * wk // wv)),
                  pl.BlockSpec((chunk, wk), lambda b, c: (rb(b, c), 0)),
                  pl.BlockSpec((chunk, wv), lambda b, c: (rb(b, c), 0)),
                  pl.BlockSpec((1, dv), lambda b, c: (0, 0)),
                  pl.BlockSpec((1, heads, dk, dv), lambda b, c: (0 if shared_init else b, 0, 0, 0))],
        out_specs=[pl.BlockSpec((chunk, wv), lambda b, c: (b * nchunk + c, 0)),
                   pl.BlockSpec((1, heads, dk, dv), lambda b, c: (b, 0, 0, 0))],
        out_shape=[jax.ShapeDtypeStruct((nseq * seq_rows, wv), BF16),
                   jax.ShapeDtypeStruct((nseq, heads, dk, dv), F32)],
        compiler_params=_cparams("arbitrary", "arbitrary"),
        name="gla_seq")(lin, lin, lin, logg, act, gn, s0)


def _gla_short_kernel(nb, sl, q_ref, k_ref, v_ref, lg_ref, gs_ref, gn_ref, s0_ref, ob_ref, s_ref):
    q = q_ref[...].astype(F32)
    k = k_ref[...].astype(F32)
    v = v_ref[...].astype(F32)
    lg = lg_ref[...]
    outs = []
    for n in range(nb):
        rs = slice(n * sl, (n + 1) * sl)
        o, s_new = _gla_chunk(q[rs], k[rs], v[rs].astype(BF16), lg[rs], s0_ref[n, 0], sl)
        s_ref[n, 0] = s_new
        outs.append(o)
    o = jnp.concatenate(outs, axis=0)
    ob_ref[...] = _gla_out(o, gn_ref[...], gs_ref[...])


def _gla_both_kernel(sub, heads, dk, dv, nchunk, nb, sl, n_long, n_short, *refs):
    (ql, kl, vl, lgl, gsl, gn, s0l, qs, ks, vs, lgs, gss, s0s, obl, sl_out, obs, ss_out) = refs
    step = pl.program_id(0)

    def long_part():
        _gla_seq_body(sub, heads, dk, dv, (step % nchunk) == 0, ql, kl, vl, lgl, gsl, gn, s0l, obl, sl_out)

    def short_part():
        _gla_short_kernel(nb, sl, qs, ks, vs, lgs, gss, gn, s0s, obs, ss_out)

    if n_long == n_short:
        long_part()
        short_part()
    else:
        pl.when(step < n_long)(long_part)
        pl.when(step < n_short)(short_part)


def _gla_both(lin_l, act_l, logg_l, s0_l, nseq_l, seq_rows, chunk, sub,
              lin_s, act_s, logg_s, s0_s, nseq_s, sl, nb, gn, heads, dk, dv):
    nchunk = seq_rows // chunk
    n_long, n_short = nseq_l * nchunk, (nseq_s // nb) * heads
    steps = max(n_long, n_short)
    wk, wv = heads * dk, heads * dv
    assert (2 * wk) % wv == 0
    rows = nb * sl
    kv = (2 * heads * dk) // dv
    lo = (lambda s: s) if n_long == steps else (lambda s: jnp.minimum(s, n_long - 1))
    sh = (lambda s: s) if n_short == steps else (lambda s: jnp.minimum(s, n_short - 1))
    long_row = lambda col: (lambda s: (lo(s), col))
    short_blk = lambda off: (lambda s: (sh(s) // heads, off + sh(s) % heads))
    return pl.pallas_call(
        functools.partial(_gla_both_kernel, sub, heads, dk, dv, nchunk, nb, sl, n_long, n_short),
        grid=(steps,),
        in_specs=[pl.BlockSpec((chunk, wk), long_row(0)), pl.BlockSpec((chunk, wk), long_row(1)),
                  pl.BlockSpec((chunk, wv), long_row(2 * wk // wv)), pl.BlockSpec((chunk, wk), long_row(0)),
                  pl.BlockSpec((chunk, wv), long_row(0)),
                  pl.BlockSpec((1, dv), lambda s: (0, 0)),
                  pl.BlockSpec((1, heads, dk, dv), lambda s: (0, 0, 0, 0)),
                  pl.BlockSpec((rows, dk), short_blk(0)), pl.BlockSpec((rows, dk), short_blk(heads)),
                  pl.BlockSpec((rows, dv), short_blk(kv)), pl.BlockSpec((rows, dk), short_blk(0)),
                  pl.BlockSpec((rows, dv), short_blk(0)),
                  pl.BlockSpec((nb, 1, dk, dv), lambda s: (sh(s) // heads, sh(s) % heads, 0, 0))],
        out_specs=[pl.BlockSpec((chunk, wv), long_row(0)),
                   pl.BlockSpec((1, heads, dk, dv), lambda s: (lo(s) // nchunk, 0, 0, 0)),
                   pl.BlockSpec((rows, dv), short_blk(0)),
                   pl.BlockSpec((nb, 1, dk, dv), lambda s: (sh(s) // heads, sh(s) % heads, 0, 0))],
        out_shape=[jax.ShapeDtypeStruct((nseq_l * seq_rows, wv), BF16),
                   jax.ShapeDtypeStruct((nseq_l, heads, dk, dv), F32),
                   jax.ShapeDtypeStruct((nseq_s * sl, wv), BF16),
                   jax.ShapeDtypeStruct((nseq_s, heads, dk, dv), F32)],
        compiler_params=_cparams("arbitrary"),
        name="gla_both")(lin_l, lin_l, lin_l, logg_l, act_l, gn, s0_l, lin_s, lin_s, lin_s, logg_s, act_s, s0_s)


def _merge_kernel(ca_ref, ob_ref, ga_ref, gb_ref, wc_ref, wg_ref, o_ref):
    ya = _dot(ca_ref[...], wc_ref[...])
    yb = _dot(ob_ref[...], wg_ref[...])
    o_ref[...] = (ga_ref[...].astype(F32) * ya + gb_ref[...].astype(F32) * yb).astype(BF16)


def _merge(ca, ob, act, wc, wg, tm, tn):
    rows, dc = ca.shape
    dg, d = wg.shape
    nn = d // tn
    g0 = (act.shape[1] - 2 * d) // tn
    return pl.pallas_call(
        _merge_kernel, grid=(rows // tm, nn),
        in_specs=[pl.BlockSpec((tm, dc), lambda i, j: (i, 0)), pl.BlockSpec((tm, dg), lambda i, j: (i, 0)),
                  pl.BlockSpec((tm, tn), lambda i, j: (i, g0 + j)),
                  pl.BlockSpec((tm, tn), lambda i, j: (i, g0 + nn + j)),
                  pl.BlockSpec((dc, tn), lambda i, j: (0, j)), pl.BlockSpec((dg, tn), lambda i, j: (0, j))],
        out_specs=pl.BlockSpec((tm, tn), lambda i, j: (i, j)),
        out_shape=jax.ShapeDtypeStruct((rows, d), BF16),
        compiler_params=_cparams("arbitrary", "arbitrary"), name="merge")(ca, ob, act, act, wc, wg)


def _oproj_kernel(m_ref, x_ref, w_ref, g_ref, h_ref, n2_ref):
    h = x_ref[...] + _dot(m_ref[...], w_ref[...])
    h_ref[...] = h
    n2_ref[...] = _rms(h, g_ref[...]).astype(BF16)


def _oproj(m, x, w, g, tm):
    rows, d = x.shape
    row = lambda i: (i, 0)
    return pl.pallas_call(
        _oproj_kernel, grid=(rows // tm,),
        in_specs=[pl.BlockSpec((tm, d), row), pl.BlockSpec((tm, d), row),
                  pl.BlockSpec((d, d), lambda i: (0, 0)), pl.BlockSpec((1, d), lambda i: (0, 0))],
        out_specs=[pl.BlockSpec((tm, d), row), pl.BlockSpec((tm, d), row)],
        out_shape=[jax.ShapeDtypeStruct((rows, d), F32), jax.ShapeDtypeStruct((rows, d), BF16)],
        compiler_params=_cparams("arbitrary"), name="oproj")(m, x, w, g)


def _ffn_up_kernel(lay, tm, fw, tiles_per_seq, *refs):
    n_ref, wa_ref, wg_ref, cwt_ref, b_ref = refs[:5]
    if lay.mode == "carry":
        init_ref, act_ref, st_ref, wbf_ref, halo_ref, carry_ref = refs[5:]
    else:
        prev2_ref, prev1_ref, act_ref, last2_ref, last1_ref, tail_ref, wbf_ref, tab2_ref, tab1_ref, gt_ref = refs[5:]
    j, i = pl.program_id(0), pl.program_id(1)

    @pl.when(i == 0)
    def _():
        _cast_into(wbf_ref, 0, 0, wa_ref)
        _cast_into(wbf_ref, 0, fw, wg_ref)

    if lay.mode == "carry":
        _load_halo(halo_ref, init_ref, carry_ref, i, tiles_per_seq)
        conv = _CausalConv(cwt_ref, lay, halo_ref[...])
    else:
        _fill_tables(lay, tab2_ref, tab1_ref, prev2_ref, prev1_ref)
        conv = _CausalConv(cwt_ref, lay, jnp.zeros((HALO, fw), F32), tab2_ref, tab1_ref)

    def matmul(r0, rs):
        return _dot(n_ref[r0:r0 + rs, :], wbf_ref[...])

    def epilogue(r0, rs, p):
        gt = p[:, fw:]
        if lay.mode == "table":
            _lane_chunks_store(gt_ref, r0, gt)
        z = conv(gt, r0) + b_ref[...]
        act_ref[r0:r0 + rs, :] = (_silu(z) * p[:, :fw]).astype(BF16)

    _sliced(tm, matmul, epilogue)
    if lay.mode == "carry":
        _store_tail(conv.tail, carry_ref, st_ref, i, j, tiles_per_seq)
    else:
        _store_short_states(lay, gt_ref, conv.tail, last2_ref, last1_ref, tail_ref)


def _ffn_up(n2, w_up, conv_w, bias, lay, tm, fw, init=None, prev=None):
    rows, d = n2.shape
    dff = conv_w.shape[1]
    nj = dff // fw
    tiles_per_seq = (lay.seq_rows // tm) if lay.mode == "carry" else 1
    xargs, xspecs, xout_specs, xout_shapes, xscratch = _conv_specs(lay, rows, tm, fw, nj, init, prev)
    wspec = lambda part: pl.BlockSpec((None, d, fw), lambda j, i: (0, 0, part * nj + j))
    return pl.pallas_call(
        functools.partial(_ffn_up_kernel, lay, tm, fw, tiles_per_seq),
        grid=(nj, rows // tm),
        in_specs=[pl.BlockSpec((tm, d), lambda j, i: (i, 0)), wspec(0), wspec(1),
                  pl.BlockSpec((CONV_W, fw), lambda j, i: (0, j)), pl.BlockSpec((1, fw), lambda j, i: (0, j))]
                 + xspecs,
        out_specs=[pl.BlockSpec((tm, fw), lambda j, i: (i, j))] + xout_specs,
        out_shape=[jax.ShapeDtypeStruct((rows, dff), BF16)] + xout_shapes,
        scratch_shapes=[pltpu.VMEM((d, 2 * fw), BF16)] + xscratch,
        compiler_params=_cparams("arbitrary", "arbitrary"),
        name="ffn_up_" + lay.mode)(n2, w_up, w_up, conv_w, bias, *xargs)


def _ffn_down_kernel(act_ref, w_ref, h_ref, g_ref, y_ref):
    y_ref[...] = _rms(h_ref[...] + _dot(act_ref[...], w_ref[...]), g_ref[...])


def _ffn_down(act, w, h, g, rows, tm):
    dff = act.shape[1]
    d = w.shape[1]
    row = lambda i: (i, 0)
    return pl.pallas_call(
        _ffn_down_kernel, grid=(rows // tm,),
        in_specs=[pl.BlockSpec((tm, dff), row),
                  pl.BlockSpec((dff, d), lambda i: (0, 0), pipeline_mode=pl.Buffered(1)),
                  pl.BlockSpec((tm, d), row), pl.BlockSpec((1, d), lambda i: (0, 0))],
        out_specs=pl.BlockSpec((tm, d), row),
        out_shape=jax.ShapeDtypeStruct((rows, d), F32),
        compiler_params=_cparams("arbitrary"), name="ffn_down")(act, w, h, g)


def kernel(x_prompt, x_sample, state_conv, state_gla, state_ffn_conv, meta_tokens, norm_mix_g, w_in, conv_mix_w, w_conv_out, w_gate_up, b_gate, gla_norm_g, w_gla_out, w_o, norm_ffn_g, w_ffn_up, ffn_conv_w, ffn_conv_b, w_ffn_down, final_norm_g):
    bp, seq, d = x_prompt.shape
    bs, sl, _ = x_sample.shape
    assert w_in.shape[0] == 1, "single-layer step"
    n_meta = meta_tokens.shape[0]
    dc = state_conv.shape[-1]
    _, _, heads, dk, dv = state_gla.shape
    dff = state_ffn_conv.shape[-1]
    rank = w_gate_up.shape[1]
    assert n_meta % GLA_SUB == 0 and seq % GLA_SUB == 0 and GLA_SUB % sl == 0 and sl >= CONV_W - 1

    o_q = 3 * dc
    o_a = o_q + 2 * heads * dk + 2 * heads * dv
    o_ga = o_a + rank
    tn = heads * dk
    assert o_q % tn == 0 and (heads * dv) % tn == 0 and d % tn == 0
    cw, fw = CONV_COLS, FFN_COLS
    o_g = o_a - heads * dv
    n_g = heads * dv // tn
    tiles_qkv = ((o_g - o_q) // tn, 0, 0)
    tiles_act = (0, n_g, 2 * d // tn)
    spec_qkv = pl.BlockSpec((None, tn, d), lambda j, i: (0, o_q // tn + j, 0))
    assert o_g % 8 == 0 and o_ga % 8 == 0 and tn % 8 == 0
    act_row8 = lambda j: jnp.where(j < n_g, o_g // 8 + j * (tn // 8), o_ga // 8 + (j - n_g) * (tn // 8))
    spec_act = pl.BlockSpec((pl.Element(1), pl.Element(tn), pl.Element(d)), lambda j, i: (0, act_row8(j) * 8, 0))
    q_scale = float(dk) ** -0.5

    w_in_t = jnp.swapaxes(w_in, 1, 2)
    wa = jnp.pad(w_in_t[0, o_a:o_ga], ((0, LANES - rank), (0, 0))).astype(BF16)
    wup = jnp.pad(w_gate_up[0], ((0, LANES - rank), (0, 0))).astype(BF16)
    wc = w_conv_out[0].astype(BF16)
    wg = w_gla_out[0].astype(BF16)
    wo = w_o[0].astype(BF16)
    wd = w_ffn_down[0].astype(BF16)
    g1, g2, gf, gn = norm_mix_g[0][None], norm_ffn_g[0][None], final_norm_g[None], gla_norm_g[0][None]
    bg, fb, cmw, fcw = b_gate[0][None], ffn_conv_b[0][None], conv_mix_w[0], ffn_conv_w[0]

    def in_projections(n, tm, lay, **conv_kw):
        conv_outs = _inproj_conv(n, w_in_t, cmw, lay, tm, cw, **conv_kw)
        lin = _inproj_act(n, w_in_t, spec_qkv, tiles_qkv, tm, tn, q_scale)
        act = _inproj_act(n, w_in_t, spec_act, tiles_act, tm, tn, 1.0)
        return conv_outs, lin, act

    n_short = bs * sl
    rows_s = n_short + n_meta
    xs = jnp.concatenate([x_sample.reshape(n_short, d), meta_tokens.astype(x_sample.dtype)], axis=0)
    lay_s = SeqLayout("table", n_short=n_short, short_len=sl)
    tm_s = rows_s
    n_s, logg_s = _rmsnorm_gate(xs, g1, wa, wup, bg, _row_tile(rows_s, ROW_TILE_RESIDENT))
    (ca_s, conv_s2, conv_s1, conv_tail), lin_s, gates_s = in_projections(
        n_s, tm_s, lay_s, prev=state_conv[0].reshape(bs, (CONV_W - 1) * dc))
    s_zero = jnp.zeros((1, heads, dk, dv), F32)
    ob_m, s_meta = _gla_seq(lin_s, gates_s, logg_s, gn, s_zero, heads, dk, dv, 1, n_meta, n_meta, GLA_SUB,
                            n_short, True)

    rows_p = bp * seq
    xp = x_prompt.reshape(rows_p, d)
    lay_p = SeqLayout("carry", seq_rows=seq)
    tm_p = _row_tile(seq, ROW_TILE)
    tm_o = _row_tile(seq, ROW_TILE_RESIDENT)
    n_p, logg_p = _rmsnorm_gate(xp, g1, wa, wup, bg, _row_tile(seq, ROW_TILE_NORM))
    (ca_p, conv_p), lin_p, gates_p = in_projections(n_p, tm_p, lay_p, init=conv_tail)
    ob_p, s_p, ob_smp, s_smp = _gla_both(
        lin_p, gates_p, logg_p, s_meta, bp, seq, _row_tile(seq, GLA_CHUNK), GLA_SUB,
        lin_s, gates_s, logg_s, state_gla[0], bs, sl, SHORT_SEQS_PER_STEP, gn, heads, dk, dv)

    ob_s = jnp.concatenate([ob_smp, ob_m], axis=0)
    m_s = _merge(ca_s, ob_s, gates_s, wc, wg, tm_s, tn)
    h_s, n2_s = _oproj(m_s, xs, wo, g2, _row_tile(rows_s, ROW_TILE_RESIDENT))
    act_s, ffn_s2, ffn_s1, ffn_tail = _ffn_up(n2_s, w_ffn_up, fcw, fb, lay_s, tm_s, fw,
                                              prev=state_ffn_conv[0].reshape(bs, (CONV_W - 1) * dff))
    y_s = _ffn_down(act_s, wd, h_s, gf, n_short, _row_tile(n_short, ROW_TILE_WIDE_K))

    m_p = _merge(ca_p, ob_p, gates_p, wc, wg, _row_tile(seq, ROW_TILE_MERGE), tn)
    h_p, n2_p = _oproj(m_p, xp, wo, g2, tm_o)
    act_p, ffn_p = _ffn_up(n2_p, w_ffn_up, fcw, fb, lay_p, tm_p, fw, init=ffn_tail)
    y_p = _ffn_down(act_p, wd, h_p, gf, rows_p, _row_tile(seq, ROW_TILE_WIDE_K))

    untile = lambda st: jnp.swapaxes(st, 1, 2).reshape(st.shape[0], CONV_W - 1, -1)
    return (y_p.reshape(bp, seq, d), y_s.reshape(bs, sl, d),
            untile(conv_p)[None], s_p[None], untile(ffn_p)[None],
            jnp.stack([conv_s2, conv_s1], axis=1)[None], s_smp[None], jnp.stack([ffn_s2, ffn_s1], axis=1)[None])
```

```python
import functools

import jax
import jax.numpy as jnp
from jax import lax
from jax.experimental import pallas as pl
from jax.experimental.pallas import tpu as pltpu

EPS = 1e-6
GATE_TAU = 16.0
CONV_W = 3
GLA_SUB = 16
HALO = 8
V7X_VMEM_LIMIT = 56 * 1024 * 1024
LANES = 128
ROW_TILE = 2048
ROW_TILE_RESIDENT = 512
ROW_TILE_MERGE = 1024
ROW_TILE_NORM = 1024
ROW_TILE_WIDE_K = 256
GLA_CHUNK = 128
CONV_COLS = 256
FFN_COLS = 512
SHORT_SEQS_PER_STEP = 8
SUB_ROWS = 256
CAST_ROWS = 128
F32 = jnp.float32
BF16 = jnp.bfloat16


def _cparams(*sem):
    return pltpu.CompilerParams(dimension_semantics=sem, vmem_limit_bytes=V7X_VMEM_LIMIT)


def _row_tile(rows, target):
    best = None
    for t in range(16, min(rows, target) + 1, 16):
        if rows % t == 0:
            best = t
    return best or rows


def _rms(x, g):
    return x * lax.rsqrt(jnp.mean(x * x, axis=-1, keepdims=True) + EPS) * g


def _silu(x):
    h = 0.5 * x
    return h * jnp.tanh(h) + h


def _dot(a, b):
    return jnp.dot(a, b, preferred_element_type=F32)


def _dot_nt(a, b):
    return lax.dot_general(a, b, (((1,), (1,)), ((), ())), preferred_element_type=F32)


def _sliced(tm, matmul, epilogue):
    n = max(1, tm // SUB_ROWS)
    sizes = [SUB_ROWS] * (n - 1) + [tm - SUB_ROWS * (n - 1)]
    r0 = 0
    for rs in sizes:
        epilogue(r0, rs, matmul(r0, rs))
        r0 += rs


def _cast_into(dst_ref, row0, col0, src_ref):
    rows, cols = src_ref.shape

    def body(r, carry):
        off = pl.multiple_of(r * CAST_ROWS, CAST_ROWS)
        dst_ref[pl.ds(row0 + off, CAST_ROWS), col0:col0 + cols] = src_ref[pl.ds(off, CAST_ROWS), :].astype(BF16)
        return carry

    lax.fori_loop(0, rows // CAST_ROWS, body, 0)


def _rmsnorm_gate_kernel(x_ref, g_ref, wa_ref, wup_ref, b_ref, n_ref, lg_ref):
    n = _rms(x_ref[...], g_ref[...]).astype(BF16)
    n_ref[...] = n
    a = _dot_nt(n, wa_ref[...]).astype(BF16)
    z = _dot(a, wup_ref[...]) + b_ref[...]
    lg_ref[...] = (jnp.minimum(z, 0.0) - jnp.log(1.0 + jnp.exp(-jnp.abs(z)))) * (1.0 / GATE_TAU)


def _rmsnorm_gate(x, g, wa, wup, b, tm):
    rows, d = x.shape
    rp, dk = wup.shape
    const = lambda i: (0, 0)
    return pl.pallas_call(
        _rmsnorm_gate_kernel, grid=(rows // tm,),
        in_specs=[pl.BlockSpec((tm, d), lambda i: (i, 0)), pl.BlockSpec((1, d), const),
                  pl.BlockSpec((rp, d), const), pl.BlockSpec((rp, dk), const), pl.BlockSpec((1, dk), const)],
        out_specs=[pl.BlockSpec((tm, d), lambda i: (i, 0)), pl.BlockSpec((tm, dk), lambda i: (i, 0))],
        out_shape=[jax.ShapeDtypeStruct((rows, d), BF16), jax.ShapeDtypeStruct((rows, dk), F32)],
        compiler_params=_cparams("arbitrary"), name="rmsnorm_gate")(x, g, wa, wup, b)


class SeqLayout:
    def __init__(self, mode, seq_rows=None, n_short=None, short_len=None):
        self.mode, self.seq_rows, self.n_short, self.short_len = mode, seq_rows, n_short, short_len


def _load_halo(halo_ref, init_ref, carry_ref, i, tiles_per_seq):
    first = (i % tiles_per_seq) == 0

    @pl.when(first)
    def _():
        halo_ref[...] = init_ref[...]

    @pl.when(jnp.logical_not(first))
    def _():
        halo_ref[...] = carry_ref[...]


def _store_tail(tail, carry_ref, st_ref, i, j, tiles_per_seq):
    carry_ref[...] = tail

    @pl.when((i % tiles_per_seq) == tiles_per_seq - 1)
    def _():
        st_ref[i // tiles_per_seq, j] = tail[HALO - (CONV_W - 1):, :]


def _shift_rows(x, prev_row):
    rows, c = x.shape
    prev8 = jnp.broadcast_to(prev_row, (HALO, c))
    above = jnp.concatenate([prev8, x[:rows - HALO]], axis=0) if rows > HALO else prev8
    last = lax.broadcasted_iota(jnp.int32, (rows, 1), 0) % HALO == HALO - 1
    mixed = jnp.where(last, above, x).reshape(rows // HALO, HALO, c)
    return pltpu.roll(mixed, 1, 1).reshape(rows, c)


class _CausalConv:
    def __init__(self, w_ref, lay, halo, prev2_ref=None, prev1_ref=None):
        w = w_ref[...]
        self.w0, self.w1, self.w2 = w[0:1, :], w[1:2, :], w[2:3, :]
        self.lay, self.tail, self.prev2_ref, self.prev1_ref = lay, halo, prev2_ref, prev1_ref

    def __call__(self, u, r0):
        rs, c = u.shape
        lay, w0, w1 = self.lay, self.w0, self.w1
        um2, um1 = self.tail[HALO - 2:HALO - 1, :], self.tail[HALO - 1:HALO, :]
        s0 = _shift_rows(w0 * u, w0 * um1)
        s1_first = w1 * um1 + w0 * um2
        if lay.mode == "table":
            take = max(0, min(lay.n_short, r0 + rs) - r0)

            def table_rows(ref):
                parts = ([_lane_chunks_load(ref, r0, take)] if take else []) + (
                    [jnp.zeros((rs - take, c), F32)] if take < rs else [])
                return parts[0] if len(parts) == 1 else jnp.concatenate(parts, axis=0)

            p2, p1 = table_rows(self.prev2_ref), table_rows(self.prev1_ref)
            t = r0 + lax.broadcasted_iota(jnp.int32, (rs, 1), 0)
            starts = jnp.where(t < lay.n_short, t % lay.short_len, t - lay.n_short) == 0
            s0 = jnp.where(starts, w0 * p1, s0)
        a = w1 * u + s0
        s1 = _shift_rows(a, s1_first)
        if lay.mode == "table":
            s1 = jnp.where(starts, w1 * p1 + w0 * p2, s1)
        self.tail = u[rs - HALO:, :]
        return self.w2 * u + s1


def _conv_specs(lay, rows, tm, c, nj, init, prev):
    col = lambda j, i: (0, j)
    if lay.mode == "carry":
        nseq = rows // lay.seq_rows
        return ((init,), [pl.BlockSpec((HALO, c), col)],
                [pl.BlockSpec((nseq, nj, CONV_W - 1, c), lambda j, i: (0, 0, 0, 0))],
                [jax.ShapeDtypeStruct((nseq, nj, CONV_W - 1, c), F32)],
                [pltpu.VMEM((HALO, c), F32)] * 2)
    assert tm == rows and lay.short_len >= CONV_W - 1
    nseq = lay.n_short // lay.short_len
    return ((prev, prev), [pl.BlockSpec((nseq, c), col), pl.BlockSpec((nseq, c), lambda j, i: (0, nj + j))],
            [pl.BlockSpec((nseq, c), col)] * 2 + [pl.BlockSpec((HALO, c), col)],
            [jax.ShapeDtypeStruct((nseq, nj * c), F32)] * 2 + [jax.ShapeDtypeStruct((HALO, nj * c), F32)],
            [pltpu.VMEM((c // LANES, lay.n_short, LANES), F32)] * 2 + [pltpu.VMEM((c // LANES, tm, LANES), F32)])


def _lane_chunks_load(ref, r0, rows):
    return jnp.concatenate([ref[k, r0:r0 + rows, :] for k in range(ref.shape[0])], axis=1)


def _lane_chunks_store(ref, r0, x):
    for k in range(ref.shape[0]):
        ref[k, r0:r0 + x.shape[0], :] = x[:, k * LANES:(k + 1) * LANES]


def _fill_tables(lay, tab2_ref, tab1_ref, prev2_ref, prev1_ref):
    nseq = lay.n_short // lay.short_len
    first_rows = pl.ds(0, nseq, stride=lay.short_len)
    for tab_ref, prev_ref in ((tab2_ref, prev2_ref), (tab1_ref, prev1_ref)):
        tab_ref[...] = jnp.zeros(tab_ref.shape, F32)
        for k in range(tab_ref.shape[0]):
            tab_ref[k, first_rows, :] = prev_ref[:, k * LANES:(k + 1) * LANES]


def _store_short_states(lay, rows_ref, tail, last2_ref, last1_ref, tail_ref):
    nseq = lay.n_short // lay.short_len
    for k in range(rows_ref.shape[0]):
        cols = slice(k * LANES, (k + 1) * LANES)
        last2_ref[:, cols] = rows_ref[k, pl.ds(lay.short_len - 2, nseq, stride=lay.short_len), :]
        last1_ref[:, cols] = rows_ref[k, pl.ds(lay.short_len - 1, nseq, stride=lay.short_len), :]
    tail_ref[...] = tail


def _inproj_conv_kernel(lay, tm, cw, tiles_per_seq, *refs):
    n_ref, wb_ref, wc_ref, wh_ref, cwt_ref = refs[:5]
    if lay.mode == "carry":
        init_ref, ca_ref, st_ref, wbf_ref, halo_ref, carry_ref = refs[5:]
    else:
        prev2_ref, prev1_ref, ca_ref, last2_ref, last1_ref, tail_ref, wbf_ref, tab2_ref, tab1_ref, u_ref = refs[5:]
    j, i = pl.program_id(0), pl.program_id(1)

    @pl.when(i == 0)
    def _():
        for part, w_ref in enumerate((wb_ref, wc_ref, wh_ref)):
            _cast_into(wbf_ref, part * cw, 0, w_ref)

    if lay.mode == "carry":
        _load_halo(halo_ref, init_ref, carry_ref, i, tiles_per_seq)
        conv = _CausalConv(cwt_ref, lay, halo_ref[...])
    else:
        _fill_tables(lay, tab2_ref, tab1_ref, prev2_ref, prev1_ref)
        conv = _CausalConv(cwt_ref, lay, jnp.zeros((HALO, cw), F32), tab2_ref, tab1_ref)

    def matmul(r0, rs):
        return _dot_nt(n_ref[r0:r0 + rs, :], wbf_ref[...])

    def epilogue(r0, rs, p):
        u = p[:, cw:2 * cw] * p[:, 2 * cw:]
        if lay.mode == "table":
            _lane_chunks_store(u_ref, r0, u)
        ca_ref[r0:r0 + rs, :] = (p[:, :cw] * conv(u, r0)).astype(BF16)

    _sliced(tm, matmul, epilogue)
    if lay.mode == "carry":
        _store_tail(conv.tail, carry_ref, st_ref, i, j, tiles_per_seq)
    else:
        _store_short_states(lay, u_ref, conv.tail, last2_ref, last1_ref, tail_ref)


def _inproj_conv(n, w_in_t, conv_w, lay, tm, cw, init=None, prev=None):
    rows, d = n.shape
    dc = conv_w.shape[1]
    nj = dc // cw
    tiles_per_seq = (lay.seq_rows // tm) if lay.mode == "carry" else 1
    xargs, xspecs, xout_specs, xout_shapes, xscratch = _conv_specs(lay, rows, tm, cw, nj, init, prev)
    wspec = lambda part: pl.BlockSpec((None, cw, d), lambda j, i: (0, part * nj + j, 0))
    return pl.pallas_call(
        functools.partial(_inproj_conv_kernel, lay, tm, cw, tiles_per_seq),
        grid=(nj, rows // tm),
        in_specs=[pl.BlockSpec((tm, d), lambda j, i: (i, 0)), wspec(0), wspec(1), wspec(2),
                  pl.BlockSpec((CONV_W, cw), lambda j, i: (0, j))] + xspecs,
        out_specs=[pl.BlockSpec((tm, cw), lambda j, i: (i, j))] + xout_specs,
        out_shape=[jax.ShapeDtypeStruct((rows, dc), BF16)] + xout_shapes,
        scratch_shapes=[pltpu.VMEM((3 * cw, d), BF16)] + xscratch,
        compiler_params=_cparams("arbitrary", "arbitrary"),
        name="inproj_conv_" + lay.mode)(n, w_in_t, w_in_t, w_in_t, conv_w, *xargs)


def _inproj_act_kernel(tm, n_lin, n_silu, q_scale, n_ref, w_ref, o_ref, wbf_ref):
    j, i = pl.program_id(0), pl.program_id(1)

    @pl.when(i == 0)
    def _():
        _cast_into(wbf_ref, 0, 0, w_ref.at[0] if len(w_ref.shape) == 3 else w_ref)

    def matmul(r0, rs):
        return _dot_nt(n_ref[r0:r0 + rs, :], wbf_ref[...])

    def epilogue(r0, rs, p):
        if n_lin:
            act = p * jnp.where(j == 0, q_scale, 1.0)
        else:
            h = 0.5 * p
            factor = jnp.where(j < n_silu, h, 0.5) if n_silu else 0.5
            act = factor * (jnp.tanh(h) + 1.0)
        o_ref[r0:r0 + rs, :] = act.astype(BF16)

    _sliced(tm, matmul, epilogue)


def _inproj_act(n, w_t, w_spec, tiles, tm, tn, q_scale):
    rows, d = n.shape
    n_lin, n_silu, n_sig = tiles
    assert not (n_lin and (n_silu or n_sig))
    n_col = n_lin + n_silu + n_sig
    return pl.pallas_call(
        functools.partial(_inproj_act_kernel, tm, n_lin, n_silu, q_scale),
        grid=(n_col, rows // tm),
        in_specs=[pl.BlockSpec((tm, d), lambda j, i: (i, 0)), w_spec],
        out_specs=pl.BlockSpec((tm, tn), lambda j, i: (i, j)),
        out_shape=jax.ShapeDtypeStruct((rows, n_col * tn), BF16),
        scratch_shapes=[pltpu.VMEM((tn, d), BF16)],
        compiler_params=_cparams("arbitrary", "arbitrary"), name="inproj_act")(n, w_t)


def _cumsum_groups(x, sub):
    rows = x.shape[0]
    pos = lax.broadcasted_iota(jnp.int32, (rows, 1), 0) % sub
    s = 1
    while s < sub:
        x = x + jnp.where(pos >= s, pltpu.roll(x, s, 0), 0.0)
        s *= 2
    return x


def _gla_chunk(q, k, v, lg, s, sub):
    c, dk = q.shape
    ngrp = c // sub
    gpu = 2 if ngrp % 2 == 0 else 1
    unit = gpu * sub
    bt = _cumsum_groups(lg, sub)
    r = jnp.zeros((1, dk), F32)
    r_grp, b_rows = [], []
    for g in range(ngrp):
        r_grp.append(r)
        b_rows.append(bt[g * sub:(g + 1) * sub, :] + r)
        r = r + bt[(g + 1) * sub - 1:(g + 1) * sub, :]
    b = jnp.concatenate(b_rows, axis=0) if ngrp > 1 else b_rows[0]
    b_last = r
    qe = (q * jnp.exp(b)).astype(BF16)
    kd = (k * jnp.exp(b_last - b)).astype(BF16)
    att_rows = []
    for i in range(c // unit):
        ref = r_grp[i * gpu + gpu - 1]
        rows = slice(i * unit, (i + 1) * unit)
        seen = (i + 1) * unit
        qt = (q[rows] * jnp.exp(b[rows] - ref)).astype(BF16)
        ke = (k[:seen] * jnp.exp(ref - b[:seen])).astype(BF16)
        if seen < c:
            ke = jnp.concatenate([ke, jnp.zeros((c - seen, dk), BF16)], axis=0)
        a = _dot_nt(qt, ke)
        col = lax.broadcasted_iota(jnp.int32, (unit, c), 1)
        rloc = lax.broadcasted_iota(jnp.int32, (unit, c), 0)
        att_rows.append(jnp.where(col <= rloc + i * unit, a, 0.0))
    att = (jnp.concatenate(att_rows, axis=0) if len(att_rows) > 1 else att_rows[0]).astype(BF16)
    o = _dot(att, v) + _dot(qe, s.astype(BF16))
    upd = lax.dot_general(kd, v, (((0,), (0,)), ((), ())), preferred_element_type=F32)
    dl = jnp.exp(b_last)
    dl_col = jnp.transpose(jnp.broadcast_to(dl, (LANES, dk)))
    dv = s.shape[1]
    s_dec = jnp.concatenate([s[:, n * LANES:(n + 1) * LANES] * dl_col for n in range(dv // LANES)], axis=1)
    return o, s_dec + upd


def _gla_out(o, gn, gs):
    return (_rms(o, gn) * gs.astype(F32)).astype(BF16)


def _gla_seq_kernel(sub, heads, dk, dv, *refs):
    _gla_seq_body(sub, heads, dk, dv, pl.program_id(1) == 0, *refs)


def _gla_seq_body(sub, heads, dk, dv, first_chunk, q_ref, k_ref, v_ref, lg_ref, gs_ref, gn_ref, s0_ref, ob_ref, s_ref):
    @pl.when(first_chunk)
    def _():
        def copy_head(h, carry):
            s_ref[0, h] = s0_ref[0, h]
            return carry

        lax.fori_loop(0, heads, copy_head, 0)

    for h in range(heads):
        _gla_seq_head(sub, dk, dv, h, q_ref, k_ref, v_ref, lg_ref, gs_ref, gn_ref, ob_ref, s_ref)


def _gla_seq_head(sub, dk, dv, h, q_ref, k_ref, v_ref, lg_ref, gs_ref, gn_ref, ob_ref, s_ref):
    ck, cv = slice(h * dk, (h + 1) * dk), slice(h * dv, (h + 1) * dv)
    o, s_new = _gla_chunk(q_ref[:, ck].astype(F32), k_ref[:, ck].astype(F32), v_ref[:, cv], lg_ref[:, ck],
                          s_ref[0, h], sub)
    s_ref[0, h] = s_new
    ob_ref[:, cv] = _gla_out(o, gn_ref[...], gs_ref[:, cv])


def _gla_seq(lin, act, logg, gn, s0, heads, dk, dv, nseq, seq_rows, chunk, sub, row0, shared_init):
    nchunk = seq_rows // chunk
    blk0 = row0 // chunk
    rb = lambda b, c: blk0 + b * nchunk + c
    wk, wv = heads * dk, heads * dv
    assert (2 * wk) % wv == 0
    return pl.pallas_call(
        functools.partial(_gla_seq_kernel, sub, heads, dk, dv),
        grid=(nseq, nchunk),
        in_specs=[pl.BlockSpec((chunk, wk), lambda b, c: (rb(b, c), 0)),
                  pl.BlockSpec((chunk, wk), lambda b, c: (rb(b, c), 1)),
                  pl.BlockSpec((chunk, wv), lambda b, c: (rb(b, c), 2 * wk // wv)),
                  pl.BlockSpec((chunk, wk), lambda b, c: (rb(b, c), 0)),
                  pl.BlockSpec((chunk, wv), lambda b, c: (rb(b, c), 0)),
                  pl.BlockSpec((1, dv), lambda b, c: (0, 0)),
                  pl.BlockSpec((1, heads, dk, dv), lambda b, c: (0 if shared_init else b, 0, 0, 0))],
        out_specs=[pl.BlockSpec((chunk, wv), lambda b, c: (b * nchunk + c, 0)),
                   pl.BlockSpec((1, heads, dk, dv), lambda b, c: (b, 0, 0, 0))],
        out_shape=[jax.ShapeDtypeStruct((nseq * seq_rows, wv), BF16),
                   jax.ShapeDtypeStruct((nseq, heads, dk, dv), F32)],
        compiler_params=_cparams("arbitrary", "arbitrary"),
        name="gla_seq")(lin, lin, lin, logg, act, gn, s0)


def _gla_short_kernel(nb, sl, q_ref, k_ref, v_ref, lg_ref, gs_ref, gn_ref, s0_ref, ob_ref, s_ref):
    q = q_ref[...].astype(F32)
    k = k_ref[...].astype(F32)
    v = v_ref[...].astype(F32)
    lg = lg_ref[...]
    outs = []
    for n in range(nb):
        rs = slice(n * sl, (n + 1) * sl)
        o, s_new = _gla_chunk(q[rs], k[rs], v[rs].astype(BF16), lg[rs], s0_ref[n, 0], sl)
        s_ref[n, 0] = s_new
        outs.append(o)
    o = jnp.concatenate(outs, axis=0)
    ob_ref[...] = _gla_out(o, gn_ref[...], gs_ref[...])


def _gla_both_kernel(sub, heads, dk, dv, nchunk, nb, sl, n_long, n_short, *refs):
    (ql, kl, vl, lgl, gsl, gn, s0l, qs, ks, vs, lgs, gss, s0s, obl, sl_out, obs, ss_out) = refs
    step = pl.program_id(0)

    def long_part():
        _gla_seq_body(sub, heads, dk, dv, (step % nchunk) == 0, ql, kl, vl, lgl, gsl, gn, s0l, obl, sl_out)

    def short_part():
        _gla_short_kernel(nb, sl, qs, ks, vs, lgs, gss, gn, s0s, obs, ss_out)

    if n_long == n_short and nb % heads == 0:
        @pl.when((step % nchunk) == 0)
        def _():
            def copy_head(h, carry):
                sl_out[0, h] = s0l[0, h]
                return carry

            lax.fori_loop(0, heads, copy_head, 0)

        q, k, v, lg = qs[...].astype(F32), ks[...].astype(F32), vs[...].astype(F32), lgs[...]
        per = nb // heads
        outs = []
        for h in range(heads):
            _gla_seq_head(sub, dk, dv, h, ql, kl, vl, lgl, gsl, gn, obl, sl_out)
            for n in range(h * per, (h + 1) * per):
                rs = slice(n * sl, (n + 1) * sl)
                o, s_new = _gla_chunk(q[rs], k[rs], v[rs].astype(BF16), lg[rs], s0s[n, 0], sl)
                ss_out[n, 0] = s_new
                outs.append(o)
        obs[...] = _gla_out(jnp.concatenate(outs, axis=0), gn[...], gss[...])
    else:
        pl.when(step < n_long)(long_part)
        pl.when(step < n_short)(short_part)


def _gla_both(lin_l, act_l, logg_l, s0_l, nseq_l, seq_rows, chunk, sub,
              lin_s, act_s, logg_s, s0_s, nseq_s, sl, nb, gn, heads, dk, dv):
    nchunk = seq_rows // chunk
    n_long, n_short = nseq_l * nchunk, (nseq_s // nb) * heads
    steps = max(n_long, n_short)
    wk, wv = heads * dk, heads * dv
    assert (2 * wk) % wv == 0
    rows = nb * sl
    kv = (2 * heads * dk) // dv
    lo = (lambda s: s) if n_long == steps else (lambda s: jnp.minimum(s, n_long - 1))
    sh = (lambda s: s) if n_short == steps else (lambda s: jnp.minimum(s, n_short - 1))
    long_row = lambda col: (lambda s: (lo(s), col))
    short_blk = lambda off: (lambda s: (sh(s) // heads, off + sh(s) % heads))
    return pl.pallas_call(
        functools.partial(_gla_both_kernel, sub, heads, dk, dv, nchunk, nb, sl, n_long, n_short),
        grid=(steps,),
        in_specs=[pl.BlockSpec((chunk, wk), long_row(0)), pl.BlockSpec((chunk, wk), long_row(1)),
                  pl.BlockSpec((chunk, wv), long_row(2 * wk // wv)), pl.BlockSpec((chunk, wk), long_row(0)),
                  pl.BlockSpec((chunk, wv), long_row(0)),
                  pl.BlockSpec((1, dv), lambda s: (0, 0)),
                  pl.BlockSpec((1, heads, dk, dv), lambda s: (0, 0, 0, 0)),
                  pl.BlockSpec((rows, dk), short_blk(0)), pl.BlockSpec((rows, dk), short_blk(heads)),
                  pl.BlockSpec((rows, dv), short_blk(kv)), pl.BlockSpec((rows, dk), short_blk(0)),
                  pl.BlockSpec((rows, dv), short_blk(0)),
                  pl.BlockSpec((nb, 1, dk, dv), lambda s: (sh(s) // heads, sh(s) % heads, 0, 0))],
        out_specs=[pl.BlockSpec((chunk, wv), long_row(0)),
                   pl.BlockSpec((1, heads, dk, dv), lambda s: (lo(s) // nchunk, 0, 0, 0)),
                   pl.BlockSpec((rows, dv), short_blk(0)),
                   pl.BlockSpec((nb, 1, dk, dv), lambda s: (sh(s) // heads, sh(s) % heads, 0, 0))],
        out_shape=[jax.ShapeDtypeStruct((nseq_l * seq_rows, wv), BF16),
                   jax.ShapeDtypeStruct((nseq_l, heads, dk, dv), F32),
                   jax.ShapeDtypeStruct((nseq_s * sl, wv), BF16),
                   jax.ShapeDtypeStruct((nseq_s, heads, dk, dv), F32)],
        compiler_params=_cparams("arbitrary"),
        name="gla_both")(lin_l, lin_l, lin_l, logg_l, act_l, gn, s0_l, lin_s, lin_s, lin_s, logg_s, act_s, s0_s)


def _merge_kernel(ca_ref, ob_ref, ga_ref, gb_ref, wc_ref, wg_ref, o_ref):
    ya = _dot(ca_ref[...], wc_ref[...])
    yb = _dot(ob_ref[...], wg_ref[...])
    o_ref[...] = (ga_ref[...].astype(F32) * ya + gb_ref[...].astype(F32) * yb).astype(BF16)


def _merge(ca, ob, act, wc, wg, tm, tn):
    rows, dc = ca.shape
    dg, d = wg.shape
    nn = d // tn
    g0 = (act.shape[1] - 2 * d) // tn
    return pl.pallas_call(
        _merge_kernel, grid=(rows // tm, nn),
        in_specs=[pl.BlockSpec((tm, dc), lambda i, j: (i, 0)), pl.BlockSpec((tm, dg), lambda i, j: (i, 0)),
                  pl.BlockSpec((tm, tn), lambda i, j: (i, g0 + j)),
                  pl.BlockSpec((tm, tn), lambda i, j: (i, g0 + nn + j)),
                  pl.BlockSpec((dc, tn), lambda i, j: (0, j)), pl.BlockSpec((dg, tn), lambda i, j: (0, j))],
        out_specs=pl.BlockSpec((tm, tn), lambda i, j: (i, j)),
        out_shape=jax.ShapeDtypeStruct((rows, d), BF16),
        compiler_params=_cparams("arbitrary", "arbitrary"), name="merge")(ca, ob, act, act, wc, wg)


def _oproj_kernel(m_ref, x_ref, w_ref, g_ref, h_ref, n2_ref):
    h = x_ref[...] + _dot(m_ref[...], w_ref[...])
    h_ref[...] = h
    n2_ref[...] = _rms(h, g_ref[...]).astype(BF16)


def _oproj(m, x, w, g, tm):
    rows, d = x.shape
    row = lambda i: (i, 0)
    return pl.pallas_call(
        _oproj_kernel, grid=(rows // tm,),
        in_specs=[pl.BlockSpec((tm, d), row), pl.BlockSpec((tm, d), row),
                  pl.BlockSpec((d, d), lambda i: (0, 0)), pl.BlockSpec((1, d), lambda i: (0, 0))],
        out_specs=[pl.BlockSpec((tm, d), row), pl.BlockSpec((tm, d), row)],
        out_shape=[jax.ShapeDtypeStruct((rows, d), F32), jax.ShapeDtypeStruct((rows, d), BF16)],
        compiler_params=_cparams("arbitrary"), name="oproj")(m, x, w, g)


def _ffn_up_kernel(lay, tm, fw, tiles_per_seq, *refs):
    n_ref, wa_ref, wg_ref, cwt_ref, b_ref = refs[:5]
    if lay.mode == "carry":
        init_ref, act_ref, st_ref, wbf_ref, halo_ref, carry_ref = refs[5:]
    else:
        prev2_ref, prev1_ref, act_ref, last2_ref, last1_ref, tail_ref, wbf_ref, tab2_ref, tab1_ref, gt_ref = refs[5:]
    j, i = pl.program_id(0), pl.program_id(1)

    @pl.when(i == 0)
    def _():
        _cast_into(wbf_ref, 0, 0, wa_ref)
        _cast_into(wbf_ref, 0, fw, wg_ref)

    if lay.mode == "carry":
        _load_halo(halo_ref, init_ref, carry_ref, i, tiles_per_seq)
        conv = _CausalConv(cwt_ref, lay, halo_ref[...])
    else:
        _fill_tables(lay, tab2_ref, tab1_ref, prev2_ref, prev1_ref)
        conv = _CausalConv(cwt_ref, lay, jnp.zeros((HALO, fw), F32), tab2_ref, tab1_ref)

    def matmul(r0, rs):
        return _dot(n_ref[r0:r0 + rs, :], wbf_ref[...])

    def epilogue(r0, rs, p):
        gt = p[:, fw:]
        if lay.mode == "table":
            _lane_chunks_store(gt_ref, r0, gt)
        z = conv(gt, r0) + b_ref[...]
        act_ref[r0:r0 + rs, :] = (_silu(z) * p[:, :fw]).astype(BF16)

    _sliced(tm, matmul, epilogue)
    if lay.mode == "carry":
        _store_tail(conv.tail, carry_ref, st_ref, i, j, tiles_per_seq)
    else:
        _store_short_states(lay, gt_ref, conv.tail, last2_ref, last1_ref, tail_ref)


def _ffn_up(n2, w_up, conv_w, bias, lay, tm, fw, init=None, prev=None):
    rows, d = n2.shape
    dff = conv_w.shape[1]
    nj = dff // fw
    tiles_per_seq = (lay.seq_rows // tm) if lay.mode == "carry" else 1
    xargs, xspecs, xout_specs, xout_shapes, xscratch = _conv_specs(lay, rows, tm, fw, nj, init, prev)
    wspec = lambda part: pl.BlockSpec((None, d, fw), lambda j, i: (0, 0, part * nj + j))
    return pl.pallas_call(
        functools.partial(_ffn_up_kernel, lay, tm, fw, tiles_per_seq),
        grid=(nj, rows // tm),
        in_specs=[pl.BlockSpec((tm, d), lambda j, i: (i, 0)), wspec(0), wspec(1),
                  pl.BlockSpec((CONV_W, fw), lambda j, i: (0, j)), pl.BlockSpec((1, fw), lambda j, i: (0, j))]
                 + xspecs,
        out_specs=[pl.BlockSpec((tm, fw), lambda j, i: (i, j))] + xout_specs,
        out_shape=[jax.ShapeDtypeStruct((rows, dff), BF16)] + xout_shapes,
        scratch_shapes=[pltpu.VMEM((d, 2 * fw), BF16)] + xscratch,
        compiler_params=_cparams("arbitrary", "arbitrary"),
        name="ffn_up_" + lay.mode)(n2, w_up, w_up, conv_w, bias, *xargs)


def _ffn_down_kernel(act_ref, w_ref, h_ref, g_ref, y_ref):
    y_ref[...] = _rms(h_ref[...] + _dot(act_ref[...], w_ref[...]), g_ref[...])


def _ffn_down(act, w, h, g, rows, tm):
    dff = act.shape[1]
    d = w.shape[1]
    row = lambda i: (i, 0)
    return pl.pallas_call(
        _ffn_down_kernel, grid=(rows // tm,),
        in_specs=[pl.BlockSpec((tm, dff), row),
                  pl.BlockSpec((dff, d), lambda i: (0, 0), pipeline_mode=pl.Buffered(1)),
                  pl.BlockSpec((tm, d), row), pl.BlockSpec((1, d), lambda i: (0, 0))],
        out_specs=pl.BlockSpec((tm, d), row),
        out_shape=jax.ShapeDtypeStruct((rows, d), F32),
        compiler_params=_cparams("arbitrary"), name="ffn_down")(act, w, h, g)


def kernel(x_prompt, x_sample, state_conv, state_gla, state_ffn_conv, meta_tokens, norm_mix_g, w_in, conv_mix_w, w_conv_out, w_gate_up, b_gate, gla_norm_g, w_gla_out, w_o, norm_ffn_g, w_ffn_up, ffn_conv_w, ffn_conv_b, w_ffn_down, final_norm_g):
    bp, seq, d = x_prompt.shape
    bs, sl, _ = x_sample.shape
    assert w_in.shape[0] == 1, "single-layer step"
    n_meta = meta_tokens.shape[0]
    dc = state_conv.shape[-1]
    _, _, heads, dk, dv = state_gla.shape
    dff = state_ffn_conv.shape[-1]
    rank = w_gate_up.shape[1]
    assert n_meta % GLA_SUB == 0 and seq % GLA_SUB == 0 and GLA_SUB % sl == 0 and sl >= CONV_W - 1

    o_q = 3 * dc
    o_a = o_q + 2 * heads * dk + 2 * heads * dv
    o_ga = o_a + rank
    tn = heads * dk
    assert o_q % tn == 0 and (heads * dv) % tn == 0 and d % tn == 0
    cw, fw = CONV_COLS, FFN_COLS
    o_g = o_a - heads * dv
    n_g = heads * dv // tn
    tiles_qkv = ((o_g - o_q) // tn, 0, 0)
    tiles_act = (0, n_g, 2 * d // tn)
    spec_qkv = pl.BlockSpec((None, tn, d), lambda j, i: (0, o_q // tn + j, 0))
    assert o_g % 8 == 0 and o_ga % 8 == 0 and tn % 8 == 0
    act_row8 = lambda j: jnp.where(j < n_g, o_g // 8 + j * (tn // 8), o_ga // 8 + (j - n_g) * (tn // 8))
    spec_act = pl.BlockSpec((pl.Element(1), pl.Element(tn), pl.Element(d)), lambda j, i: (0, act_row8(j) * 8, 0))
    q_scale = float(dk) ** -0.5

    w_in_t = jnp.swapaxes(w_in, 1, 2)
    wa = jnp.pad(w_in_t[0, o_a:o_ga], ((0, LANES - rank), (0, 0))).astype(BF16)
    wup = jnp.pad(w_gate_up[0], ((0, LANES - rank), (0, 0))).astype(BF16)
    wc = w_conv_out[0].astype(BF16)
    wg = w_gla_out[0].astype(BF16)
    wo = w_o[0].astype(BF16)
    wd = w_ffn_down[0].astype(BF16)
    g1, g2, gf, gn = norm_mix_g[0][None], norm_ffn_g[0][None], final_norm_g[None], gla_norm_g[0][None]
    bg, fb, cmw, fcw = b_gate[0][None], ffn_conv_b[0][None], conv_mix_w[0], ffn_conv_w[0]

    def in_projections(n, tm, lay, **conv_kw):
        conv_outs = _inproj_conv(n, w_in_t, cmw, lay, tm, cw, **conv_kw)
        lin = _inproj_act(n, w_in_t, spec_qkv, tiles_qkv, tm, tn, q_scale)
        act = _inproj_act(n, w_in_t, spec_act, tiles_act, tm, tn, 1.0)
        return conv_outs, lin, act

    n_short = bs * sl
    rows_s = n_short + n_meta
    xs = jnp.concatenate([x_sample.reshape(n_short, d), meta_tokens.astype(x_sample.dtype)], axis=0)
    lay_s = SeqLayout("table", n_short=n_short, short_len=sl)
    tm_s = rows_s
    n_s, logg_s = _rmsnorm_gate(xs, g1, wa, wup, bg, _row_tile(rows_s, ROW_TILE_RESIDENT))
    (ca_s, conv_s2, conv_s1, conv_tail), lin_s, gates_s = in_projections(
        n_s, tm_s, lay_s, prev=state_conv[0].reshape(bs, (CONV_W - 1) * dc))
    s_zero = jnp.zeros((1, heads, dk, dv), F32)
    ob_m, s_meta = _gla_seq(lin_s, gates_s, logg_s, gn, s_zero, heads, dk, dv, 1, n_meta, n_meta, GLA_SUB,
                            n_short, True)

    rows_p = bp * seq
    xp = x_prompt.reshape(rows_p, d)
    lay_p = SeqLayout("carry", seq_rows=seq)
    tm_p = _row_tile(seq, ROW_TILE)
    tm_o = _row_tile(seq, ROW_TILE_RESIDENT)
    n_p, logg_p = _rmsnorm_gate(xp, g1, wa, wup, bg, _row_tile(seq, ROW_TILE_NORM))
    (ca_p, conv_p), lin_p, gates_p = in_projections(n_p, tm_p, lay_p, init=conv_tail)
    ob_p, s_p, ob_smp, s_smp = _gla_both(
        lin_p, gates_p, logg_p, s_meta, bp, seq, _row_tile(seq, GLA_CHUNK), GLA_SUB,
        lin_s, gates_s, logg_s, state_gla[0], bs, sl, SHORT_SEQS_PER_STEP, gn, heads, dk, dv)

    ob_s = jnp.concatenate([ob_smp, ob_m], axis=0)
    m_s = _merge(ca_s, ob_s, gates_s, wc, wg, tm_s, tn)
    h_s, n2_s = _oproj(m_s, xs, wo, g2, _row_tile(rows_s, ROW_TILE_RESIDENT))
    act_s, ffn_s2, ffn_s1, ffn_tail = _ffn_up(n2_s, w_ffn_up, fcw, fb, lay_s, tm_s, fw,
                                              prev=state_ffn_conv[0].reshape(bs, (CONV_W - 1) * dff))
    y_s = _ffn_down(act_s, wd, h_s, gf, n_short, _row_tile(n_short, ROW_TILE_WIDE_K))

    m_p = _merge(ca_p, ob_p, gates_p, wc, wg, _row_tile(seq, ROW_TILE_MERGE), tn)
    h_p, n2_p = _oproj(m_p, xp, wo, g2, tm_o)
    act_p, ffn_p = _ffn_up(n2_p, w_ffn_up, fcw, fb, lay_p, tm_p, fw, init=ffn_tail)
    y_p = _ffn_down(act_p, wd, h_p, gf, rows_p, _row_tile(seq, ROW_TILE_WIDE_K))

    untile = lambda st: jnp.swapaxes(st, 1, 2).reshape(st.shape[0], CONV_W - 1, -1)
    return (y_p.reshape(bp, seq, d), y_s.reshape(bs, sl, d),
            untile(conv_p)[None], s_p[None], untile(ffn_p)[None],
            jnp.stack([conv_s2, conv_s1], axis=1)[None], s_smp[None], jnp.stack([ffn_s2, ffn_s1], axis=1)[None])
```

```python
import functools

import jax
import jax.numpy as jnp
from jax import lax
from jax.experimental import pallas as pl
from jax.experimental.pallas import tpu as pltpu

EPS = 1e-6
GATE_TAU = 16.0
CONV_W = 3
GLA_SUB = 16
HALO = 8
V7X_VMEM_LIMIT = 56 * 1024 * 1024
LANES = 128
ROW_TILE = 2048
ROW_TILE_RESIDENT = 512
ROW_TILE_MERGE = 1024
ROW_TILE_NORM = 1024
ROW_TILE_WIDE_K = 256
GLA_CHUNK = 128
CONV_COLS = 256
FFN_COLS = 512
SHORT_SEQS_PER_STEP = 2
SUB_ROWS = 256
CAST_ROWS = 128
F32 = jnp.float32
BF16 = jnp.bfloat16


def _cparams(*sem):
    return pltpu.CompilerParams(dimension_semantics=sem, vmem_limit_bytes=V7X_VMEM_LIMIT)


def _row_tile(rows, target):
    best = None
    for t in range(16, min(rows, target) + 1, 16):
        if rows % t == 0:
            best = t
    return best or rows


def _rms(x, g):
    return x * lax.rsqrt(jnp.mean(x * x, axis=-1, keepdims=True) + EPS) * g


def _silu(x):
    h = 0.5 * x
    return h * jnp.tanh(h) + h


def _dot(a, b):
    return jnp.dot(a, b, preferred_element_type=F32)


def _dot_nt(a, b):
    return lax.dot_general(a, b, (((1,), (1,)), ((), ())), preferred_element_type=F32)


def _sliced(tm, matmul, epilogue):
    n = max(1, tm // SUB_ROWS)
    sizes = [SUB_ROWS] * (n - 1) + [tm - SUB_ROWS * (n - 1)]
    r0 = 0
    for rs in sizes:
        epilogue(r0, rs, matmul(r0, rs))
        r0 += rs


def _cast_into(dst_ref, row0, col0, src_ref):
    rows, cols = src_ref.shape

    def body(r, carry):
        off = pl.multiple_of(r * CAST_ROWS, CAST_ROWS)
        dst_ref[pl.ds(row0 + off, CAST_ROWS), col0:col0 + cols] = src_ref[pl.ds(off, CAST_ROWS), :].astype(BF16)
        return carry

    lax.fori_loop(0, rows // CAST_ROWS, body, 0)


def _rmsnorm_gate_kernel(x_ref, g_ref, wa_ref, wup_ref, b_ref, n_ref, lg_ref):
    n = _rms(x_ref[...], g_ref[...]).astype(BF16)
    n_ref[...] = n
    a = _dot_nt(n, wa_ref[...]).astype(BF16)
    z = _dot(a, wup_ref[...]) + b_ref[...]
    lg_ref[...] = (jnp.minimum(z, 0.0) - jnp.log(1.0 + jnp.exp(-jnp.abs(z)))) * (1.0 / GATE_TAU)


def _rmsnorm_gate(x, g, wa, wup, b, tm):
    rows, d = x.shape
    rp, dk = wup.shape
    const = lambda i: (0, 0)
    return pl.pallas_call(
        _rmsnorm_gate_kernel, grid=(rows // tm,),
        in_specs=[pl.BlockSpec((tm, d), lambda i: (i, 0)), pl.BlockSpec((1, d), const),
                  pl.BlockSpec((rp, d), const), pl.BlockSpec((rp, dk), const), pl.BlockSpec((1, dk), const)],
        out_specs=[pl.BlockSpec((tm, d), lambda i: (i, 0)), pl.BlockSpec((tm, dk), lambda i: (i, 0))],
        out_shape=[jax.ShapeDtypeStruct((rows, d), BF16), jax.ShapeDtypeStruct((rows, dk), F32)],
        compiler_params=_cparams("arbitrary"), name="rmsnorm_gate")(x, g, wa, wup, b)


class SeqLayout:
    def __init__(self, mode, seq_rows=None, n_short=None, short_len=None):
        self.mode, self.seq_rows, self.n_short, self.short_len = mode, seq_rows, n_short, short_len


def _load_halo(halo_ref, init_ref, carry_ref, i, tiles_per_seq):
    first = (i % tiles_per_seq) == 0

    @pl.when(first)
    def _():
        halo_ref[...] = init_ref[...]

    @pl.when(jnp.logical_not(first))
    def _():
        halo_ref[...] = carry_ref[...]


def _store_tail(tail, carry_ref, st_ref, i, j, tiles_per_seq):
    carry_ref[...] = tail

    @pl.when((i % tiles_per_seq) == tiles_per_seq - 1)
    def _():
        st_ref[i // tiles_per_seq, j] = tail[HALO - (CONV_W - 1):, :]


def _shift_rows(x, prev_row):
    rows, c = x.shape
    prev8 = jnp.broadcast_to(prev_row, (HALO, c))
    above = jnp.concatenate([prev8, x[:rows - HALO]], axis=0) if rows > HALO else prev8
    last = lax.broadcasted_iota(jnp.int32, (rows, 1), 0) % HALO == HALO - 1
    mixed = jnp.where(last, above, x).reshape(rows // HALO, HALO, c)
    return pltpu.roll(mixed, 1, 1).reshape(rows, c)


class _CausalConv:
    def __init__(self, w_ref, lay, halo, prev2_ref=None, prev1_ref=None):
        w = w_ref[...]
        self.w0, self.w1, self.w2 = w[0:1, :], w[1:2, :], w[2:3, :]
        self.lay, self.tail, self.prev2_ref, self.prev1_ref = lay, halo, prev2_ref, prev1_ref

    def __call__(self, u, r0):
        rs, c = u.shape
        lay, w0, w1 = self.lay, self.w0, self.w1
        um2, um1 = self.tail[HALO - 2:HALO - 1, :], self.tail[HALO - 1:HALO, :]
        s0 = _shift_rows(w0 * u, w0 * um1)
        s1_first = w1 * um1 + w0 * um2
        if lay.mode == "table":
            take = max(0, min(lay.n_short, r0 + rs) - r0)

            def table_rows(ref):
                parts = ([_lane_chunks_load(ref, r0, take)] if take else []) + (
                    [jnp.zeros((rs - take, c), F32)] if take < rs else [])
                return parts[0] if len(parts) == 1 else jnp.concatenate(parts, axis=0)

            p2, p1 = table_rows(self.prev2_ref), table_rows(self.prev1_ref)
            t = r0 + lax.broadcasted_iota(jnp.int32, (rs, 1), 0)
            starts = jnp.where(t < lay.n_short, t % lay.short_len, t - lay.n_short) == 0
            s0 = jnp.where(starts, w0 * p1, s0)
        a = w1 * u + s0
        s1 = _shift_rows(a, s1_first)
        if lay.mode == "table":
            s1 = jnp.where(starts, w1 * p1 + w0 * p2, s1)
        self.tail = u[rs - HALO:, :]
        return self.w2 * u + s1


def _conv_specs(lay, rows, tm, c, nj, init, prev):
    col = lambda j, i: (0, j)
    if lay.mode == "carry":
        nseq = rows // lay.seq_rows
        return ((init,), [pl.BlockSpec((HALO, c), col)],
                [pl.BlockSpec((nseq, nj, CONV_W - 1, c), lambda j, i: (0, 0, 0, 0))],
                [jax.ShapeDtypeStruct((nseq, nj, CONV_W - 1, c), F32)],
                [pltpu.VMEM((HALO, c), F32)] * 2)
    assert tm == rows and lay.short_len >= CONV_W - 1
    nseq = lay.n_short // lay.short_len
    return ((prev, prev), [pl.BlockSpec((nseq, c), col), pl.BlockSpec((nseq, c), lambda j, i: (0, nj + j))],
            [pl.BlockSpec((nseq, c), col)] * 2 + [pl.BlockSpec((HALO, c), col)],
            [jax.ShapeDtypeStruct((nseq, nj * c), F32)] * 2 + [jax.ShapeDtypeStruct((HALO, nj * c), F32)],
            [pltpu.VMEM((c // LANES, lay.n_short, LANES), F32)] * 2 + [pltpu.VMEM((c // LANES, tm, LANES), F32)])


def _lane_chunks_load(ref, r0, rows):
    return jnp.concatenate([ref[k, r0:r0 + rows, :] for k in range(ref.shape[0])], axis=1)


def _lane_chunks_store(ref, r0, x):
    for k in range(ref.shape[0]):
        ref[k, r0:r0 + x.shape[0], :] = x[:, k * LANES:(k + 1) * LANES]


def _fill_tables(lay, tab2_ref, tab1_ref, prev2_ref, prev1_ref):
    nseq = lay.n_short // lay.short_len
    first_rows = pl.ds(0, nseq, stride=lay.short_len)
    for tab_ref, prev_ref in ((tab2_ref, prev2_ref), (tab1_ref, prev1_ref)):
        tab_ref[...] = jnp.zeros(tab_ref.shape, F32)
        for k in range(tab_ref.shape[0]):
            tab_ref[k, first_rows, :] = prev_ref[:, k * LANES:(k + 1) * LANES]


def _store_short_states(lay, rows_ref, tail, last2_ref, last1_ref, tail_ref):
    nseq = lay.n_short // lay.short_len
    for k in range(rows_ref.shape[0]):
        cols = slice(k * LANES, (k + 1) * LANES)
        last2_ref[:, cols] = rows_ref[k, pl.ds(lay.short_len - 2, nseq, stride=lay.short_len), :]
        last1_ref[:, cols] = rows_ref[k, pl.ds(lay.short_len - 1, nseq, stride=lay.short_len), :]
    tail_ref[...] = tail


def _inproj_conv_kernel(lay, tm, cw, tiles_per_seq, *refs):
    n_ref, wb_ref, wc_ref, wh_ref, cwt_ref = refs[:5]
    if lay.mode == "carry":
        init_ref, ca_ref, st_ref, wbf_ref, halo_ref, carry_ref = refs[5:]
    else:
        prev2_ref, prev1_ref, ca_ref, last2_ref, last1_ref, tail_ref, wbf_ref, tab2_ref, tab1_ref, u_ref = refs[5:]
    j, i = pl.program_id(0), pl.program_id(1)

    @pl.when(i == 0)
    def _():
        for part, w_ref in enumerate((wb_ref, wc_ref, wh_ref)):
            _cast_into(wbf_ref, part * cw, 0, w_ref)

    if lay.mode == "carry":
        _load_halo(halo_ref, init_ref, carry_ref, i, tiles_per_seq)
        conv = _CausalConv(cwt_ref, lay, halo_ref[...])
    else:
        _fill_tables(lay, tab2_ref, tab1_ref, prev2_ref, prev1_ref)
        conv = _CausalConv(cwt_ref, lay, jnp.zeros((HALO, cw), F32), tab2_ref, tab1_ref)

    def matmul(r0, rs):
        return _dot_nt(n_ref[r0:r0 + rs, :], wbf_ref[...])

    def epilogue(r0, rs, p):
        u = p[:, cw:2 * cw] * p[:, 2 * cw:]
        if lay.mode == "table":
            _lane_chunks_store(u_ref, r0, u)
        ca_ref[r0:r0 + rs, :] = (p[:, :cw] * conv(u, r0)).astype(BF16)

    _sliced(tm, matmul, epilogue)
    if lay.mode == "carry":
        _store_tail(conv.tail, carry_ref, st_ref, i, j, tiles_per_seq)
    else:
        _store_short_states(lay, u_ref, conv.tail, last2_ref, last1_ref, tail_ref)


def _inproj_conv(n, w_in_t, conv_w, lay, tm, cw, init=None, prev=None):
    rows, d = n.shape
    dc = conv_w.shape[1]
    nj = dc // cw
    tiles_per_seq = (lay.seq_rows // tm) if lay.mode == "carry" else 1
    xargs, xspecs, xout_specs, xout_shapes, xscratch = _conv_specs(lay, rows, tm, cw, nj, init, prev)
    wspec = lambda part: pl.BlockSpec((None, cw, d), lambda j, i: (0, part * nj + j, 0))
    return pl.pallas_call(
        functools.partial(_inproj_conv_kernel, lay, tm, cw, tiles_per_seq),
        grid=(nj, rows // tm),
        in_specs=[pl.BlockSpec((tm, d), lambda j, i: (i, 0)), wspec(0), wspec(1), wspec(2),
                  pl.BlockSpec((CONV_W, cw), lambda j, i: (0, j))] + xspecs,
        out_specs=[pl.BlockSpec((tm, cw), lambda j, i: (i, j))] + xout_specs,
        out_shape=[jax.ShapeDtypeStruct((rows, dc), BF16)] + xout_shapes,
        scratch_shapes=[pltpu.VMEM((3 * cw, d), BF16)] + xscratch,
        compiler_params=_cparams("arbitrary", "arbitrary"),
        name="inproj_conv_" + lay.mode)(n, w_in_t, w_in_t, w_in_t, conv_w, *xargs)


def _inproj_act_kernel(tm, n_lin, n_silu, q_scale, n_ref, w_ref, o_ref, wbf_ref):
    j, i = pl.program_id(0), pl.program_id(1)

    @pl.when(i == 0)
    def _():
        _cast_into(wbf_ref, 0, 0, w_ref.at[0] if len(w_ref.shape) == 3 else w_ref)

    def matmul(r0, rs):
        return _dot_nt(n_ref[r0:r0 + rs, :], wbf_ref[...])

    def epilogue(r0, rs, p):
        if n_lin:
            act = p * jnp.where(j == 0, q_scale, 1.0)
        else:
            h = 0.5 * p
            factor = jnp.where(j < n_silu, h, 0.5) if n_silu else 0.5
            act = factor * (jnp.tanh(h) + 1.0)
        o_ref[r0:r0 + rs, :] = act.astype(BF16)

    _sliced(tm, matmul, epilogue)


def _inproj_act(n, w_t, w_spec, tiles, tm, tn, q_scale):
    rows, d = n.shape
    n_lin, n_silu, n_sig = tiles
    assert not (n_lin and (n_silu or n_sig))
    n_col = n_lin + n_silu + n_sig
    return pl.pallas_call(
        functools.partial(_inproj_act_kernel, tm, n_lin, n_silu, q_scale),
        grid=(n_col, rows // tm),
        in_specs=[pl.BlockSpec((tm, d), lambda j, i: (i, 0)), w_spec],
        out_specs=pl.BlockSpec((tm, tn), lambda j, i: (i, j)),
        out_shape=jax.ShapeDtypeStruct((rows, n_col * tn), BF16),
        scratch_shapes=[pltpu.VMEM((tn, d), BF16)],
        compiler_params=_cparams("arbitrary", "arbitrary"), name="inproj_act")(n, w_t)


def _cumsum_groups(x, sub):
    rows = x.shape[0]
    pos = lax.broadcasted_iota(jnp.int32, (rows, 1), 0) % sub
    s = 1
    while s < sub:
        x = x + jnp.where(pos >= s, pltpu.roll(x, s, 0), 0.0)
        s *= 2
    return x


def _gla_chunk(q, k, v, lg, s, sub):
    c, dk = q.shape
    ngrp = c // sub
    gpu = 2 if ngrp % 2 == 0 else 1
    unit = gpu * sub
    bt = _cumsum_groups(lg, sub)
    r = jnp.zeros((1, dk), F32)
    r_grp, b_rows = [], []
    for g in range(ngrp):
        r_grp.append(r)
        b_rows.append(bt[g * sub:(g + 1) * sub, :] + r)
        r = r + bt[(g + 1) * sub - 1:(g + 1) * sub, :]
    b = jnp.concatenate(b_rows, axis=0) if ngrp > 1 else b_rows[0]
    b_last = r
    qe = (q * jnp.exp(b)).astype(BF16)
    kd = (k * jnp.exp(b_last - b)).astype(BF16)
    att_rows = []
    for i in range(c // unit):
        ref = r_grp[i * gpu + gpu - 1]
        rows = slice(i * unit, (i + 1) * unit)
        seen = (i + 1) * unit
        qt = (q[rows] * jnp.exp(b[rows] - ref)).astype(BF16)
        ke = (k[:seen] * jnp.exp(ref - b[:seen])).astype(BF16)
        if seen < c:
            ke = jnp.concatenate([ke, jnp.zeros((c - seen, dk), BF16)], axis=0)
        a = _dot_nt(qt, ke)
        col = lax.broadcasted_iota(jnp.int32, (unit, c), 1)
        rloc = lax.broadcasted_iota(jnp.int32, (unit, c), 0)
        att_rows.append(jnp.where(col <= rloc + i * unit, a, 0.0))
    att = (jnp.concatenate(att_rows, axis=0) if len(att_rows) > 1 else att_rows[0]).astype(BF16)
    o = _dot(att, v) + _dot(qe, s.astype(BF16))
    upd = lax.dot_general(kd, v, (((0,), (0,)), ((), ())), preferred_element_type=F32)
    dl = jnp.exp(b_last)
    dl_col = jnp.transpose(jnp.broadcast_to(dl, (LANES, dk)))
    dv = s.shape[1]
    s_dec = jnp.concatenate([s[:, n * LANES:(n + 1) * LANES] * dl_col for n in range(dv // LANES)], axis=1)
    return o, s_dec + upd


def _gla_out(o, gn, gs):
    return (_rms(o, gn) * gs.astype(F32)).astype(BF16)


def _gla_seq_kernel(sub, heads, dk, dv, *refs):
    _gla_seq_body(sub, heads, dk, dv, pl.program_id(1) == 0, *refs)


def _gla_seq_body(sub, heads, dk, dv, first_chunk, q_ref, k_ref, v_ref, lg_ref, gs_ref, gn_ref, s0_ref, ob_ref, s_ref):
    @pl.when(first_chunk)
    def _():
        def copy_head(h, carry):
            s_ref[0, h] = s0_ref[0, h]
            return carry

        lax.fori_loop(0, heads, copy_head, 0)

    for h in range(heads):
        _gla_seq_head(sub, dk, dv, h, q_ref, k_ref, v_ref, lg_ref, gs_ref, gn_ref, ob_ref, s_ref)


def _gla_seq_head(sub, dk, dv, h, q_ref, k_ref, v_ref, lg_ref, gs_ref, gn_ref, ob_ref, s_ref):
    ck, cv = slice(h * dk, (h + 1) * dk), slice(h * dv, (h + 1) * dv)
    o, s_new = _gla_chunk(q_ref[:, ck].astype(F32), k_ref[:, ck].astype(F32), v_ref[:, cv], lg_ref[:, ck],
                          s_ref[0, h], sub)
    s_ref[0, h] = s_new
    ob_ref[:, cv] = _gla_out(o, gn_ref[...], gs_ref[:, cv])


def _gla_seq(lin, act, logg, gn, s0, heads, dk, dv, nseq, seq_rows, chunk, sub, row0, shared_init):
    nchunk = seq_rows // chunk
    blk0 = row0 // chunk
    rb = lambda b, c: blk0 + b * nchunk + c
    wk, wv = heads * dk, heads * dv
    assert (2 * wk) % wv == 0
    return pl.pallas_call(
        functools.partial(_gla_seq_kernel, sub, heads, dk, dv),
        grid=(nseq, nchunk),
        in_specs=[pl.BlockSpec((chunk, wk), lambda b, c: (rb(b, c), 0)),
                  pl.BlockSpec((chunk, wk), lambda b, c: (rb(b, c), 1)),
                  pl.BlockSpec((chunk, wv), lambda b, c: (rb(b, c), 2 * wk // wv)),
                  pl.BlockSpec((chunk, wk), lambda b, c: (rb(b, c), 0)),
                  pl.BlockSpec((chunk, wv), lambda b, c: (rb(b, c), 0)),
                  pl.BlockSpec((1, dv), lambda b, c: (0, 0)),
                  pl.BlockSpec((1, heads, dk, dv), lambda b, c: (0 if shared_init else b, 0, 0, 0))],
        out_specs=[pl.BlockSpec((chunk, wv), lambda b, c: (b * nchunk + c, 0)),
                   pl.BlockSpec((1, heads, dk, dv), lambda b, c: (b, 0, 0, 0))],
        out_shape=[jax.ShapeDtypeStruct((nseq * seq_rows, wv), BF16),
                   jax.ShapeDtypeStruct((nseq, heads, dk, dv), F32)],
        compiler_params=_cparams("arbitrary", "arbitrary"),
        name="gla_seq")(lin, lin, lin, logg, act, gn, s0)


class _ShortGla:
    def __init__(self, nb, sl, dk, dv, q_ref, k_ref, v_ref, lg_ref, gs_ref, gn_ref, s0_ref, ob_ref, s_ref):
        self.nb, self.sl, self.dk, self.dv = nb, sl, dk, dv
        self.q, self.k, self.v = q_ref[...].astype(F32), k_ref[...].astype(F32), v_ref[...].astype(F32)
        self.lg, self.gs_ref, self.gn_ref = lg_ref[...], gs_ref, gn_ref
        self.s0_ref, self.ob_ref, self.s_ref = s0_ref, ob_ref, s_ref

    def head(self, h):
        ck, cv = slice(h * self.dk, (h + 1) * self.dk), slice(h * self.dv, (h + 1) * self.dv)
        outs = []
        for n in range(self.nb):
            rs = slice(n * self.sl, (n + 1) * self.sl)
            o, s_new = _gla_chunk(self.q[rs, ck], self.k[rs, ck], self.v[rs, cv].astype(BF16), self.lg[rs, ck],
                                  self.s0_ref[n, h], self.sl)
            self.s_ref[n, h] = s_new
            outs.append(o)
        o = jnp.concatenate(outs, axis=0) if self.nb > 1 else outs[0]
        self.ob_ref[:, cv] = _gla_out(o, self.gn_ref[...], self.gs_ref[:, cv])


def _gla_both_kernel(sub, heads, dk, dv, nchunk, nb, sl, n_long, n_short, *refs):
    (ql, kl, vl, lgl, gsl, gn, s0l, qs, ks, vs, lgs, gss, s0s, obl, sl_out, obs, ss_out) = refs
    step = pl.program_id(0)

    def long_part():
        _gla_seq_body(sub, heads, dk, dv, (step % nchunk) == 0, ql, kl, vl, lgl, gsl, gn, s0l, obl, sl_out)

    def short_part():
        short = _ShortGla(nb, sl, dk, dv, qs, ks, vs, lgs, gss, gn, s0s, obs, ss_out)
        for h in range(heads):
            short.head(h)

    if n_long == n_short:
        @pl.when((step % nchunk) == 0)
        def _():
            def copy_head(h, carry):
                sl_out[0, h] = s0l[0, h]
                return carry

            lax.fori_loop(0, heads, copy_head, 0)

        short = _ShortGla(nb, sl, dk, dv, qs, ks, vs, lgs, gss, gn, s0s, obs, ss_out)
        for h in range(heads):
            _gla_seq_head(sub, dk, dv, h, ql, kl, vl, lgl, gsl, gn, obl, sl_out)
            short.head(h)
    else:
        pl.when(step < n_long)(long_part)
        pl.when(step < n_short)(short_part)


def _gla_both(lin_l, act_l, logg_l, s0_l, nseq_l, seq_rows, chunk, sub,
              lin_s, act_s, logg_s, s0_s, nseq_s, sl, nb, gn, heads, dk, dv):
    nchunk = seq_rows // chunk
    n_long, n_short = nseq_l * nchunk, nseq_s // nb
    steps = max(n_long, n_short)
    wk, wv = heads * dk, heads * dv
    assert (2 * wk) % wv == 0
    rows = nb * sl
    lo = (lambda s: s) if n_long == steps else (lambda s: jnp.minimum(s, n_long - 1))
    sh = (lambda s: s) if n_short == steps else (lambda s: jnp.minimum(s, n_short - 1))
    long_row = lambda col: (lambda s: (lo(s), col))
    short_row = lambda col: (lambda s: (sh(s), col))
    short_state = pl.BlockSpec((nb, heads, dk, dv), lambda s: (sh(s), 0, 0, 0))
    return pl.pallas_call(
        functools.partial(_gla_both_kernel, sub, heads, dk, dv, nchunk, nb, sl, n_long, n_short),
        grid=(steps,),
        in_specs=[pl.BlockSpec((chunk, wk), long_row(0)), pl.BlockSpec((chunk, wk), long_row(1)),
                  pl.BlockSpec((chunk, wv), long_row(2 * wk // wv)), pl.BlockSpec((chunk, wk), long_row(0)),
                  pl.BlockSpec((chunk, wv), long_row(0)),
                  pl.BlockSpec((1, dv), lambda s: (0, 0)),
                  pl.BlockSpec((1, heads, dk, dv), lambda s: (0, 0, 0, 0)),
                  pl.BlockSpec((rows, wk), short_row(0)), pl.BlockSpec((rows, wk), short_row(1)),
                  pl.BlockSpec((rows, wv), short_row(2 * wk // wv)), pl.BlockSpec((rows, wk), short_row(0)),
                  pl.BlockSpec((rows, wv), short_row(0)),
                  short_state],
        out_specs=[pl.BlockSpec((chunk, wv), long_row(0)),
                   pl.BlockSpec((1, heads, dk, dv), lambda s: (lo(s) // nchunk, 0, 0, 0)),
                   pl.BlockSpec((rows, wv), short_row(0)),
                   short_state],
        out_shape=[jax.ShapeDtypeStruct((nseq_l * seq_rows, wv), BF16),
                   jax.ShapeDtypeStruct((nseq_l, heads, dk, dv), F32),
                   jax.ShapeDtypeStruct((nseq_s * sl, wv), BF16),
                   jax.ShapeDtypeStruct((nseq_s, heads, dk, dv), F32)],
        compiler_params=_cparams("arbitrary"),
        name="gla_both")(lin_l, lin_l, lin_l, logg_l, act_l, gn, s0_l, lin_s, lin_s, lin_s, logg_s, act_s, s0_s)


def _merge_kernel(ca_ref, ob_ref, ga_ref, gb_ref, wc_ref, wg_ref, o_ref):
    ya = _dot(ca_ref[...], wc_ref[...])
    yb = _dot(ob_ref[...], wg_ref[...])
    o_ref[...] = (ga_ref[...].astype(F32) * ya + gb_ref[...].astype(F32) * yb).astype(BF16)


def _merge(ca, ob, act, wc, wg, tm, tn):
    rows, dc = ca.shape
    dg, d = wg.shape
    nn = d // tn
    g0 = (act.shape[1] - 2 * d) // tn
    return pl.pallas_call(
        _merge_kernel, grid=(rows // tm, nn),
        in_specs=[pl.BlockSpec((tm, dc), lambda i, j: (i, 0)), pl.BlockSpec((tm, dg), lambda i, j: (i, 0)),
                  pl.BlockSpec((tm, tn), lambda i, j: (i, g0 + j)),
                  pl.BlockSpec((tm, tn), lambda i, j: (i, g0 + nn + j)),
                  pl.BlockSpec((dc, tn), lambda i, j: (0, j)), pl.BlockSpec((dg, tn), lambda i, j: (0, j))],
        out_specs=pl.BlockSpec((tm, tn), lambda i, j: (i, j)),
        out_shape=jax.ShapeDtypeStruct((rows, d), BF16),
        compiler_params=_cparams("arbitrary", "arbitrary"), name="merge")(ca, ob, act, act, wc, wg)


def _oproj_kernel(m_ref, x_ref, w_ref, g_ref, h_ref, n2_ref):
    h = x_ref[...] + _dot(m_ref[...], w_ref[...])
    h_ref[...] = h
    n2_ref[...] = _rms(h, g_ref[...]).astype(BF16)


def _oproj(m, x, w, g, tm):
    rows, d = x.shape
    row = lambda i: (i, 0)
    return pl.pallas_call(
        _oproj_kernel, grid=(rows // tm,),
        in_specs=[pl.BlockSpec((tm, d), row), pl.BlockSpec((tm, d), row),
                  pl.BlockSpec((d, d), lambda i: (0, 0)), pl.BlockSpec((1, d), lambda i: (0, 0))],
        out_specs=[pl.BlockSpec((tm, d), row), pl.BlockSpec((tm, d), row)],
        out_shape=[jax.ShapeDtypeStruct((rows, d), F32), jax.ShapeDtypeStruct((rows, d), BF16)],
        compiler_params=_cparams("arbitrary"), name="oproj")(m, x, w, g)


def _ffn_up_kernel(lay, tm, fw, tiles_per_seq, *refs):
    n_ref, wa_ref, wg_ref, cwt_ref, b_ref = refs[:5]
    if lay.mode == "carry":
        init_ref, act_ref, st_ref, wbf_ref, halo_ref, carry_ref = refs[5:]
    else:
        prev2_ref, prev1_ref, act_ref, last2_ref, last1_ref, tail_ref, wbf_ref, tab2_ref, tab1_ref, gt_ref = refs[5:]
    j, i = pl.program_id(0), pl.program_id(1)

    @pl.when(i == 0)
    def _():
        _cast_into(wbf_ref, 0, 0, wa_ref)
        _cast_into(wbf_ref, 0, fw, wg_ref)

    if lay.mode == "carry":
        _load_halo(halo_ref, init_ref, carry_ref, i, tiles_per_seq)
        conv = _CausalConv(cwt_ref, lay, halo_ref[...])
    else:
        _fill_tables(lay, tab2_ref, tab1_ref, prev2_ref, prev1_ref)
        conv = _CausalConv(cwt_ref, lay, jnp.zeros((HALO, fw), F32), tab2_ref, tab1_ref)

    def matmul(r0, rs):
        return _dot(n_ref[r0:r0 + rs, :], wbf_ref[...])

    def epilogue(r0, rs, p):
        gt = p[:, fw:]
        if lay.mode == "table":
            _lane_chunks_store(gt_ref, r0, gt)
        z = conv(gt, r0) + b_ref[...]
        act_ref[r0:r0 + rs, :] = (_silu(z) * p[:, :fw]).astype(BF16)

    _sliced(tm, matmul, epilogue)
    if lay.mode == "carry":
        _store_tail(conv.tail, carry_ref, st_ref, i, j, tiles_per_seq)
    else:
        _store_short_states(lay, gt_ref, conv.tail, last2_ref, last1_ref, tail_ref)


def _ffn_up(n2, w_up, conv_w, bias, lay, tm, fw, init=None, prev=None):
    rows, d = n2.shape
    dff = conv_w.shape[1]
    nj = dff // fw
    tiles_per_seq = (lay.seq_rows // tm) if lay.mode == "carry" else 1
    xargs, xspecs, xout_specs, xout_shapes, xscratch = _conv_specs(lay, rows, tm, fw, nj, init, prev)
    wspec = lambda part: pl.BlockSpec((None, d, fw), lambda j, i: (0, 0, part * nj + j))
    return pl.pallas_call(
        functools.partial(_ffn_up_kernel, lay, tm, fw, tiles_per_seq),
        grid=(nj, rows // tm),
        in_specs=[pl.BlockSpec((tm, d), lambda j, i: (i, 0)), wspec(0), wspec(1),
                  pl.BlockSpec((CONV_W, fw), lambda j, i: (0, j)), pl.BlockSpec((1, fw), lambda j, i: (0, j))]
                 + xspecs,
        out_specs=[pl.BlockSpec((tm, fw), lambda j, i: (i, j))] + xout_specs,
        out_shape=[jax.ShapeDtypeStruct((rows, dff), BF16)] + xout_shapes,
        scratch_shapes=[pltpu.VMEM((d, 2 * fw), BF16)] + xscratch,
        compiler_params=_cparams("arbitrary", "arbitrary"),
        name="ffn_up_" + lay.mode)(n2, w_up, w_up, conv_w, bias, *xargs)


def _ffn_down_kernel(act_ref, w_ref, h_ref, g_ref, y_ref):
    y_ref[...] = _rms(h_ref[...] + _dot(act_ref[...], w_ref[...]), g_ref[...])


def _ffn_down(act, w, h, g, rows, tm):
    dff = act.shape[1]
    d = w.shape[1]
    row = lambda i: (i, 0)
    return pl.pallas_call(
        _ffn_down_kernel, grid=(rows // tm,),
        in_specs=[pl.BlockSpec((tm, dff), row),
                  pl.BlockSpec((dff, d), lambda i: (0, 0), pipeline_mode=pl.Buffered(1)),
                  pl.BlockSpec((tm, d), row), pl.BlockSpec((1, d), lambda i: (0, 0))],
        out_specs=pl.BlockSpec((tm, d), row),
        out_shape=jax.ShapeDtypeStruct((rows, d), F32),
        compiler_params=_cparams("arbitrary"), name="ffn_down")(act, w, h, g)


def kernel(x_prompt, x_sample, state_conv, state_gla, state_ffn_conv, meta_tokens, norm_mix_g, w_in, conv_mix_w, w_conv_out, w_gate_up, b_gate, gla_norm_g, w_gla_out, w_o, norm_ffn_g, w_ffn_up, ffn_conv_w, ffn_conv_b, w_ffn_down, final_norm_g):
    bp, seq, d = x_prompt.shape
    bs, sl, _ = x_sample.shape
    assert w_in.shape[0] == 1, "single-layer step"
    n_meta = meta_tokens.shape[0]
    dc = state_conv.shape[-1]
    _, _, heads, dk, dv = state_gla.shape
    dff = state_ffn_conv.shape[-1]
    rank = w_gate_up.shape[1]
    assert n_meta % GLA_SUB == 0 and seq % GLA_SUB == 0 and GLA_SUB % sl == 0 and sl >= CONV_W - 1

    o_q = 3 * dc
    o_a = o_q + 2 * heads * dk + 2 * heads * dv
    o_ga = o_a + rank
    tn = heads * dk
    assert o_q % tn == 0 and (heads * dv) % tn == 0 and d % tn == 0
    cw, fw = CONV_COLS, FFN_COLS
    o_g = o_a - heads * dv
    n_g = heads * dv // tn
    tiles_qkv = ((o_g - o_q) // tn, 0, 0)
    tiles_act = (0, n_g, 2 * d // tn)
    spec_qkv = pl.BlockSpec((None, tn, d), lambda j, i: (0, o_q // tn + j, 0))
    assert o_g % 8 == 0 and o_ga % 8 == 0 and tn % 8 == 0
    act_row8 = lambda j: jnp.where(j < n_g, o_g // 8 + j * (tn // 8), o_ga // 8 + (j - n_g) * (tn // 8))
    spec_act = pl.BlockSpec((pl.Element(1), pl.Element(tn), pl.Element(d)), lambda j, i: (0, act_row8(j) * 8, 0))
    q_scale = float(dk) ** -0.5

    w_in_t = jnp.swapaxes(w_in, 1, 2)
    wa = jnp.pad(w_in_t[0, o_a:o_ga], ((0, LANES - rank), (0, 0))).astype(BF16)
    wup = jnp.pad(w_gate_up[0], ((0, LANES - rank), (0, 0))).astype(BF16)
    wc = w_conv_out[0].astype(BF16)
    wg = w_gla_out[0].astype(BF16)
    wo = w_o[0].astype(BF16)
    wd = w_ffn_down[0].astype(BF16)
    g1, g2, gf, gn = norm_mix_g[0][None], norm_ffn_g[0][None], final_norm_g[None], gla_norm_g[0][None]
    bg, fb, cmw, fcw = b_gate[0][None], ffn_conv_b[0][None], conv_mix_w[0], ffn_conv_w[0]

    def in_projections(n, tm, lay, **conv_kw):
        conv_outs = _inproj_conv(n, w_in_t, cmw, lay, tm, cw, **conv_kw)
        lin = _inproj_act(n, w_in_t, spec_qkv, tiles_qkv, tm, tn, q_scale)
        act = _inproj_act(n, w_in_t, spec_act, tiles_act, tm, tn, 1.0)
        return conv_outs, lin, act

    n_short = bs * sl
    rows_s = n_short + n_meta
    xs = jnp.concatenate([x_sample.reshape(n_short, d), meta_tokens.astype(x_sample.dtype)], axis=0)
    lay_s = SeqLayout("table", n_short=n_short, short_len=sl)
    tm_s = rows_s
    n_s, logg_s = _rmsnorm_gate(xs, g1, wa, wup, bg, _row_tile(rows_s, ROW_TILE_RESIDENT))
    (ca_s, conv_s2, conv_s1, conv_tail), lin_s, gates_s = in_projections(
        n_s, tm_s, lay_s, prev=state_conv[0].reshape(bs, (CONV_W - 1) * dc))
    s_zero = jnp.zeros((1, heads, dk, dv), F32)
    ob_m, s_meta = _gla_seq(lin_s, gates_s, logg_s, gn, s_zero, heads, dk, dv, 1, n_meta, n_meta, GLA_SUB,
                            n_short, True)

    rows_p = bp * seq
    xp = x_prompt.reshape(rows_p, d)
    lay_p = SeqLayout("carry", seq_rows=seq)
    tm_p = _row_tile(seq, ROW_TILE)
    tm_o = _row_tile(seq, ROW_TILE_RESIDENT)
    n_p, logg_p = _rmsnorm_gate(xp, g1, wa, wup, bg, _row_tile(seq, ROW_TILE_NORM))
    (ca_p, conv_p), lin_p, gates_p = in_projections(n_p, tm_p, lay_p, init=conv_tail)
    ob_p, s_p, ob_smp, s_smp = _gla_both(
        lin_p, gates_p, logg_p, s_meta, bp, seq, _row_tile(seq, GLA_CHUNK), GLA_SUB,
        lin_s, gates_s, logg_s, state_gla[0], bs, sl, SHORT_SEQS_PER_STEP, gn, heads, dk, dv)

    ob_s = jnp.concatenate([ob_smp, ob_m], axis=0)
    m_s = _merge(ca_s, ob_s, gates_s, wc, wg, tm_s, tn)
    h_s, n2_s = _oproj(m_s, xs, wo, g2, _row_tile(rows_s, ROW_TILE_RESIDENT))
    act_s, ffn_s2, ffn_s1, ffn_tail = _ffn_up(n2_s, w_ffn_up, fcw, fb, lay_s, tm_s, fw,
                                              prev=state_ffn_conv[0].reshape(bs, (CONV_W - 1) * dff))
    y_s = _ffn_down(act_s, wd, h_s, gf, n_short, _row_tile(n_short, ROW_TILE_WIDE_K))

    m_p = _merge(ca_p, ob_p, gates_p, wc, wg, _row_tile(seq, ROW_TILE_MERGE), tn)
    h_p, n2_p = _oproj(m_p, xp, wo, g2, tm_o)
    act_p, ffn_p = _ffn_up(n2_p, w_ffn_up, fcw, fb, lay_p, tm_p, fw, init=ffn_tail)
    y_p = _ffn_down(act_p, wd, h_p, gf, rows_p, _row_tile(seq, ROW_TILE_WIDE_K))

    untile = lambda st: jnp.swapaxes(st, 1, 2).reshape(st.shape[0], CONV_W - 1, -1)
    return (y_p.reshape(bp, seq, d), y_s.reshape(bs, sl, d),
            untile(conv_p)[None], s_p[None], untile(ffn_p)[None],
            jnp.stack([conv_s2, conv_s1], axis=1)[None], s_smp[None], jnp.stack([ffn_s2, ffn_s1], axis=1)[None])
```

```python
import functools

import jax
import jax.numpy as jnp
from jax import lax
from jax.experimental import pallas as pl
from jax.experimental.pallas import tpu as pltpu

EPS = 1e-6
GATE_TAU = 16.0
CONV_W = 3
GLA_SUB = 16
HALO = 8
V7X_VMEM_LIMIT = 56 * 1024 * 1024
LANES = 128
ROW_TILE = 2048
ROW_TILE_RESIDENT = 512
ROW_TILE_MERGE = 1024
ROW_TILE_NORM = 1024
ROW_TILE_WIDE_K = 512
GLA_CHUNK = 128
CONV_COLS = 256
FFN_COLS = 512
SHORT_SEQS_PER_STEP = 2
SUB_ROWS = 256
CAST_ROWS = 128
F32 = jnp.float32
BF16 = jnp.bfloat16


def _cparams(*sem):
    return pltpu.CompilerParams(dimension_semantics=sem, vmem_limit_bytes=V7X_VMEM_LIMIT)


def _row_tile(rows, target):
    best = None
    for t in range(16, min(rows, target) + 1, 16):
        if rows % t == 0:
            best = t
    return best or rows


def _rms(x, g):
    return x * lax.rsqrt(jnp.mean(x * x, axis=-1, keepdims=True) + EPS) * g


def _silu(x):
    h = 0.5 * x
    return h * jnp.tanh(h) + h


def _dot(a, b):
    return jnp.dot(a, b, preferred_element_type=F32)


def _dot_nt(a, b):
    return lax.dot_general(a, b, (((1,), (1,)), ((), ())), preferred_element_type=F32)


def _sliced(tm, matmul, epilogue):
    n = max(1, tm // SUB_ROWS)
    sizes = [SUB_ROWS] * (n - 1) + [tm - SUB_ROWS * (n - 1)]
    r0 = 0
    for rs in sizes:
        epilogue(r0, rs, matmul(r0, rs))
        r0 += rs


def _cast_into(dst_ref, row0, col0, src_ref):
    rows, cols = src_ref.shape

    def body(r, carry):
        off = pl.multiple_of(r * CAST_ROWS, CAST_ROWS)
        dst_ref[pl.ds(row0 + off, CAST_ROWS), col0:col0 + cols] = src_ref[pl.ds(off, CAST_ROWS), :].astype(BF16)
        return carry

    lax.fori_loop(0, rows // CAST_ROWS, body, 0)


def _rmsnorm_gate_kernel(x_ref, g_ref, wa_ref, wup_ref, b_ref, n_ref, lg_ref):
    n = _rms(x_ref[...], g_ref[...]).astype(BF16)
    n_ref[...] = n
    a = _dot_nt(n, wa_ref[...]).astype(BF16)
    z = _dot(a, wup_ref[...]) + b_ref[...]
    lg_ref[...] = (jnp.minimum(z, 0.0) - jnp.log(1.0 + jnp.exp(-jnp.abs(z)))) * (1.0 / GATE_TAU)


def _rmsnorm_gate(x, g, wa, wup, b, tm):
    rows, d = x.shape
    rp, dk = wup.shape
    const = lambda i: (0, 0)
    return pl.pallas_call(
        _rmsnorm_gate_kernel, grid=(rows // tm,),
        in_specs=[pl.BlockSpec((tm, d), lambda i: (i, 0)), pl.BlockSpec((1, d), const),
                  pl.BlockSpec((rp, d), const), pl.BlockSpec((rp, dk), const), pl.BlockSpec((1, dk), const)],
        out_specs=[pl.BlockSpec((tm, d), lambda i: (i, 0)), pl.BlockSpec((tm, dk), lambda i: (i, 0))],
        out_shape=[jax.ShapeDtypeStruct((rows, d), BF16), jax.ShapeDtypeStruct((rows, dk), F32)],
        compiler_params=_cparams("arbitrary"), name="rmsnorm_gate")(x, g, wa, wup, b)


class SeqLayout:
    def __init__(self, mode, seq_rows=None, n_short=None, short_len=None):
        self.mode, self.seq_rows, self.n_short, self.short_len = mode, seq_rows, n_short, short_len


def _load_halo(halo_ref, init_ref, carry_ref, i, tiles_per_seq):
    first = (i % tiles_per_seq) == 0

    @pl.when(first)
    def _():
        halo_ref[...] = init_ref[...]

    @pl.when(jnp.logical_not(first))
    def _():
        halo_ref[...] = carry_ref[...]


def _store_tail(tail, carry_ref, st_ref, i, j, tiles_per_seq):
    carry_ref[...] = tail

    @pl.when((i % tiles_per_seq) == tiles_per_seq - 1)
    def _():
        st_ref[i // tiles_per_seq, j] = tail[HALO - (CONV_W - 1):, :]


def _shift_rows(x, prev_row):
    rows, c = x.shape
    prev8 = jnp.broadcast_to(prev_row, (HALO, c))
    above = jnp.concatenate([prev8, x[:rows - HALO]], axis=0) if rows > HALO else prev8
    last = lax.broadcasted_iota(jnp.int32, (rows, 1), 0) % HALO == HALO - 1
    mixed = jnp.where(last, above, x).reshape(rows // HALO, HALO, c)
    return pltpu.roll(mixed, 1, 1).reshape(rows, c)


class _CausalConv:
    def __init__(self, w_ref, lay, halo, prev2_ref=None, prev1_ref=None):
        w = w_ref[...]
        self.w0, self.w1, self.w2 = w[0:1, :], w[1:2, :], w[2:3, :]
        self.lay, self.tail, self.prev2_ref, self.prev1_ref = lay, halo, prev2_ref, prev1_ref

    def __call__(self, u, r0):
        rs, c = u.shape
        lay, w0, w1 = self.lay, self.w0, self.w1
        um2, um1 = self.tail[HALO - 2:HALO - 1, :], self.tail[HALO - 1:HALO, :]
        s0 = _shift_rows(w0 * u, w0 * um1)
        s1_first = w1 * um1 + w0 * um2
        if lay.mode == "table":
            take = max(0, min(lay.n_short, r0 + rs) - r0)

            def table_rows(ref):
                parts = ([_lane_chunks_load(ref, r0, take)] if take else []) + (
                    [jnp.zeros((rs - take, c), F32)] if take < rs else [])
                return parts[0] if len(parts) == 1 else jnp.concatenate(parts, axis=0)

            p2, p1 = table_rows(self.prev2_ref), table_rows(self.prev1_ref)
            t = r0 + lax.broadcasted_iota(jnp.int32, (rs, 1), 0)
            starts = jnp.where(t < lay.n_short, t % lay.short_len, t - lay.n_short) == 0
            s0 = jnp.where(starts, w0 * p1, s0)
        a = w1 * u + s0
        s1 = _shift_rows(a, s1_first)
        if lay.mode == "table":
            s1 = jnp.where(starts, w1 * p1 + w0 * p2, s1)
        self.tail = u[rs - HALO:, :]
        return self.w2 * u + s1


def _conv_specs(lay, rows, tm, c, nj, init, prev):
    col = lambda j, i: (0, j)
    if lay.mode == "carry":
        nseq = rows // lay.seq_rows
        return ((init,), [pl.BlockSpec((HALO, c), col)],
                [pl.BlockSpec((nseq, nj, CONV_W - 1, c), lambda j, i: (0, 0, 0, 0))],
                [jax.ShapeDtypeStruct((nseq, nj, CONV_W - 1, c), F32)],
                [pltpu.VMEM((HALO, c), F32)] * 2)
    assert tm == rows and lay.short_len >= CONV_W - 1
    nseq = lay.n_short // lay.short_len
    return ((prev, prev), [pl.BlockSpec((nseq, c), col), pl.BlockSpec((nseq, c), lambda j, i: (0, nj + j))],
            [pl.BlockSpec((nseq, c), col)] * 2 + [pl.BlockSpec((HALO, c), col)],
            [jax.ShapeDtypeStruct((nseq, nj * c), F32)] * 2 + [jax.ShapeDtypeStruct((HALO, nj * c), F32)],
            [pltpu.VMEM((c // LANES, lay.n_short, LANES), F32)] * 2 + [pltpu.VMEM((c // LANES, tm, LANES), F32)])


def _lane_chunks_load(ref, r0, rows):
    return jnp.concatenate([ref[k, r0:r0 + rows, :] for k in range(ref.shape[0])], axis=1)


def _lane_chunks_store(ref, r0, x):
    for k in range(ref.shape[0]):
        ref[k, r0:r0 + x.shape[0], :] = x[:, k * LANES:(k + 1) * LANES]


def _fill_tables(lay, tab2_ref, tab1_ref, prev2_ref, prev1_ref):
    nseq = lay.n_short // lay.short_len
    first_rows = pl.ds(0, nseq, stride=lay.short_len)
    for tab_ref, prev_ref in ((tab2_ref, prev2_ref), (tab1_ref, prev1_ref)):
        tab_ref[...] = jnp.zeros(tab_ref.shape, F32)
        for k in range(tab_ref.shape[0]):
            tab_ref[k, first_rows, :] = prev_ref[:, k * LANES:(k + 1) * LANES]


def _store_short_states(lay, rows_ref, tail, last2_ref, last1_ref, tail_ref):
    nseq = lay.n_short // lay.short_len
    for k in range(rows_ref.shape[0]):
        cols = slice(k * LANES, (k + 1) * LANES)
        last2_ref[:, cols] = rows_ref[k, pl.ds(lay.short_len - 2, nseq, stride=lay.short_len), :]
        last1_ref[:, cols] = rows_ref[k, pl.ds(lay.short_len - 1, nseq, stride=lay.short_len), :]
    tail_ref[...] = tail


def _inproj_conv_kernel(lay, tm, cw, tiles_per_seq, *refs):
    n_ref, wb_ref, wc_ref, wh_ref, cwt_ref = refs[:5]
    if lay.mode == "carry":
        init_ref, ca_ref, st_ref, wbf_ref, halo_ref, carry_ref = refs[5:]
    else:
        prev2_ref, prev1_ref, ca_ref, last2_ref, last1_ref, tail_ref, wbf_ref, tab2_ref, tab1_ref, u_ref = refs[5:]
    j, i = pl.program_id(0), pl.program_id(1)

    @pl.when(i == 0)
    def _():
        for part, w_ref in enumerate((wb_ref, wc_ref, wh_ref)):
            _cast_into(wbf_ref, part * cw, 0, w_ref)

    if lay.mode == "carry":
        _load_halo(halo_ref, init_ref, carry_ref, i, tiles_per_seq)
        conv = _CausalConv(cwt_ref, lay, halo_ref[...])
    else:
        _fill_tables(lay, tab2_ref, tab1_ref, prev2_ref, prev1_ref)
        conv = _CausalConv(cwt_ref, lay, jnp.zeros((HALO, cw), F32), tab2_ref, tab1_ref)

    def matmul(r0, rs):
        return _dot_nt(n_ref[r0:r0 + rs, :], wbf_ref[...])

    def epilogue(r0, rs, p):
        u = p[:, cw:2 * cw] * p[:, 2 * cw:]
        if lay.mode == "table":
            _lane_chunks_store(u_ref, r0, u)
        ca_ref[r0:r0 + rs, :] = (p[:, :cw] * conv(u, r0)).astype(BF16)

    _sliced(tm, matmul, epilogue)
    if lay.mode == "carry":
        _store_tail(conv.tail, carry_ref, st_ref, i, j, tiles_per_seq)
    else:
        _store_short_states(lay, u_ref, conv.tail, last2_ref, last1_ref, tail_ref)


def _inproj_conv(n, w_in_t, conv_w, lay, tm, cw, init=None, prev=None):
    rows, d = n.shape
    dc = conv_w.shape[1]
    nj = dc // cw
    tiles_per_seq = (lay.seq_rows // tm) if lay.mode == "carry" else 1
    xargs, xspecs, xout_specs, xout_shapes, xscratch = _conv_specs(lay, rows, tm, cw, nj, init, prev)
    wspec = lambda part: pl.BlockSpec((None, cw, d), lambda j, i: (0, part * nj + j, 0))
    return pl.pallas_call(
        functools.partial(_inproj_conv_kernel, lay, tm, cw, tiles_per_seq),
        grid=(nj, rows // tm),
        in_specs=[pl.BlockSpec((tm, d), lambda j, i: (i, 0)), wspec(0), wspec(1), wspec(2),
                  pl.BlockSpec((CONV_W, cw), lambda j, i: (0, j))] + xspecs,
        out_specs=[pl.BlockSpec((tm, cw), lambda j, i: (i, j))] + xout_specs,
        out_shape=[jax.ShapeDtypeStruct((rows, dc), BF16)] + xout_shapes,
        scratch_shapes=[pltpu.VMEM((3 * cw, d), BF16)] + xscratch,
        compiler_params=_cparams("arbitrary", "arbitrary"),
        name="inproj_conv_" + lay.mode)(n, w_in_t, w_in_t, w_in_t, conv_w, *xargs)


def _inproj_act_kernel(tm, n_lin, n_silu, q_scale, n_ref, w_ref, o_ref, wbf_ref):
    j, i = pl.program_id(0), pl.program_id(1)

    @pl.when(i == 0)
    def _():
        _cast_into(wbf_ref, 0, 0, w_ref.at[0] if len(w_ref.shape) == 3 else w_ref)

    def matmul(r0, rs):
        return _dot_nt(n_ref[r0:r0 + rs, :], wbf_ref[...])

    def epilogue(r0, rs, p):
        if n_lin:
            act = p * jnp.where(j == 0, q_scale, 1.0)
        else:
            h = 0.5 * p
            factor = jnp.where(j < n_silu, h, 0.5) if n_silu else 0.5
            act = factor * (jnp.tanh(h) + 1.0)
        o_ref[r0:r0 + rs, :] = act.astype(BF16)

    _sliced(tm, matmul, epilogue)


def _inproj_act(n, w_t, w_spec, tiles, tm, tn, q_scale):
    rows, d = n.shape
    n_lin, n_silu, n_sig = tiles
    assert not (n_lin and (n_silu or n_sig))
    n_col = n_lin + n_silu + n_sig
    return pl.pallas_call(
        functools.partial(_inproj_act_kernel, tm, n_lin, n_silu, q_scale),
        grid=(n_col, rows // tm),
        in_specs=[pl.BlockSpec((tm, d), lambda j, i: (i, 0)), w_spec],
        out_specs=pl.BlockSpec((tm, tn), lambda j, i: (i, j)),
        out_shape=jax.ShapeDtypeStruct((rows, n_col * tn), BF16),
        scratch_shapes=[pltpu.VMEM((tn, d), BF16)],
        compiler_params=_cparams("arbitrary", "arbitrary"), name="inproj_act")(n, w_t)


def _cumsum_groups(x, sub):
    rows = x.shape[0]
    pos = lax.broadcasted_iota(jnp.int32, (rows, 1), 0) % sub
    s = 1
    while s < sub:
        x = x + jnp.where(pos >= s, pltpu.roll(x, s, 0), 0.0)
        s *= 2
    return x


def _gla_chunk(q, k, v, lg, s, sub):
    c, dk = q.shape
    ngrp = c // sub
    gpu = 2 if ngrp % 2 == 0 else 1
    unit = gpu * sub
    bt = _cumsum_groups(lg, sub)
    r = jnp.zeros((1, dk), F32)
    r_grp, b_rows = [], []
    for g in range(ngrp):
        r_grp.append(r)
        b_rows.append(bt[g * sub:(g + 1) * sub, :] + r)
        r = r + bt[(g + 1) * sub - 1:(g + 1) * sub, :]
    b = jnp.concatenate(b_rows, axis=0) if ngrp > 1 else b_rows[0]
    b_last = r
    qe = (q * jnp.exp(b)).astype(BF16)
    kd = (k * jnp.exp(b_last - b)).astype(BF16)
    att_rows = []
    for i in range(c // unit):
        ref = r_grp[i * gpu + gpu - 1]
        rows = slice(i * unit, (i + 1) * unit)
        seen = (i + 1) * unit
        qt = (q[rows] * jnp.exp(b[rows] - ref)).astype(BF16)
        ke = (k[:seen] * jnp.exp(ref - b[:seen])).astype(BF16)
        if seen < c:
            ke = jnp.concatenate([ke, jnp.zeros((c - seen, dk), BF16)], axis=0)
        a = _dot_nt(qt, ke)
        col = lax.broadcasted_iota(jnp.int32, (unit, c), 1)
        rloc = lax.broadcasted_iota(jnp.int32, (unit, c), 0)
        att_rows.append(jnp.where(col <= rloc + i * unit, a, 0.0))
    att = (jnp.concatenate(att_rows, axis=0) if len(att_rows) > 1 else att_rows[0]).astype(BF16)
    o = _dot(att, v) + _dot(qe, s.astype(BF16))
    upd = lax.dot_general(kd, v, (((0,), (0,)), ((), ())), preferred_element_type=F32)
    dl = jnp.exp(b_last)
    dl_col = jnp.transpose(jnp.broadcast_to(dl, (LANES, dk)))
    dv = s.shape[1]
    s_dec = jnp.concatenate([s[:, n * LANES:(n + 1) * LANES] * dl_col for n in range(dv // LANES)], axis=1)
    return o, s_dec + upd


def _gla_out(o, gn, gs):
    return (_rms(o, gn) * gs.astype(F32)).astype(BF16)


def _gla_seq_kernel(sub, heads, dk, dv, *refs):
    _gla_seq_body(sub, heads, dk, dv, pl.program_id(1) == 0, *refs)


def _gla_seq_body(sub, heads, dk, dv, first_chunk, q_ref, k_ref, v_ref, lg_ref, gs_ref, gn_ref, s0_ref, ob_ref, s_ref):
    @pl.when(first_chunk)
    def _():
        def copy_head(h, carry):
            s_ref[0, h] = s0_ref[0, h]
            return carry

        lax.fori_loop(0, heads, copy_head, 0)

    for h in range(heads):
        _gla_seq_head(sub, dk, dv, h, q_ref, k_ref, v_ref, lg_ref, gs_ref, gn_ref, ob_ref, s_ref)


def _gla_seq_head(sub, dk, dv, h, q_ref, k_ref, v_ref, lg_ref, gs_ref, gn_ref, ob_ref, s_ref):
    ck, cv = slice(h * dk, (h + 1) * dk), slice(h * dv, (h + 1) * dv)
    o, s_new = _gla_chunk(q_ref[:, ck].astype(F32), k_ref[:, ck].astype(F32), v_ref[:, cv], lg_ref[:, ck],
                          s_ref[0, h], sub)
    s_ref[0, h] = s_new
    ob_ref[:, cv] = _gla_out(o, gn_ref[...], gs_ref[:, cv])


def _gla_seq(lin, act, logg, gn, s0, heads, dk, dv, nseq, seq_rows, chunk, sub, row0, shared_init):
    nchunk = seq_rows // chunk
    blk0 = row0 // chunk
    rb = lambda b, c: blk0 + b * nchunk + c
    wk, wv = heads * dk, heads * dv
    assert (2 * wk) % wv == 0
    return pl.pallas_call(
        functools.partial(_gla_seq_kernel, sub, heads, dk, dv),
        grid=(nseq, nchunk),
        in_specs=[pl.BlockSpec((chunk, wk), lambda b, c: (rb(b, c), 0)),
                  pl.BlockSpec((chunk, wk), lambda b, c: (rb(b, c), 1)),
                  pl.BlockSpec((chunk, wv), lambda b, c: (rb(b, c), 2 * wk // wv)),
                  pl.BlockSpec((chunk, wk), lambda b, c: (rb(b, c), 0)),
                  pl.BlockSpec((chunk, wv), lambda b, c: (rb(b, c), 0)),
                  pl.BlockSpec((1, dv), lambda b, c: (0, 0)),
                  pl.BlockSpec((1, heads, dk, dv), lambda b, c: (0 if shared_init else b, 0, 0, 0))],
        out_specs=[pl.BlockSpec((chunk, wv), lambda b, c: (b * nchunk + c, 0)),
                   pl.BlockSpec((1, heads, dk, dv), lambda b, c: (b, 0, 0, 0))],
        out_shape=[jax.ShapeDtypeStruct((nseq * seq_rows, wv), BF16),
                   jax.ShapeDtypeStruct((nseq, heads, dk, dv), F32)],
        compiler_params=_cparams("arbitrary", "arbitrary"),
        name="gla_seq")(lin, lin, lin, logg, act, gn, s0)


class _ShortGla:
    def __init__(self, nb, sl, dk, dv, q_ref, k_ref, v_ref, lg_ref, gs_ref, gn_ref, s0_ref, ob_ref, s_ref):
        self.nb, self.sl, self.dk, self.dv = nb, sl, dk, dv
        self.q, self.k, self.v = q_ref[...].astype(F32), k_ref[...].astype(F32), v_ref[...].astype(F32)
        self.lg, self.gs_ref, self.gn_ref = lg_ref[...], gs_ref, gn_ref
        self.s0_ref, self.ob_ref, self.s_ref = s0_ref, ob_ref, s_ref

    def head(self, h):
        ck, cv = slice(h * self.dk, (h + 1) * self.dk), slice(h * self.dv, (h + 1) * self.dv)
        outs = []
        for n in range(self.nb):
            rs = slice(n * self.sl, (n + 1) * self.sl)
            o, s_new = _gla_chunk(self.q[rs, ck], self.k[rs, ck], self.v[rs, cv].astype(BF16), self.lg[rs, ck],
                                  self.s0_ref[n, h], self.sl)
            self.s_ref[n, h] = s_new
            outs.append(o)
        o = jnp.concatenate(outs, axis=0) if self.nb > 1 else outs[0]
        self.ob_ref[:, cv] = _gla_out(o, self.gn_ref[...], self.gs_ref[:, cv])


def _gla_both_kernel(sub, heads, dk, dv, nchunk, nb, sl, n_long, n_short, *refs):
    (ql, kl, vl, lgl, gsl, gn, s0l, qs, ks, vs, lgs, gss, s0s, obl, sl_out, obs, ss_out) = refs
    step = pl.program_id(0)

    def long_part():
        _gla_seq_body(sub, heads, dk, dv, (step % nchunk) == 0, ql, kl, vl, lgl, gsl, gn, s0l, obl, sl_out)

    def short_part():
        short = _ShortGla(nb, sl, dk, dv, qs, ks, vs, lgs, gss, gn, s0s, obs, ss_out)
        for h in range(heads):
            short.head(h)

    if n_long == n_short:
        @pl.when((step % nchunk) == 0)
        def _():
            def copy_head(h, carry):
                sl_out[0, h] = s0l[0, h]
                return carry

            lax.fori_loop(0, heads, copy_head, 0)

        short = _ShortGla(nb, sl, dk, dv, qs, ks, vs, lgs, gss, gn, s0s, obs, ss_out)
        for h in range(heads):
            _gla_seq_head(sub, dk, dv, h, ql, kl, vl, lgl, gsl, gn, obl, sl_out)
            short.head(h)
    else:
        pl.when(step < n_long)(long_part)
        pl.when(step < n_short)(short_part)


def _gla_both(lin_l, act_l, logg_l, s0_l, nseq_l, seq_rows, chunk, sub,
              lin_s, act_s, logg_s, s0_s, nseq_s, sl, nb, gn, heads, dk, dv):
    nchunk = seq_rows // chunk
    n_long, n_short = nseq_l * nchunk, nseq_s // nb
    steps = max(n_long, n_short)
    wk, wv = heads * dk, heads * dv
    assert (2 * wk) % wv == 0
    rows = nb * sl
    lo = (lambda s: s) if n_long == steps else (lambda s: jnp.minimum(s, n_long - 1))
    sh = (lambda s: s) if n_short == steps else (lambda s: jnp.minimum(s, n_short - 1))
    long_row = lambda col: (lambda s: (lo(s), col))
    short_row = lambda col: (lambda s: (sh(s), col))
    short_state = pl.BlockSpec((nb, heads, dk, dv), lambda s: (sh(s), 0, 0, 0))
    return pl.pallas_call(
        functools.partial(_gla_both_kernel, sub, heads, dk, dv, nchunk, nb, sl, n_long, n_short),
        grid=(steps,),
        in_specs=[pl.BlockSpec((chunk, wk), long_row(0)), pl.BlockSpec((chunk, wk), long_row(1)),
                  pl.BlockSpec((chunk, wv), long_row(2 * wk // wv)), pl.BlockSpec((chunk, wk), long_row(0)),
                  pl.BlockSpec((chunk, wv), long_row(0)),
                  pl.BlockSpec((1, dv), lambda s: (0, 0)),
                  pl.BlockSpec((1, heads, dk, dv), lambda s: (0, 0, 0, 0)),
                  pl.BlockSpec((rows, wk), short_row(0)), pl.BlockSpec((rows, wk), short_row(1)),
                  pl.BlockSpec((rows, wv), short_row(2 * wk // wv)), pl.BlockSpec((rows, wk), short_row(0)),
                  pl.BlockSpec((rows, wv), short_row(0)),
                  short_state],
        out_specs=[pl.BlockSpec((chunk, wv), long_row(0)),
                   pl.BlockSpec((1, heads, dk, dv), lambda s: (lo(s) // nchunk, 0, 0, 0)),
                   pl.BlockSpec((rows, wv), short_row(0)),
                   short_state],
        out_shape=[jax.ShapeDtypeStruct((nseq_l * seq_rows, wv), BF16),
                   jax.ShapeDtypeStruct((nseq_l, heads, dk, dv), F32),
                   jax.ShapeDtypeStruct((nseq_s * sl, wv), BF16),
                   jax.ShapeDtypeStruct((nseq_s, heads, dk, dv), F32)],
        compiler_params=_cparams("arbitrary"),
        name="gla_both")(lin_l, lin_l, lin_l, logg_l, act_l, gn, s0_l, lin_s, lin_s, lin_s, logg_s, act_s, s0_s)


def _merge_kernel(ca_ref, ob_ref, ga_ref, gb_ref, wc_ref, wg_ref, o_ref):
    ya = _dot(ca_ref[...], wc_ref[...])
    yb = _dot(ob_ref[...], wg_ref[...])
    o_ref[...] = (ga_ref[...].astype(F32) * ya + gb_ref[...].astype(F32) * yb).astype(BF16)


def _merge(ca, ob, act, wc, wg, tm, tn):
    rows, dc = ca.shape
    dg, d = wg.shape
    nn = d // tn
    g0 = (act.shape[1] - 2 * d) // tn
    return pl.pallas_call(
        _merge_kernel, grid=(rows // tm, nn),
        in_specs=[pl.BlockSpec((tm, dc), lambda i, j: (i, 0)), pl.BlockSpec((tm, dg), lambda i, j: (i, 0)),
                  pl.BlockSpec((tm, tn), lambda i, j: (i, g0 + j)),
                  pl.BlockSpec((tm, tn), lambda i, j: (i, g0 + nn + j)),
                  pl.BlockSpec((dc, tn), lambda i, j: (0, j)), pl.BlockSpec((dg, tn), lambda i, j: (0, j))],
        out_specs=pl.BlockSpec((tm, tn), lambda i, j: (i, j)),
        out_shape=jax.ShapeDtypeStruct((rows, d), BF16),
        compiler_params=_cparams("arbitrary", "arbitrary"), name="merge")(ca, ob, act, act, wc, wg)


def _oproj_kernel(m_ref, x_ref, w_ref, g_ref, h_ref, n2_ref):
    h = x_ref[...] + _dot(m_ref[...], w_ref[...])
    h_ref[...] = h
    n2_ref[...] = _rms(h, g_ref[...]).astype(BF16)


def _oproj(m, x, w, g, tm):
    rows, d = x.shape
    row = lambda i: (i, 0)
    return pl.pallas_call(
        _oproj_kernel, grid=(rows // tm,),
        in_specs=[pl.BlockSpec((tm, d), row), pl.BlockSpec((tm, d), row),
                  pl.BlockSpec((d, d), lambda i: (0, 0)), pl.BlockSpec((1, d), lambda i: (0, 0))],
        out_specs=[pl.BlockSpec((tm, d), row), pl.BlockSpec((tm, d), row)],
        out_shape=[jax.ShapeDtypeStruct((rows, d), F32), jax.ShapeDtypeStruct((rows, d), BF16)],
        compiler_params=_cparams("arbitrary"), name="oproj")(m, x, w, g)


def _ffn_up_kernel(lay, tm, fw, tiles_per_seq, *refs):
    n_ref, wa_ref, wg_ref, cwt_ref, b_ref = refs[:5]
    if lay.mode == "carry":
        init_ref, act_ref, st_ref, wbf_ref, halo_ref, carry_ref = refs[5:]
    else:
        prev2_ref, prev1_ref, act_ref, last2_ref, last1_ref, tail_ref, wbf_ref, tab2_ref, tab1_ref, gt_ref = refs[5:]
    j, i = pl.program_id(0), pl.program_id(1)

    @pl.when(i == 0)
    def _():
        _cast_into(wbf_ref, 0, 0, wa_ref)
        _cast_into(wbf_ref, 0, fw, wg_ref)

    if lay.mode == "carry":
        _load_halo(halo_ref, init_ref, carry_ref, i, tiles_per_seq)
        conv = _CausalConv(cwt_ref, lay, halo_ref[...])
    else:
        _fill_tables(lay, tab2_ref, tab1_ref, prev2_ref, prev1_ref)
        conv = _CausalConv(cwt_ref, lay, jnp.zeros((HALO, fw), F32), tab2_ref, tab1_ref)

    def matmul(r0, rs):
        return _dot(n_ref[r0:r0 + rs, :], wbf_ref[...])

    def epilogue(r0, rs, p):
        gt = p[:, fw:]
        if lay.mode == "table":
            _lane_chunks_store(gt_ref, r0, gt)
        z = conv(gt, r0) + b_ref[...]
        act_ref[r0:r0 + rs, :] = (_silu(z) * p[:, :fw]).astype(BF16)

    _sliced(tm, matmul, epilogue)
    if lay.mode == "carry":
        _store_tail(conv.tail, carry_ref, st_ref, i, j, tiles_per_seq)
    else:
        _store_short_states(lay, gt_ref, conv.tail, last2_ref, last1_ref, tail_ref)


def _ffn_up(n2, w_up, conv_w, bias, lay, tm, fw, init=None, prev=None):
    rows, d = n2.shape
    dff = conv_w.shape[1]
    nj = dff // fw
    tiles_per_seq = (lay.seq_rows // tm) if lay.mode == "carry" else 1
    xargs, xspecs, xout_specs, xout_shapes, xscratch = _conv_specs(lay, rows, tm, fw, nj, init, prev)
    wspec = lambda part: pl.BlockSpec((None, d, fw), lambda j, i: (0, 0, part * nj + j))
    return pl.pallas_call(
        functools.partial(_ffn_up_kernel, lay, tm, fw, tiles_per_seq),
        grid=(nj, rows // tm),
        in_specs=[pl.BlockSpec((tm, d), lambda j, i: (i, 0)), wspec(0), wspec(1),
                  pl.BlockSpec((CONV_W, fw), lambda j, i: (0, j)), pl.BlockSpec((1, fw), lambda j, i: (0, j))]
                 + xspecs,
        out_specs=[pl.BlockSpec((tm, fw), lambda j, i: (i, j))] + xout_specs,
        out_shape=[jax.ShapeDtypeStruct((rows, dff), BF16)] + xout_shapes,
        scratch_shapes=[pltpu.VMEM((d, 2 * fw), BF16)] + xscratch,
        compiler_params=_cparams("arbitrary", "arbitrary"),
        name="ffn_up_" + lay.mode)(n2, w_up, w_up, conv_w, bias, *xargs)


def _ffn_down_kernel(act_ref, w_ref, h_ref, g_ref, y_ref):
    y_ref[...] = _rms(h_ref[...] + _dot(act_ref[...], w_ref[...]), g_ref[...])


def _ffn_down(act, w, h, g, rows, tm):
    dff = act.shape[1]
    d = w.shape[1]
    row = lambda i: (i, 0)
    return pl.pallas_call(
        _ffn_down_kernel, grid=(rows // tm,),
        in_specs=[pl.BlockSpec((tm, dff), row),
                  pl.BlockSpec((dff, d), lambda i: (0, 0), pipeline_mode=pl.Buffered(1)),
                  pl.BlockSpec((tm, d), row), pl.BlockSpec((1, d), lambda i: (0, 0))],
        out_specs=pl.BlockSpec((tm, d), row),
        out_shape=jax.ShapeDtypeStruct((rows, d), F32),
        compiler_params=_cparams("arbitrary"), name="ffn_down")(act, w, h, g)


def kernel(x_prompt, x_sample, state_conv, state_gla, state_ffn_conv, meta_tokens, norm_mix_g, w_in, conv_mix_w, w_conv_out, w_gate_up, b_gate, gla_norm_g, w_gla_out, w_o, norm_ffn_g, w_ffn_up, ffn_conv_w, ffn_conv_b, w_ffn_down, final_norm_g):
    bp, seq, d = x_prompt.shape
    bs, sl, _ = x_sample.shape
    assert w_in.shape[0] == 1, "single-layer step"
    n_meta = meta_tokens.shape[0]
    dc = state_conv.shape[-1]
    _, _, heads, dk, dv = state_gla.shape
    dff = state_ffn_conv.shape[-1]
    rank = w_gate_up.shape[1]
    assert n_meta % GLA_SUB == 0 and seq % GLA_SUB == 0 and GLA_SUB % sl == 0 and sl >= CONV_W - 1

    o_q = 3 * dc
    o_a = o_q + 2 * heads * dk + 2 * heads * dv
    o_ga = o_a + rank
    tn = heads * dk
    assert o_q % tn == 0 and (heads * dv) % tn == 0 and d % tn == 0
    cw, fw = CONV_COLS, FFN_COLS
    o_g = o_a - heads * dv
    n_g = heads * dv // tn
    tiles_qkv = ((o_g - o_q) // tn, 0, 0)
    tiles_act = (0, n_g, 2 * d // tn)
    spec_qkv = pl.BlockSpec((None, tn, d), lambda j, i: (0, o_q // tn + j, 0))
    assert o_g % 8 == 0 and o_ga % 8 == 0 and tn % 8 == 0
    act_row8 = lambda j: jnp.where(j < n_g, o_g // 8 + j * (tn // 8), o_ga // 8 + (j - n_g) * (tn // 8))
    spec_act = pl.BlockSpec((pl.Element(1), pl.Element(tn), pl.Element(d)), lambda j, i: (0, act_row8(j) * 8, 0))
    q_scale = float(dk) ** -0.5

    w_in_t = jnp.swapaxes(w_in, 1, 2)
    wa = jnp.pad(w_in_t[0, o_a:o_ga], ((0, LANES - rank), (0, 0))).astype(BF16)
    wup = jnp.pad(w_gate_up[0], ((0, LANES - rank), (0, 0))).astype(BF16)
    wc = w_conv_out[0].astype(BF16)
    wg = w_gla_out[0].astype(BF16)
    wo = w_o[0].astype(BF16)
    wd = w_ffn_down[0].astype(BF16)
    g1, g2, gf, gn = norm_mix_g[0][None], norm_ffn_g[0][None], final_norm_g[None], gla_norm_g[0][None]
    bg, fb, cmw, fcw = b_gate[0][None], ffn_conv_b[0][None], conv_mix_w[0], ffn_conv_w[0]

    def in_projections(n, tm, lay, **conv_kw):
        conv_outs = _inproj_conv(n, w_in_t, cmw, lay, tm, cw, **conv_kw)
        lin = _inproj_act(n, w_in_t, spec_qkv, tiles_qkv, tm, tn, q_scale)
        act = _inproj_act(n, w_in_t, spec_act, tiles_act, tm, tn, 1.0)
        return conv_outs, lin, act

    n_short = bs * sl
    rows_s = n_short + n_meta
    xs = jnp.concatenate([x_sample.reshape(n_short, d), meta_tokens.astype(x_sample.dtype)], axis=0)
    lay_s = SeqLayout("table", n_short=n_short, short_len=sl)
    tm_s = rows_s
    n_s, logg_s = _rmsnorm_gate(xs, g1, wa, wup, bg, _row_tile(rows_s, ROW_TILE_RESIDENT))
    (ca_s, conv_s2, conv_s1, conv_tail), lin_s, gates_s = in_projections(
        n_s, tm_s, lay_s, prev=state_conv[0].reshape(bs, (CONV_W - 1) * dc))
    s_zero = jnp.zeros((1, heads, dk, dv), F32)
    ob_m, s_meta = _gla_seq(lin_s, gates_s, logg_s, gn, s_zero, heads, dk, dv, 1, n_meta, n_meta, GLA_SUB,
                            n_short, True)

    rows_p = bp * seq
    xp = x_prompt.reshape(rows_p, d)
    lay_p = SeqLayout("carry", seq_rows=seq)
    tm_p = _row_tile(seq, ROW_TILE)
    tm_o = _row_tile(seq, ROW_TILE_RESIDENT)
    n_p, logg_p = _rmsnorm_gate(xp, g1, wa, wup, bg, _row_tile(seq, ROW_TILE_NORM))
    (ca_p, conv_p), lin_p, gates_p = in_projections(n_p, tm_p, lay_p, init=conv_tail)
    ob_p, s_p, ob_smp, s_smp = _gla_both(
        lin_p, gates_p, logg_p, s_meta, bp, seq, _row_tile(seq, GLA_CHUNK), GLA_SUB,
        lin_s, gates_s, logg_s, state_gla[0], bs, sl, SHORT_SEQS_PER_STEP, gn, heads, dk, dv)

    ob_s = jnp.concatenate([ob_smp, ob_m], axis=0)
    m_s = _merge(ca_s, ob_s, gates_s, wc, wg, tm_s, tn)
    h_s, n2_s = _oproj(m_s, xs, wo, g2, _row_tile(rows_s, ROW_TILE_RESIDENT))
    act_s, ffn_s2, ffn_s1, ffn_tail = _ffn_up(n2_s, w_ffn_up, fcw, fb, lay_s, tm_s, fw,
                                              prev=state_ffn_conv[0].reshape(bs, (CONV_W - 1) * dff))
    y_s = _ffn_down(act_s, wd, h_s, gf, n_short, _row_tile(n_short, ROW_TILE_WIDE_K))

    m_p = _merge(ca_p, ob_p, gates_p, wc, wg, _row_tile(seq, ROW_TILE_MERGE), tn)
    h_p, n2_p = _oproj(m_p, xp, wo, g2, tm_o)
    act_p, ffn_p = _ffn_up(n2_p, w_ffn_up, fcw, fb, lay_p, tm_p, fw, init=ffn_tail)
    y_p = _ffn_down(act_p, wd, h_p, gf, rows_p, _row_tile(seq, ROW_TILE_WIDE_K))

    untile = lambda st: jnp.swapaxes(st, 1, 2).reshape(st.shape[0], CONV_W - 1, -1)
    return (y_p.reshape(bp, seq, d), y_s.reshape(bs, sl, d),
            untile(conv_p)[None], s_p[None], untile(ffn_p)[None],
            jnp.stack([conv_s2, conv_s1], axis=1)[None], s_smp[None], jnp.stack([ffn_s2, ffn_s1], axis=1)[None])
```

```python
import functools

import jax
import jax.numpy as jnp
from jax import lax
from jax.experimental import pallas as pl
from jax.experimental.pallas import tpu as pltpu

EPS = 1e-6
GATE_TAU = 16.0
CONV_W = 3
GLA_SUB = 16
HALO = 8
V7X_VMEM_LIMIT = 56 * 1024 * 1024
LANES = 128
ROW_TILE = 2048
ROW_TILE_RESIDENT = 512
ROW_TILE_MERGE = 1024
ROW_TILE_NORM = 1024
ROW_TILE_WIDE_K = 256
GLA_CHUNK = 128
CONV_COLS = 256
FFN_COLS = 512
SHORT_SEQS_PER_STEP = 8
SUB_ROWS = 256
CAST_ROWS = 128
F32 = jnp.float32
BF16 = jnp.bfloat16


def _cparams(*sem):
    return pltpu.CompilerParams(dimension_semantics=sem, vmem_limit_bytes=V7X_VMEM_LIMIT)


def _row_tile(rows, target):
    best = None
    for t in range(16, min(rows, target) + 1, 16):
        if rows % t == 0:
            best = t
    return best or rows


def _rms(x, g):
    return x * lax.rsqrt(jnp.mean(x * x, axis=-1, keepdims=True) + EPS) * g


def _sigmoid(x):
    return 0.5 * jnp.tanh(0.5 * x) + 0.5


def _silu(x):
    h = 0.5 * x
    return h * jnp.tanh(h) + h


def _dot(a, b):
    return jnp.dot(a, b, preferred_element_type=F32)


def _dot_nt(a, b):
    return lax.dot_general(a, b, (((1,), (1,)), ((), ())), preferred_element_type=F32)


def _sliced(tm, matmul, epilogue):
    n = max(1, tm // SUB_ROWS)
    sizes = [SUB_ROWS] * (n - 1) + [tm - SUB_ROWS * (n - 1)]
    r0 = 0
    for rs in sizes:
        epilogue(r0, rs, matmul(r0, rs))
        r0 += rs


def _cast_into(dst_ref, row0, col0, src_ref):
    rows, cols = src_ref.shape

    def body(r, carry):
        off = pl.multiple_of(r * CAST_ROWS, CAST_ROWS)
        dst_ref[pl.ds(row0 + off, CAST_ROWS), col0:col0 + cols] = src_ref[pl.ds(off, CAST_ROWS), :].astype(BF16)
        return carry

    lax.fori_loop(0, rows // CAST_ROWS, body, 0)


def _rmsnorm_gate_kernel(x_ref, g_ref, wa_ref, wup_ref, b_ref, n_ref, lg_ref):
    n = _rms(x_ref[...], g_ref[...]).astype(BF16)
    n_ref[...] = n
    a = _dot_nt(n, wa_ref[...]).astype(BF16)
    z = _dot(a, wup_ref[...]) + b_ref[...]
    lg_ref[...] = (jnp.minimum(z, 0.0) - jnp.log(1.0 + jnp.exp(-jnp.abs(z)))) * (1.0 / GATE_TAU)


def _rmsnorm_gate(x, g, wa, wup, b, tm):
    rows, d = x.shape
    rp, dk = wup.shape
    const = lambda i: (0, 0)
    return pl.pallas_call(
        _rmsnorm_gate_kernel, grid=(rows // tm,),
        in_specs=[pl.BlockSpec((tm, d), lambda i: (i, 0)), pl.BlockSpec((1, d), const),
                  pl.BlockSpec((rp, d), const), pl.BlockSpec((rp, dk), const), pl.BlockSpec((1, dk), const)],
        out_specs=[pl.BlockSpec((tm, d), lambda i: (i, 0)), pl.BlockSpec((tm, dk), lambda i: (i, 0))],
        out_shape=[jax.ShapeDtypeStruct((rows, d), BF16), jax.ShapeDtypeStruct((rows, dk), F32)],
        compiler_params=_cparams("arbitrary"), name="rmsnorm_gate")(x, g, wa, wup, b)


class SeqLayout:
    def __init__(self, mode, seq_rows=None, n_short=None, short_len=None):
        self.mode, self.seq_rows, self.n_short, self.short_len = mode, seq_rows, n_short, short_len


def _load_halo(halo_ref, init_ref, carry_ref, i, tiles_per_seq):
    first = (i % tiles_per_seq) == 0

    @pl.when(first)
    def _():
        halo_ref[...] = init_ref[...]

    @pl.when(jnp.logical_not(first))
    def _():
        halo_ref[...] = carry_ref[...]


def _store_tail(tail, carry_ref, st_ref, i, j, tiles_per_seq):
    carry_ref[...] = tail

    @pl.when((i % tiles_per_seq) == tiles_per_seq - 1)
    def _():
        st_ref[i // tiles_per_seq, j] = tail[HALO - (CONV_W - 1):, :]


def _shift_rows(x, prev_row):
    rows, c = x.shape
    prev8 = jnp.broadcast_to(prev_row, (HALO, c))
    above = jnp.concatenate([prev8, x[:rows - HALO]], axis=0) if rows > HALO else prev8
    last = lax.broadcasted_iota(jnp.int32, (rows, 1), 0) % HALO == HALO - 1
    mixed = jnp.where(last, above, x).reshape(rows // HALO, HALO, c)
    return pltpu.roll(mixed, 1, 1).reshape(rows, c)


class _CausalConv:
    def __init__(self, w_ref, lay, halo, prev2_ref=None, prev1_ref=None):
        w = w_ref[...]
        self.w0, self.w1, self.w2 = w[0:1, :], w[1:2, :], w[2:3, :]
        self.lay, self.tail, self.prev2_ref, self.prev1_ref = lay, halo, prev2_ref, prev1_ref

    def __call__(self, u, r0):
        rs, c = u.shape
        lay, w0, w1 = self.lay, self.w0, self.w1
        um2, um1 = self.tail[HALO - 2:HALO - 1, :], self.tail[HALO - 1:HALO, :]
        s0 = _shift_rows(w0 * u, w0 * um1)
        s1_first = w1 * um1 + w0 * um2
        if lay.mode == "table":
            take = max(0, min(lay.n_short, r0 + rs) - r0)

            def table_rows(ref):
                parts = ([_lane_chunks_load(ref, r0, take)] if take else []) + (
                    [jnp.zeros((rs - take, c), F32)] if take < rs else [])
                return parts[0] if len(parts) == 1 else jnp.concatenate(parts, axis=0)

            p2, p1 = table_rows(self.prev2_ref), table_rows(self.prev1_ref)
            t = r0 + lax.broadcasted_iota(jnp.int32, (rs, 1), 0)
            starts = jnp.where(t < lay.n_short, t % lay.short_len, t - lay.n_short) == 0
            s0 = jnp.where(starts, w0 * p1, s0)
        a = w1 * u + s0
        s1 = _shift_rows(a, s1_first)
        if lay.mode == "table":
            s1 = jnp.where(starts, w1 * p1 + w0 * p2, s1)
        self.tail = u[rs - HALO:, :]
        return self.w2 * u + s1


def _conv_specs(lay, rows, tm, c, nj, init, prev):
    col = lambda j, i: (0, j)
    if lay.mode == "carry":
        nseq = rows // lay.seq_rows
        return ((init,), [pl.BlockSpec((HALO, c), col)],
                [pl.BlockSpec((nseq, nj, CONV_W - 1, c), lambda j, i: (0, 0, 0, 0))],
                [jax.ShapeDtypeStruct((nseq, nj, CONV_W - 1, c), F32)],
                [pltpu.VMEM((HALO, c), F32)] * 2)
    assert tm == rows and lay.short_len >= CONV_W - 1
    nseq = lay.n_short // lay.short_len
    return ((prev, prev), [pl.BlockSpec((nseq, c), col), pl.BlockSpec((nseq, c), lambda j, i: (0, nj + j))],
            [pl.BlockSpec((nseq, c), col)] * 2 + [pl.BlockSpec((HALO, c), col)],
            [jax.ShapeDtypeStruct((nseq, nj * c), F32)] * 2 + [jax.ShapeDtypeStruct((HALO, nj * c), F32)],
            [pltpu.VMEM((c // LANES, lay.n_short, LANES), F32)] * 2 + [pltpu.VMEM((c // LANES, tm, LANES), F32)])


def _lane_chunks_load(ref, r0, rows):
    return jnp.concatenate([ref[k, r0:r0 + rows, :] for k in range(ref.shape[0])], axis=1)


def _lane_chunks_store(ref, r0, x):
    for k in range(ref.shape[0]):
        ref[k, r0:r0 + x.shape[0], :] = x[:, k * LANES:(k + 1) * LANES]


def _fill_tables(lay, tab2_ref, tab1_ref, prev2_ref, prev1_ref):
    nseq = lay.n_short // lay.short_len
    first_rows = pl.ds(0, nseq, stride=lay.short_len)
    for tab_ref, prev_ref in ((tab2_ref, prev2_ref), (tab1_ref, prev1_ref)):
        tab_ref[...] = jnp.zeros(tab_ref.shape, F32)
        for k in range(tab_ref.shape[0]):
            tab_ref[k, first_rows, :] = prev_ref[:, k * LANES:(k + 1) * LANES]


def _store_short_states(lay, rows_ref, tail, last2_ref, last1_ref, tail_ref):
    nseq = lay.n_short // lay.short_len
    for k in range(rows_ref.shape[0]):
        cols = slice(k * LANES, (k + 1) * LANES)
        last2_ref[:, cols] = rows_ref[k, pl.ds(lay.short_len - 2, nseq, stride=lay.short_len), :]
        last1_ref[:, cols] = rows_ref[k, pl.ds(lay.short_len - 1, nseq, stride=lay.short_len), :]
    tail_ref[...] = tail


def _inproj_conv_kernel(lay, tm, cw, tiles_per_seq, *refs):
    n_ref, wb_ref, wc_ref, wh_ref, cwt_ref = refs[:5]
    if lay.mode == "carry":
        init_ref, ca_ref, st_ref, wbf_ref, halo_ref, carry_ref = refs[5:]
    else:
        prev2_ref, prev1_ref, ca_ref, last2_ref, last1_ref, tail_ref, wbf_ref, tab2_ref, tab1_ref, u_ref = refs[5:]
    j, i = pl.program_id(0), pl.program_id(1)

    @pl.when(i == 0)
    def _():
        for part, w_ref in enumerate((wb_ref, wc_ref, wh_ref)):
            _cast_into(wbf_ref, part * cw, 0, w_ref)

    if lay.mode == "carry":
        _load_halo(halo_ref, init_ref, carry_ref, i, tiles_per_seq)
        conv = _CausalConv(cwt_ref, lay, halo_ref[...])
    else:
        _fill_tables(lay, tab2_ref, tab1_ref, prev2_ref, prev1_ref)
        conv = _CausalConv(cwt_ref, lay, jnp.zeros((HALO, cw), F32), tab2_ref, tab1_ref)

    def matmul(r0, rs):
        return _dot_nt(n_ref[r0:r0 + rs, :], wbf_ref[...])

    def epilogue(r0, rs, p):
        u = p[:, cw:2 * cw] * p[:, 2 * cw:]
        if lay.mode == "table":
            _lane_chunks_store(u_ref, r0, u)
        ca_ref[r0:r0 + rs, :] = (p[:, :cw] * conv(u, r0)).astype(BF16)

    _sliced(tm, matmul, epilogue)
    if lay.mode == "carry":
        _store_tail(conv.tail, carry_ref, st_ref, i, j, tiles_per_seq)
    else:
        _store_short_states(lay, u_ref, conv.tail, last2_ref, last1_ref, tail_ref)


def _inproj_conv(n, w_in_t, conv_w, lay, tm, cw, init=None, prev=None):
    rows, d = n.shape
    dc = conv_w.shape[1]
    nj = dc // cw
    tiles_per_seq = (lay.seq_rows // tm) if lay.mode == "carry" else 1
    xargs, xspecs, xout_specs, xout_shapes, xscratch = _conv_specs(lay, rows, tm, cw, nj, init, prev)
    wspec = lambda part: pl.BlockSpec((None, cw, d), lambda j, i: (0, part * nj + j, 0))
    return pl.pallas_call(
        functools.partial(_inproj_conv_kernel, lay, tm, cw, tiles_per_seq),
        grid=(nj, rows // tm),
        in_specs=[pl.BlockSpec((tm, d), lambda j, i: (i, 0)), wspec(0), wspec(1), wspec(2),
                  pl.BlockSpec((CONV_W, cw), lambda j, i: (0, j))] + xspecs,
        out_specs=[pl.BlockSpec((tm, cw), lambda j, i: (i, j))] + xout_specs,
        out_shape=[jax.ShapeDtypeStruct((rows, dc), BF16)] + xout_shapes,
        scratch_shapes=[pltpu.VMEM((3 * cw, d), BF16)] + xscratch,
        compiler_params=_cparams("arbitrary", "arbitrary"),
        name="inproj_conv_" + lay.mode)(n, w_in_t, w_in_t, w_in_t, conv_w, *xargs)


def _inproj_act_kernel(tm, n_lin, n_silu, q_scale, n_ref, w_ref, o_ref, wbf_ref):
    j, i = pl.program_id(0), pl.program_id(1)

    @pl.when(i == 0)
    def _():
        _cast_into(wbf_ref, 0, 0, w_ref.at[0] if len(w_ref.shape) == 3 else w_ref)

    def matmul(r0, rs):
        return _dot_nt(n_ref[r0:r0 + rs, :], wbf_ref[...])

    def epilogue(r0, rs, p):
        if n_lin:
            act = p * jnp.where(j == 0, q_scale, 1.0)
        else:
            act = jnp.where(j < n_silu, _silu(p), _sigmoid(p)) if n_silu else _sigmoid(p)
        o_ref[r0:r0 + rs, :] = act.astype(BF16)

    _sliced(tm, matmul, epilogue)


def _inproj_act(n, w_t, w_spec, tiles, tm, tn, q_scale):
    rows, d = n.shape
    n_lin, n_silu, n_sig = tiles
    assert not (n_lin and (n_silu or n_sig))
    n_col = n_lin + n_silu + n_sig
    return pl.pallas_call(
        functools.partial(_inproj_act_kernel, tm, n_lin, n_silu, q_scale),
        grid=(n_col, rows // tm),
        in_specs=[pl.BlockSpec((tm, d), lambda j, i: (i, 0)), w_spec],
        out_specs=pl.BlockSpec((tm, tn), lambda j, i: (i, j)),
        out_shape=jax.ShapeDtypeStruct((rows, n_col * tn), BF16),
        scratch_shapes=[pltpu.VMEM((tn, d), BF16)],
        compiler_params=_cparams("arbitrary", "arbitrary"), name="inproj_act")(n, w_t)


def _cumsum_groups(x, sub):
    rows = x.shape[0]
    pos = lax.broadcasted_iota(jnp.int32, (rows, 1), 0) % sub
    s = 1
    while s < sub:
        x = x + jnp.where(pos >= s, pltpu.roll(x, s, 0), 0.0)
        s *= 2
    return x


def _gla_chunk(q, k, v, lg, s, sub):
    c, dk = q.shape
    ngrp = c // sub
    gpu = 2 if ngrp % 2 == 0 else 1
    unit = gpu * sub
    bt = _cumsum_groups(lg, sub)
    r = jnp.zeros((1, dk), F32)
    r_grp, b_rows = [], []
    for g in range(ngrp):
        r_grp.append(r)
        b_rows.append(bt[g * sub:(g + 1) * sub, :] + r)
        r = r + bt[(g + 1) * sub - 1:(g + 1) * sub, :]
    b = jnp.concatenate(b_rows, axis=0) if ngrp > 1 else b_rows[0]
    b_last = r
    qe = (q * jnp.exp(b)).astype(BF16)
    kd = (k * jnp.exp(b_last - b)).astype(BF16)
    att_rows = []
    for i in range(c // unit):
        ref = r_grp[i * gpu + gpu - 1]
        rows = slice(i * unit, (i + 1) * unit)
        seen = (i + 1) * unit
        qt = (q[rows] * jnp.exp(b[rows] - ref)).astype(BF16)
        ke = (k[:seen] * jnp.exp(ref - b[:seen])).astype(BF16)
        if seen < c:
            ke = jnp.concatenate([ke, jnp.zeros((c - seen, dk), BF16)], axis=0)
        a = _dot_nt(qt, ke)
        col = lax.broadcasted_iota(jnp.int32, (unit, c), 1)
        rloc = lax.broadcasted_iota(jnp.int32, (unit, c), 0)
        att_rows.append(jnp.where(col <= rloc + i * unit, a, 0.0))
    att = (jnp.concatenate(att_rows, axis=0) if len(att_rows) > 1 else att_rows[0]).astype(BF16)
    o = _dot(att, v) + _dot(qe, s.astype(BF16))
    upd = lax.dot_general(kd, v, (((0,), (0,)), ((), ())), preferred_element_type=F32)
    dl = jnp.exp(b_last)
    dl_col = jnp.transpose(jnp.broadcast_to(dl, (LANES, dk)))
    dv = s.shape[1]
    s_dec = jnp.concatenate([s[:, n * LANES:(n + 1) * LANES] * dl_col for n in range(dv // LANES)], axis=1)
    return o, s_dec + upd


def _gla_out(o, gn, gs):
    return (_rms(o, gn) * gs.astype(F32)).astype(BF16)


def _gla_seq_kernel(sub, heads, dk, dv, *refs):
    _gla_seq_body(sub, heads, dk, dv, pl.program_id(1) == 0, *refs)


def _gla_seq_body(sub, heads, dk, dv, first_chunk, q_ref, k_ref, v_ref, lg_ref, gs_ref, gn_ref, s0_ref, ob_ref, s_ref):
    @pl.when(first_chunk)
    def _():
        def copy_head(h, carry):
            s_ref[0, h] = s0_ref[0, h]
            return carry

        lax.fori_loop(0, heads, copy_head, 0)

    for h in range(heads):
        _gla_seq_head(sub, dk, dv, h, q_ref, k_ref, v_ref, lg_ref, gs_ref, gn_ref, ob_ref, s_ref)


def _gla_seq_head(sub, dk, dv, h, q_ref, k_ref, v_ref, lg_ref, gs_ref, gn_ref, ob_ref, s_ref):
    ck, cv = slice(h * dk, (h + 1) * dk), slice(h * dv, (h + 1) * dv)
    o, s_new = _gla_chunk(q_ref[:, ck].astype(F32), k_ref[:, ck].astype(F32), v_ref[:, cv], lg_ref[:, ck],
                          s_ref[0, h], sub)
    s_ref[0, h] = s_new
    ob_ref[:, cv] = _gla_out(o, gn_ref[...], gs_ref[:, cv])


def _gla_seq(lin, act, logg, gn, s0, heads, dk, dv, nseq, seq_rows, chunk, sub, row0, shared_init):
    nchunk = seq_rows // chunk
    blk0 = row0 // chunk
    rb = lambda b, c: blk0 + b * nchunk + c
    wk, wv = heads * dk, heads * dv
    assert (2 * wk) % wv == 0
    return pl.pallas_call(
        functools.partial(_gla_seq_kernel, sub, heads, dk, dv),
        grid=(nseq, nchunk),
        in_specs=[pl.BlockSpec((chunk, wk), lambda b, c: (rb(b, c), 0)),
                  pl.BlockSpec((chunk, wk), lambda b, c: (rb(b, c), 1)),
                  pl.BlockSpec((chunk, wv), lambda b, c: (rb(b, c), 2 * wk // wv)),
                  pl.BlockSpec((chunk, wk), lambda b, c: (rb(b, c), 0)),
                  pl.BlockSpec((chunk, wv), lambda b, c: (rb(b, c), 0)),
                  pl.BlockSpec((1, dv), lambda b, c: (0, 0)),
                  pl.BlockSpec((1, heads, dk, dv), lambda b, c: (0 if shared_init else b, 0, 0, 0))],
        out_specs=[pl.BlockSpec((chunk, wv), lambda b, c: (b * nchunk + c, 0)),
                   pl.BlockSpec((1, heads, dk, dv), lambda b, c: (b, 0, 0, 0))],
        out_shape=[jax.ShapeDtypeStruct((nseq * seq_rows, wv), BF16),
                   jax.ShapeDtypeStruct((nseq, heads, dk, dv), F32)],
        compiler_params=_cparams("arbitrary", "arbitrary"),
        name="gla_seq")(lin, lin, lin, logg, act, gn, s0)


def _gla_short_kernel(nb, sl, q_ref, k_ref, v_ref, lg_ref, gs_ref, gn_ref, s0_ref, ob_ref, s_ref):
    q = q_ref[...].astype(F32)
    k = k_ref[...].astype(F32)
    v = v_ref[...].astype(F32)
    lg = lg_ref[...]
    outs = []
    for n in range(nb):
        rs = slice(n * sl, (n + 1) * sl)
        o, s_new = _gla_chunk(q[rs], k[rs], v[rs].astype(BF16), lg[rs], s0_ref[n, 0], sl)
        s_ref[n, 0] = s_new
        outs.append(o)
    o = jnp.concatenate(outs, axis=0)
    ob_ref[...] = _gla_out(o, gn_ref[...], gs_ref[...])


def _gla_both_kernel(sub, heads, dk, dv, nchunk, nb, sl, n_long, n_short, *refs):
    (ql, kl, vl, lgl, gsl, gn, s0l, qs, ks, vs, lgs, gss, s0s, obl, sl_out, obs, ss_out) = refs
    step = pl.program_id(0)

    def long_part():
        _gla_seq_body(sub, heads, dk, dv, (step % nchunk) == 0, ql, kl, vl, lgl, gsl, gn, s0l, obl, sl_out)

    def short_part():
        _gla_short_kernel(nb, sl, qs, ks, vs, lgs, gss, gn, s0s, obs, ss_out)

    if n_long == n_short and nb % heads == 0:
        @pl.when((step % nchunk) == 0)
        def _():
            def copy_head(h, carry):
                sl_out[0, h] = s0l[0, h]
                return carry

            lax.fori_loop(0, heads, copy_head, 0)

        q, k, v, lg = qs[...].astype(F32), ks[...].astype(F32), vs[...].astype(F32), lgs[...]
        per = nb // heads
        outs = []
        for h in range(heads):
            _gla_seq_head(sub, dk, dv, h, ql, kl, vl, lgl, gsl, gn, obl, sl_out)
            for n in range(h * per, (h + 1) * per):
                rs = slice(n * sl, (n + 1) * sl)
                o, s_new = _gla_chunk(q[rs], k[rs], v[rs].astype(BF16), lg[rs], s0s[n, 0], sl)
                ss_out[n, 0] = s_new
                outs.append(o)
        obs[...] = _gla_out(jnp.concatenate(outs, axis=0), gn[...], gss[...])
    else:
        pl.when(step < n_long)(long_part)
        pl.when(step < n_short)(short_part)


def _gla_both(lin_l, act_l, logg_l, s0_l, nseq_l, seq_rows, chunk, sub,
              lin_s, act_s, logg_s, s0_s, nseq_s, sl, nb, gn, heads, dk, dv):
    nchunk = seq_rows // chunk
    n_long, n_short = nseq_l * nchunk, (nseq_s // nb) * heads
    steps = max(n_long, n_short)
    wk, wv = heads * dk, heads * dv
    assert (2 * wk) % wv == 0
    rows = nb * sl
    kv = (2 * heads * dk) // dv
    lo = (lambda s: s) if n_long == steps else (lambda s: jnp.minimum(s, n_long - 1))
    sh = (lambda s: s) if n_short == steps else (lambda s: jnp.minimum(s, n_short - 1))
    long_row = lambda col: (lambda s: (lo(s), col))
    short_blk = lambda off: (lambda s: (sh(s) // heads, off + sh(s) % heads))
    return pl.pallas_call(
        functools.partial(_gla_both_kernel, sub, heads, dk, dv, nchunk, nb, sl, n_long, n_short),
        grid=(steps,),
        in_specs=[pl.BlockSpec((chunk, wk), long_row(0)), pl.BlockSpec((chunk, wk), long_row(1)),
                  pl.BlockSpec((chunk, wv), long_row(2 * wk // wv)), pl.BlockSpec((chunk, wk), long_row(0)),
                  pl.BlockSpec((chunk, wv), long_row(0)),
                  pl.BlockSpec((1, dv), lambda s: (0, 0)),
                  pl.BlockSpec((1, heads, dk, dv), lambda s: (0, 0, 0, 0)),
                  pl.BlockSpec((rows, dk), short_blk(0)), pl.BlockSpec((rows, dk), short_blk(heads)),
                  pl.BlockSpec((rows, dv), short_blk(kv)), pl.BlockSpec((rows, dk), short_blk(0)),
                  pl.BlockSpec((rows, dv), short_blk(0)),
                  pl.BlockSpec((nb, 1, dk, dv), lambda s: (sh(s) // heads, sh(s) % heads, 0, 0))],
        out_specs=[pl.BlockSpec((chunk, wv), long_row(0)),
                   pl.BlockSpec((1, heads, dk, dv), lambda s: (lo(s) // nchunk, 0, 0, 0)),
                   pl.BlockSpec((rows, dv), short_blk(0)),
                   pl.BlockSpec((nb, 1, dk, dv), lambda s: (sh(s) // heads, sh(s) % heads, 0, 0))],
        out_shape=[jax.ShapeDtypeStruct((nseq_l * seq_rows, wv), BF16),
                   jax.ShapeDtypeStruct((nseq_l, heads, dk, dv), F32),
                   jax.ShapeDtypeStruct((nseq_s * sl, wv), BF16),
                   jax.ShapeDtypeStruct((nseq_s, heads, dk, dv), F32)],
        compiler_params=_cparams("arbitrary"),
        name="gla_both")(lin_l, lin_l, lin_l, logg_l, act_l, gn, s0_l, lin_s, lin_s, lin_s, logg_s, act_s, s0_s)


def _merge_kernel(ca_ref, ob_ref, ga_ref, gb_ref, wc_ref, wg_ref, o_ref):
    ya = _dot(ca_ref[...], wc_ref[...])
    yb = _dot(ob_ref[...], wg_ref[...])
    o_ref[...] = (ga_ref[...].astype(F32) * ya + gb_ref[...].astype(F32) * yb).astype(BF16)


def _merge(ca, ob, act, wc, wg, tm, tn):
    rows, dc = ca.shape
    dg, d = wg.shape
    nn = d // tn
    g0 = (act.shape[1] - 2 * d) // tn
    return pl.pallas_call(
        _merge_kernel, grid=(rows // tm, nn),
        in_specs=[pl.BlockSpec((tm, dc), lambda i, j: (i, 0)), pl.BlockSpec((tm, dg), lambda i, j: (i, 0)),
                  pl.BlockSpec((tm, tn), lambda i, j: (i, g0 + j)),
                  pl.BlockSpec((tm, tn), lambda i, j: (i, g0 + nn + j)),
                  pl.BlockSpec((dc, tn), lambda i, j: (0, j)), pl.BlockSpec((dg, tn), lambda i, j: (0, j))],
        out_specs=pl.BlockSpec((tm, tn), lambda i, j: (i, j)),
        out_shape=jax.ShapeDtypeStruct((rows, d), BF16),
        compiler_params=_cparams("arbitrary", "arbitrary"), name="merge")(ca, ob, act, act, wc, wg)


def _oproj_kernel(m_ref, x_ref, w_ref, g_ref, h_ref, n2_ref):
    h = x_ref[...] + _dot(m_ref[...], w_ref[...])
    h_ref[...] = h
    n2_ref[...] = _rms(h, g_ref[...]).astype(BF16)


def _oproj(m, x, w, g, tm):
    rows, d = x.shape
    row = lambda i: (i, 0)
    return pl.pallas_call(
        _oproj_kernel, grid=(rows // tm,),
        in_specs=[pl.BlockSpec((tm, d), row), pl.BlockSpec((tm, d), row),
                  pl.BlockSpec((d, d), lambda i: (0, 0)), pl.BlockSpec((1, d), lambda i: (0, 0))],
        out_specs=[pl.BlockSpec((tm, d), row), pl.BlockSpec((tm, d), row)],
        out_shape=[jax.ShapeDtypeStruct((rows, d), F32), jax.ShapeDtypeStruct((rows, d), BF16)],
        compiler_params=_cparams("arbitrary"), name="oproj")(m, x, w, g)


def _ffn_up_kernel(lay, tm, fw, tiles_per_seq, *refs):
    n_ref, wa_ref, wg_ref, cwt_ref, b_ref = refs[:5]
    if lay.mode == "carry":
        init_ref, act_ref, st_ref, wbf_ref, halo_ref, carry_ref = refs[5:]
    else:
        prev2_ref, prev1_ref, act_ref, last2_ref, last1_ref, tail_ref, wbf_ref, tab2_ref, tab1_ref, gt_ref = refs[5:]
    j, i = pl.program_id(0), pl.program_id(1)

    @pl.when(i == 0)
    def _():
        _cast_into(wbf_ref, 0, 0, wa_ref)
        _cast_into(wbf_ref, 0, fw, wg_ref)

    if lay.mode == "carry":
        _load_halo(halo_ref, init_ref, carry_ref, i, tiles_per_seq)
        conv = _CausalConv(cwt_ref, lay, halo_ref[...])
    else:
        _fill_tables(lay, tab2_ref, tab1_ref, prev2_ref, prev1_ref)
        conv = _CausalConv(cwt_ref, lay, jnp.zeros((HALO, fw), F32), tab2_ref, tab1_ref)

    def matmul(r0, rs):
        return _dot(n_ref[r0:r0 + rs, :], wbf_ref[...])

    def epilogue(r0, rs, p):
        gt = p[:, fw:]
        if lay.mode == "table":
            _lane_chunks_store(gt_ref, r0, gt)
        z = conv(gt, r0) + b_ref[...]
        act_ref[r0:r0 + rs, :] = (_silu(z) * p[:, :fw]).astype(BF16)

    _sliced(tm, matmul, epilogue)
    if lay.mode == "carry":
        _store_tail(conv.tail, carry_ref, st_ref, i, j, tiles_per_seq)
    else:
        _store_short_states(lay, gt_ref, conv.tail, last2_ref, last1_ref, tail_ref)


def _ffn_up(n2, w_up, conv_w, bias, lay, tm, fw, init=None, prev=None):
    rows, d = n2.shape
    dff = conv_w.shape[1]
    nj = dff // fw
    tiles_per_seq = (lay.seq_rows // tm) if lay.mode == "carry" else 1
    xargs, xspecs, xout_specs, xout_shapes, xscratch = _conv_specs(lay, rows, tm, fw, nj, init, prev)
    wspec = lambda part: pl.BlockSpec((None, d, fw), lambda j, i: (0, 0, part * nj + j))
    return pl.pallas_call(
        functools.partial(_ffn_up_kernel, lay, tm, fw, tiles_per_seq),
        grid=(nj, rows // tm),
        in_specs=[pl.BlockSpec((tm, d), lambda j, i: (i, 0)), wspec(0), wspec(1),
                  pl.BlockSpec((CONV_W, fw), lambda j, i: (0, j)), pl.BlockSpec((1, fw), lambda j, i: (0, j))]
                 + xspecs,
        out_specs=[pl.BlockSpec((tm, fw), lambda j, i: (i, j))] + xout_specs,
        out_shape=[jax.ShapeDtypeStruct((rows, dff), BF16)] + xout_shapes,
        scratch_shapes=[pltpu.VMEM((d, 2 * fw), BF16)] + xscratch,
        compiler_params=_cparams("arbitrary", "arbitrary"),
        name="ffn_up_" + lay.mode)(n2, w_up, w_up, conv_w, bias, *xargs)


def _ffn_down_kernel(act_ref, w_ref, h_ref, g_ref, y_ref):
    y_ref[...] = _rms(h_ref[...] + _dot(act_ref[...], w_ref[...]), g_ref[...])


def _ffn_down(act, w, h, g, rows, tm):
    dff = act.shape[1]
    d = w.shape[1]
    row = lambda i: (i, 0)
    return pl.pallas_call(
        _ffn_down_kernel, grid=(rows // tm,),
        in_specs=[pl.BlockSpec((tm, dff), row),
                  pl.BlockSpec((dff, d), lambda i: (0, 0), pipeline_mode=pl.Buffered(1)),
                  pl.BlockSpec((tm, d), row), pl.BlockSpec((1, d), lambda i: (0, 0))],
        out_specs=pl.BlockSpec((tm, d), row),
        out_shape=jax.ShapeDtypeStruct((rows, d), F32),
        compiler_params=_cparams("arbitrary"), name="ffn_down")(act, w, h, g)


def kernel(x_prompt, x_sample, state_conv, state_gla, state_ffn_conv, meta_tokens, norm_mix_g, w_in, conv_mix_w, w_conv_out, w_gate_up, b_gate, gla_norm_g, w_gla_out, w_o, norm_ffn_g, w_ffn_up, ffn_conv_w, ffn_conv_b, w_ffn_down, final_norm_g):
    bp, seq, d = x_prompt.shape
    bs, sl, _ = x_sample.shape
    assert w_in.shape[0] == 1, "single-layer step"
    n_meta = meta_tokens.shape[0]
    dc = state_conv.shape[-1]
    _, _, heads, dk, dv = state_gla.shape
    dff = state_ffn_conv.shape[-1]
    rank = w_gate_up.shape[1]
    assert n_meta % GLA_SUB == 0 and seq % GLA_SUB == 0 and GLA_SUB % sl == 0 and sl >= CONV_W - 1

    o_q = 3 * dc
    o_a = o_q + 2 * heads * dk + 2 * heads * dv
    o_ga = o_a + rank
    tn = heads * dk
    assert o_q % tn == 0 and (heads * dv) % tn == 0 and d % tn == 0
    cw, fw = CONV_COLS, FFN_COLS
    o_g = o_a - heads * dv
    n_g = heads * dv // tn
    tiles_qkv = ((o_g - o_q) // tn, 0, 0)
    tiles_act = (0, n_g, 2 * d // tn)
    spec_qkv = pl.BlockSpec((None, tn, d), lambda j, i: (0, o_q // tn + j, 0))
    assert o_g % 8 == 0 and o_ga % 8 == 0 and tn % 8 == 0
    act_row8 = lambda j: jnp.where(j < n_g, o_g // 8 + j * (tn // 8), o_ga // 8 + (j - n_g) * (tn // 8))
    spec_act = pl.BlockSpec((pl.Element(1), pl.Element(tn), pl.Element(d)), lambda j, i: (0, act_row8(j) * 8, 0))
    q_scale = float(dk) ** -0.5

    w_in_t = jnp.swapaxes(w_in, 1, 2)
    wa = jnp.pad(w_in_t[0, o_a:o_ga], ((0, LANES - rank), (0, 0))).astype(BF16)
    wup = jnp.pad(w_gate_up[0], ((0, LANES - rank), (0, 0))).astype(BF16)
    wc = w_conv_out[0].astype(BF16)
    wg = w_gla_out[0].astype(BF16)
    wo = w_o[0].astype(BF16)
    wd = w_ffn_down[0].astype(BF16)
    g1, g2, gf, gn = norm_mix_g[0][None], norm_ffn_g[0][None], final_norm_g[None], gla_norm_g[0][None]
    bg, fb, cmw, fcw = b_gate[0][None], ffn_conv_b[0][None], conv_mix_w[0], ffn_conv_w[0]

    def in_projections(n, tm, lay, **conv_kw):
        conv_outs = _inproj_conv(n, w_in_t, cmw, lay, tm, cw, **conv_kw)
        lin = _inproj_act(n, w_in_t, spec_qkv, tiles_qkv, tm, tn, q_scale)
        act = _inproj_act(n, w_in_t, spec_act, tiles_act, tm, tn, 1.0)
        return conv_outs, lin, act

    n_short = bs * sl
    rows_s = n_short + n_meta
    xs = jnp.concatenate([x_sample.reshape(n_short, d), meta_tokens.astype(x_sample.dtype)], axis=0)
    lay_s = SeqLayout("table", n_short=n_short, short_len=sl)
    tm_s = rows_s
    n_s, logg_s = _rmsnorm_gate(xs, g1, wa, wup, bg, _row_tile(rows_s, ROW_TILE_RESIDENT))
    (ca_s, conv_s2, conv_s1, conv_tail), lin_s, gates_s = in_projections(
        n_s, tm_s, lay_s, prev=state_conv[0].reshape(bs, (CONV_W - 1) * dc))
    s_zero = jnp.zeros((1, heads, dk, dv), F32)
    ob_m, s_meta = _gla_seq(lin_s, gates_s, logg_s, gn, s_zero, heads, dk, dv, 1, n_meta, n_meta, GLA_SUB,
                            n_short, True)

    rows_p = bp * seq
    xp = x_prompt.reshape(rows_p, d)
    lay_p = SeqLayout("carry", seq_rows=seq)
    tm_p = _row_tile(seq, ROW_TILE)
    tm_o = _row_tile(seq, ROW_TILE_RESIDENT)
    n_p, logg_p = _rmsnorm_gate(xp, g1, wa, wup, bg, _row_tile(seq, ROW_TILE_NORM))
    (ca_p, conv_p), lin_p, gates_p = in_projections(n_p, tm_p, lay_p, init=conv_tail)
    ob_p, s_p, ob_smp, s_smp = _gla_both(
        lin_p, gates_p, logg_p, s_meta, bp, seq, _row_tile(seq, GLA_CHUNK), GLA_SUB,
        lin_s, gates_s, logg_s, state_gla[0], bs, sl, SHORT_SEQS_PER_STEP, gn, heads, dk, dv)

    ob_s = jnp.concatenate([ob_smp, ob_m], axis=0)
    m_s = _merge(ca_s, ob_s, gates_s, wc, wg, tm_s, tn)
    h_s, n2_s = _oproj(m_s, xs, wo, g2, _row_tile(rows_s, ROW_TILE_RESIDENT))
    act_s, ffn_s2, ffn_s1, ffn_tail = _ffn_up(n2_s, w_ffn_up, fcw, fb, lay_s, tm_s, fw,
                                              prev=state_ffn_conv[0].reshape(bs, (CONV_W - 1) * dff))
    y_s = _ffn_down(act_s, wd, h_s, gf, n_short, _row_tile(n_short, ROW_TILE_WIDE_K))

    m_p = _merge(ca_p, ob_p, gates_p, wc, wg, _row_tile(seq, ROW_TILE_MERGE), tn)
    h_p, n2_p = _oproj(m_p, xp, wo, g2, tm_o)
    act_p, ffn_p = _ffn_up(n2_p, w_ffn_up, fcw, fb, lay_p, tm_p, fw, init=ffn_tail)
    y_p = _ffn_down(act_p, wd, h_p, gf, rows_p, _row_tile(seq, ROW_TILE_WIDE_K))

    untile = lambda st: jnp.swapaxes(st, 1, 2).reshape(st.shape[0], CONV_W - 1, -1)
    return (y_p.reshape(bp, seq, d), y_s.reshape(bs, sl, d),
            untile(conv_p)[None], s_p[None], untile(ffn_p)[None],
            jnp.stack([conv_s2, conv_s1], axis=1)[None], s_smp[None], jnp.stack([ffn_s2, ffn_s1], axis=1)[None])
```
